```python
import jax, jax.numpy as jnp
from jax import lax
import numpy as np

D_MODEL = 2048
BATCH = 4
SEQ = 4096
DEPTH = 4

GRID_W = 64
CTX_LEN = 256
N_MIXERS = 2
N_LAYERS_NA = (DEPTH + 1) // 2
N_LAYERS_GDN = DEPTH // 2

NA_HEADS = 16
NA_HEAD_DIM = D_MODEL // NA_HEADS
NA_WIN_H = 8
NA_WIN_W = 16
NA_SCALE = NA_HEAD_DIM ** -0.5

GDN_K_HEADS = 16
GDN_V_HEADS = 32
GDN_HEAD_K = 128
GDN_HEAD_V = 128
GDN_KEY_DIM = GDN_K_HEADS * GDN_HEAD_K
GDN_VALUE_DIM = GDN_V_HEADS * GDN_HEAD_V
GDN_CONV_DIM = 2 * GDN_KEY_DIM + GDN_VALUE_DIM
GDN_CONV = 4
GDN_CHUNK = 64
GDN_SCALE = GDN_HEAD_K ** -0.5

D_FF = -(-8 * D_MODEL // (3 * 256)) * 256

ADA_SCALE = 0.5
EPS = 1e-6
NEG_INF = -1e30

kernel_name = "hybrid_natten_gdeltanet_dit"


def rms_norm(x, w):
    xf = x.astype(jnp.float32)
    y = xf * lax.rsqrt(jnp.mean(xf * xf, axis=-1, keepdims=True) + EPS)
    return (y * w.astype(jnp.float32)).astype(x.dtype)


def l2_norm(x):
    xf = x.astype(jnp.float32)
    return xf * lax.rsqrt(jnp.sum(xf * xf, axis=-1, keepdims=True) + EPS)


def modulate(h, shift, scale):
    return h * (1 + scale) + shift


def swiglu(h, w1, w3, w2):
    return (jax.nn.silu(h @ w1) * (h @ w3)) @ w2


def centred_dwconv(x, w):
    k = w.shape[0]
    return lax.conv_general_dilated(
        x, w[:, None, :], window_strides=(1,), padding=(((k - 1) // 2, k // 2),),
        dimension_numbers=("NWC", "WIO", "NWC"), feature_group_count=x.shape[-1])


def na_mixer(h, hc, w_qkv, w_o, q_gain, k_gain, rpb, update_ctx):
    bsz, seq, _ = h.shape
    rows = seq // GRID_W
    kh = min(NA_WIN_H, rows)

    def project(t):
        qkv = (t @ w_qkv).reshape(t.shape[0], t.shape[1], 3, NA_HEADS, NA_HEAD_DIM)
        return rms_norm(qkv[:, :, 0], q_gain), rms_norm(qkv[:, :, 1], k_gain), qkv[:, :, 2]

    q, k, v = project(h)
    qc, kc, vc = project(hc)

    def to_grid(t):
        return t.reshape(bsz, rows, GRID_W, NA_HEADS, NA_HEAD_DIM).transpose(0, 3, 1, 2, 4)

    q, k, v = to_grid(q), to_grid(k), to_grid(v)
    kc_h = kc.transpose(0, 2, 1, 3)
    vc_h = vc.transpose(0, 2, 1, 3)

    cols = jnp.arange(GRID_W)
    col_start = jnp.clip(cols - NA_WIN_W // 2, 0, GRID_W - NA_WIN_W)
    in_win = (cols[None, :] >= col_start[:, None]) & (cols[None, :] < col_start[:, None] + NA_WIN_W)
    dc = jnp.clip(cols[None, :] - cols[:, None], 1 - NA_WIN_W, NA_WIN_W - 1) + NA_WIN_W - 1
    bias_cols = jnp.where(in_win[None, None], rpb[:, :, dc].astype(jnp.float32), NEG_INF)

    def row_block(r):
        rs = jnp.clip(r - kh // 2, 0, rows - kh)
        q_r = lax.dynamic_index_in_dim(q, r, axis=2, keepdims=False)
        k_b = lax.dynamic_slice_in_dim(k, rs, kh, axis=2)
        v_b = lax.dynamic_slice_in_dim(v, rs, kh, axis=2)
        bias = jnp.take(bias_cols, rs - r + jnp.arange(kh) + NA_WIN_H - 1, axis=1)
        s_loc = jnp.einsum("bhqd,bhjkd->bhqjk", q_r, k_b,
                           preferred_element_type=jnp.float32) * NA_SCALE + bias.transpose(0, 2, 1, 3)
        s_ctx = jnp.einsum("bhqd,bhcd->bhqc", q_r, kc_h,
                           preferred_element_type=jnp.float32) * NA_SCALE
        s = jnp.concatenate([s_loc.reshape(bsz, NA_HEADS, GRID_W, kh * GRID_W), s_ctx], axis=-1)
        p = jax.nn.softmax(s, axis=-1).astype(v.dtype)
        p_loc = p[..., :kh * GRID_W].reshape(bsz, NA_HEADS, GRID_W, kh, GRID_W)
        return (jnp.einsum("bhqjk,bhjkd->bhqd", p_loc, v_b)
                + jnp.einsum("bhqc,bhcd->bhqd", p[..., kh * GRID_W:], vc_h))

    o = lax.map(row_block, jnp.arange(rows))
    y = o.transpose(1, 0, 3, 2, 4).reshape(bsz, seq, D_MODEL) @ w_o

    yc = None
    if update_ctx:
        sc = jnp.einsum("bqhd,bkhd->bhqk", qc, kc, preferred_element_type=jnp.float32) * NA_SCALE
        pc = jax.nn.softmax(sc, axis=-1).astype(vc.dtype)
        yc = jnp.einsum("bhqk,bkhd->bqhd", pc, vc).reshape(bsz, hc.shape[1], D_MODEL) @ w_o
    return y, yc


def chunk_gated_delta(q, k, v, g, beta, state0):
    bsz, nh, tlen, dk = q.shape
    n = tlen // GDN_CHUNK
    cl = GDN_CHUNK

    def chunks(t):
        return t.reshape(bsz, nh, n, cl, *t.shape[3:])

    q, k, v, g, beta = chunks(q), chunks(k), chunks(v), chunks(g), chunks(beta)
    g = jnp.cumsum(g, axis=-1)
    lower = jnp.tril(jnp.ones((cl, cl), dtype=bool))
    strict = jnp.tril(jnp.ones((cl, cl), dtype=bool), -1)
    diff = g[..., :, None] - g[..., None, :]
    decay = jnp.where(lower, jnp.exp(jnp.where(lower, diff, 0.0)), 0.0)
    k_beta = k * beta[..., None]
    v_beta = v * beta[..., None]
    l_mat = jnp.where(strict, jnp.einsum("bhnid,bhnjd->bhnij", k_beta, k) * decay, 0.0)
    eye = jnp.eye(cl, dtype=jnp.float32)
    t_inv = lax.linalg.triangular_solve(eye + l_mat, jnp.broadcast_to(eye, l_mat.shape),
                                        left_side=True, lower=True, unit_diagonal=True)
    u = jnp.einsum("bhnij,bhnjd->bhnid", t_inv, v_beta)
    w = jnp.einsum("bhnij,bhnjd->bhnid", t_inv, k_beta * jnp.exp(g)[..., None])
    a_intra = jnp.where(lower, jnp.einsum("bhnid,bhnjd->bhnij", q, k) * decay, 0.0)
    g_last = g[..., -1]
    k_dec = k * jnp.exp(g_last[..., None] - g)[..., None]
    q_dec = q * jnp.exp(g)[..., None]

    def step(s, xs):
        u_i, w_i, q_i, kd_i, a_i, gl_i = xs
        v_new = u_i - jnp.einsum("bhcd,bhde->bhce", w_i, s)
        o = jnp.einsum("bhcd,bhde->bhce", q_i, s) + jnp.einsum("bhcs,bhse->bhce", a_i, v_new)
        s = s * jnp.exp(gl_i)[..., None, None] + jnp.einsum("bhcd,bhce->bhde", kd_i, v_new)
        return s, o

    xs = tuple(jnp.moveaxis(t, 2, 0) for t in (u, w, q_dec, k_dec, a_intra, g_last))
    s_final, o = lax.scan(step, state0, xs)
    return s_final, jnp.moveaxis(o, 0, 2).reshape(bsz, nh, tlen, v.shape[-1])


def gdn_mixer(h, hc, w_in, conv_w, w_ba, a_log, dt_bias, norm_w, w_o, update_ctx):
    rep = GDN_V_HEADS // GDN_K_HEADS

    def project(t):
        bsz, tlen, _ = t.shape
        qkvz = t @ w_in
        qkv = jax.nn.silu(centred_dwconv(qkvz[..., :GDN_CONV_DIM], conv_w))
        z = qkvz[..., GDN_CONV_DIM:]
        q = qkv[..., :GDN_KEY_DIM].reshape(bsz, tlen, GDN_K_HEADS, GDN_HEAD_K)
        k = qkv[..., GDN_KEY_DIM:2 * GDN_KEY_DIM].reshape(bsz, tlen, GDN_K_HEADS, GDN_HEAD_K)
        v = qkv[..., 2 * GDN_KEY_DIM:].reshape(bsz, tlen, GDN_V_HEADS, GDN_HEAD_V)
        q = jnp.repeat(l2_norm(q) * GDN_SCALE, rep, axis=2)
        k = jnp.repeat(l2_norm(k), rep, axis=2)
        ba = jnp.einsum("btd,zde->zbte", t, w_ba).astype(jnp.float32)
        beta = jax.nn.sigmoid(ba[..., :GDN_V_HEADS])
        g = -jnp.exp(a_log.astype(jnp.float32))[:, None, None, :] * jax.nn.softplus(
            ba[..., GDN_V_HEADS:] + dt_bias.astype(jnp.float32)[:, None, None, :])
        return (jnp.swapaxes(q, 1, 2), jnp.swapaxes(k, 1, 2),
                jnp.swapaxes(v.astype(jnp.float32), 1, 2), z,
                jnp.swapaxes(g, 2, 3), jnp.swapaxes(beta, 2, 3))

    def gated_out(o, z):
        bsz, _, tlen, _ = o.shape
        zz = z.reshape(bsz, tlen, GDN_V_HEADS, GDN_HEAD_V).astype(jnp.float32)
        y = rms_norm(jnp.swapaxes(o, 1, 2), norm_w) * jax.nn.silu(zz)
        return y.reshape(bsz, tlen, GDN_VALUE_DIM).astype(h.dtype) @ w_o

    def flip(t):
        return jnp.flip(t, axis=2)

    q, k, v, z, g, beta = project(h)
    qc, kc, vc, zc, gc, betac = project(hc)
    s0 = jnp.zeros((h.shape[0], GDN_V_HEADS, GDN_HEAD_K, GDN_HEAD_V), jnp.float32)
    sc_f, oc_f = chunk_gated_delta(qc, kc, vc, gc[0], betac[0], s0)
    sc_b, oc_b = chunk_gated_delta(flip(qc), flip(kc), flip(vc), flip(gc[1]), flip(betac[1]), s0)
    _, o_f = chunk_gated_delta(q, k, v, g[0], beta[0], sc_f)
    _, o_b = chunk_gated_delta(flip(q), flip(k), flip(v), flip(g[1]), flip(beta[1]), sc_b)
    y = gated_out(o_f + flip(o_b), z)
    yc = gated_out(oc_f + flip(oc_b), zc) if update_ctx else None
    return y, yc


def setup_inputs(seed: int = 0) -> dict:
    key = jax.random.key(seed)
    ks = jax.random.split(key, 24)

    def nrm(k, shape, scale):
        return jax.random.normal(k, shape, jnp.float32) * scale

    d = D_MODEL
    dt = jnp.exp(jax.random.uniform(ks[17], (N_LAYERS_GDN, 2, GDN_V_HEADS), jnp.float32,
                                    minval=np.log(1e-3), maxval=np.log(1e-1)))
    return {
        "x": nrm(ks[0], (BATCH, SEQ, d), 1.0),
        "c": nrm(ks[1], (BATCH, d), 1.0),
        "ctx": nrm(ks[2], (BATCH, CTX_LEN, d), 1.0),
        "c_ctx": nrm(ks[3], (d,), 1.0),
        "w_ada": nrm(ks[4], (DEPTH, d, 6 * d), ADA_SCALE * d ** -0.5),
        "b_ada": nrm(ks[5], (DEPTH, 6 * d), 0.02),
        "norm1_w": 1.0 + nrm(ks[6], (DEPTH, d), 0.02),
        "norm2_w": 1.0 + nrm(ks[7], (DEPTH, d), 0.02),
        "na_w_qkv": nrm(ks[8], (N_LAYERS_NA, d, 3 * d), d ** -0.5),
        "na_w_o": nrm(ks[9], (N_LAYERS_NA, d, d), d ** -0.5),
        "na_q_gain": 1.0 + nrm(ks[10], (N_LAYERS_NA, NA_HEAD_DIM), 0.02),
        "na_k_gain": 1.0 + nrm(ks[11], (N_LAYERS_NA, NA_HEAD_DIM), 0.02),
        "na_rpb": nrm(ks[12], (N_LAYERS_NA, NA_HEADS, 2 * NA_WIN_H - 1, 2 * NA_WIN_W - 1), 0.5),
        "gdn_w_in": nrm(ks[13], (N_LAYERS_GDN, d, GDN_CONV_DIM + GDN_VALUE_DIM), d ** -0.5),
        "gdn_conv_w": nrm(ks[14], (N_LAYERS_GDN, GDN_CONV, GDN_CONV_DIM), GDN_CONV ** -0.5),
        "gdn_w_ba": nrm(ks[15], (N_LAYERS_GDN, 2, d, 2 * GDN_V_HEADS), d ** -0.5),
        "gdn_a_log": jnp.log(jax.random.uniform(ks[16], (N_LAYERS_GDN, 2, GDN_V_HEADS), jnp.float32,
                                                minval=1.0, maxval=16.0)),
        "gdn_dt_bias": jnp.log(jnp.expm1(dt)),
        "gdn_norm_w": 1.0 + nrm(ks[18], (N_LAYERS_GDN, GDN_HEAD_V), 0.02),
        "gdn_w_o": nrm(ks[19], (N_LAYERS_GDN, GDN_VALUE_DIM, d), GDN_VALUE_DIM ** -0.5),
        "ffn_w1": nrm(ks[20], (DEPTH, d, D_FF), d ** -0.5),
        "ffn_w3": nrm(ks[21], (DEPTH, d, D_FF), d ** -0.5),
        "ffn_w2": nrm(ks[22], (DEPTH, D_FF, d), D_FF ** -0.5),
    }


def reference(x, c, ctx, c_ctx, w_ada, b_ada, norm1_w, norm2_w, na_w_qkv, na_w_o, na_q_gain,
              na_k_gain, na_rpb, gdn_w_in, gdn_conv_w, gdn_w_ba, gdn_a_log, gdn_dt_bias,
              gdn_norm_w, gdn_w_o, ffn_w1, ffn_w3, ffn_w2):
    xc = ctx
    silu_c = jax.nn.silu(c)
    silu_cc = jax.nn.silu(c_ctx)
    for i in range(DEPTH):
        update_ctx = i < DEPTH - 1
        mod = (silu_c @ w_ada[i] + b_ada[i])[:, None, :]
        mod_c = (silu_cc @ w_ada[i] + b_ada[i])[None, None, :]
        sh1, sc1, gt1, sh2, sc2, gt2 = jnp.split(mod, 6, axis=-1)
        csh1, csc1, cgt1, csh2, csc2, cgt2 = jnp.split(mod_c, 6, axis=-1)

        h = modulate(rms_norm(x, norm1_w[i]), sh1, sc1)
        hc = modulate(rms_norm(xc, norm1_w[i]), csh1, csc1)
        j = i // N_MIXERS
        if i % N_MIXERS == 0:
            y, yc = na_mixer(h, hc, na_w_qkv[j], na_w_o[j], na_q_gain[j], na_k_gain[j],
                             na_rpb[j], update_ctx)
        else:
            y, yc = gdn_mixer(h, hc, gdn_w_in[j], gdn_conv_w[j], gdn_w_ba[j], gdn_a_log[j],
                              gdn_dt_bias[j], gdn_norm_w[j], gdn_w_o[j], update_ctx)

        x = x + gt1 * y
        x = x + gt2 * swiglu(modulate(rms_norm(x, norm2_w[i]), sh2, sc2),
                             ffn_w1[i], ffn_w3[i], ffn_w2[i])
        if update_ctx:
            xc = xc + cgt1 * yc
            xc = xc + cgt2 * swiglu(modulate(rms_norm(xc, norm2_w[i]), csh2, csc2),
                                    ffn_w1[i], ffn_w3[i], ffn_w2[i])
    return x
```

```python
import functools

import jax
import jax.numpy as jnp
import numpy as np
from jax import lax
from jax.experimental import pallas as pl
from jax.experimental.pallas import tpu as pltpu

HEAD = 128
GRID_W = 64
WIN_H = 8
WIN_W = 16
ROW_GROUP = 4
KEY_ROWS = ROW_GROUP + WIN_H
CHUNK = 128
CONV_TAPS = 4
CONV_PAD = 8
EPS = 1e-6
NEG_INF = -1e30
NA_SCALE = HEAD ** -0.5
GDN_SCALE = HEAD ** -0.5
TOKEN_TILE = 512
VMEM_LIMIT = 48 * 1024 * 1024

F32 = jnp.float32
BF16 = jnp.bfloat16


def _params(*sem):
    return pltpu.CompilerParams(dimension_semantics=sem, vmem_limit_bytes=VMEM_LIMIT)


def _pick(n, cands):
    for c in cands:
        if n % c == 0:
            return c
    raise ValueError(f"no tile in {cands} divides {n}")


def _silu(x):
    return x / (1.0 + jnp.exp(-x))


def _dot(a, b):
    return jnp.dot(a, b, preferred_element_type=F32)


def _dot_nt(a, b):
    return lax.dot_general(a, b, (((1,), (1,)), ((), ())), preferred_element_type=F32)


def _dot_tn(a, b):
    return lax.dot_general(a, b, (((0,), (0,)), ((), ())), preferred_element_type=F32)


def _dot_hi(a, b):
    return jnp.dot(a, b, preferred_element_type=F32, precision=lax.Precision.HIGHEST)


def _ada_kernel(c_ref, w_ref, b_ref, o_ref):
    s = _silu(c_ref[...]).astype(BF16)
    o_ref[...] = _dot(s, w_ref[...].astype(BF16)) + b_ref[...]


def _ada(cond, w_ada, b_ada):
    depth, d, n = w_ada.shape
    tn = _pick(n, (1024, 512, 256, 128))
    return pl.pallas_call(
        _ada_kernel,
        out_shape=jax.ShapeDtypeStruct((depth, 8, n), F32),
        grid=(depth, n // tn),
        in_specs=[
            pl.BlockSpec((8, d), lambda l, j: (0, 0)),
            pl.BlockSpec((None, d, tn), lambda l, j: (l, 0, j)),
            pl.BlockSpec((None, 1, tn), lambda l, j: (l, 0, j)),
        ],
        out_specs=pl.BlockSpec((None, 8, tn), lambda l, j: (l, 0, j)),
        compiler_params=_params("parallel", "parallel"),
        name="ada_mod",
    )(cond, w_ada, b_ada.reshape(depth, 1, n))


class _Tokens:
    def __init__(self, batch, seq, ctx):
        self.batch, self.seq, self.ctx = batch, seq, ctx
        self.n_lat = batch * seq
        self.n_all = self.n_lat + batch * ctx
        self.tm = TOKEN_TILE
        assert seq % self.tm == 0 and (batch * ctx) % self.tm == 0
        self.lat_tiles = self.n_lat // self.tm
        self.all_tiles = self.n_all // self.tm
        self.tiles_per_batch = seq // self.tm

    def mod_row(self, t):
        return jnp.where(t < self.lat_tiles, t // self.tiles_per_batch, self.batch)


def _mod_spec(tok, d, chunk):
    return pl.BlockSpec((None, 1, d), lambda m, j: (tok.mod_row(m), 0, chunk))


def _norm_mod(x, nw, sh, sc):
    ms = jnp.mean(x * x, axis=-1, keepdims=True)
    return (x * lax.rsqrt(ms + EPS) * nw) * (1.0 + sc) + sh


def _nmm_kernel(x_ref, nw_ref, sh_ref, sc_ref, w_ref, o_ref, h_scr):
    @pl.when(pl.program_id(1) == 0)
    def _():
        h_scr[...] = _norm_mod(x_ref[...], nw_ref[...], sh_ref[...], sc_ref[...]).astype(BF16)

    o_ref[...] = _dot(h_scr[...], w_ref[...]).astype(o_ref.dtype)


def _nmm_na_kernel(x_ref, nw_ref, sh_ref, sc_ref, w_ref, qg_ref, kg_ref, o_ref, h_scr, *, tiles_per_kind):
    n = pl.program_id(1)

    @pl.when(n == 0)
    def _():
        h_scr[...] = _norm_mod(x_ref[...], nw_ref[...], sh_ref[...], sc_ref[...]).astype(BF16)

    acc = _dot(h_scr[...], w_ref[...])
    kind = n // tiles_per_kind

    @pl.when(kind == 2)
    def _():
        o_ref[...] = acc.astype(o_ref.dtype)

    @pl.when(kind < 2)
    def _():
        gain = jnp.where(kind == 0, qg_ref[...], kg_ref[...])
        for j in range(acc.shape[1] // HEAD):
            a = acc[:, j * HEAD:(j + 1) * HEAD]
            ms = jnp.mean(a * a, axis=-1, keepdims=True)
            o_ref[:, j * HEAD:(j + 1) * HEAD] = (a * lax.rsqrt(ms + EPS) * gain).astype(o_ref.dtype)


def _norm_mod_matmul(tok, x_all, norm_w, mod, w, out_dtype, name, na_gains=None):
    d, n = w.shape
    tm = tok.tm
    if na_gains is None:
        tn = _pick(n, (1024, 512, 256, 128))
        kern = _nmm_kernel
        extra_specs, extra_args = [], []
    else:
        tn = _pick(d, (1024, 512, 256, 128))
        kern = functools.partial(_nmm_na_kernel, tiles_per_kind=d // tn)
        extra_specs = [pl.BlockSpec((1, HEAD), lambda m, j: (0, 0))] * 2
        extra_args = [g.reshape(1, HEAD) for g in na_gains]
    return pl.pallas_call(
        kern,
        out_shape=jax.ShapeDtypeStruct((tok.n_all, n), out_dtype),
        grid=(tok.all_tiles, n // tn),
        in_specs=[
            pl.BlockSpec((tm, d), lambda m, j: (m, 0)),
            pl.BlockSpec((1, d), lambda m, j: (0, 0)),
            _mod_spec(tok, d, 0),
            _mod_spec(tok, d, 1),
            pl.BlockSpec((d, tn), lambda m, j: (0, j)),
        ] + extra_specs,
        out_specs=pl.BlockSpec((tm, tn), lambda m, j: (m, j)),
        scratch_shapes=[pltpu.VMEM((tm, d), BF16)],
        compiler_params=_params("parallel", "arbitrary"),
        name=name,
    )(x_all, norm_w.reshape(1, d), mod, mod, w, *extra_args)


def _mm_res_kernel(a_ref, w_ref, x_ref, g_ref, o_ref):
    o_ref[...] = x_ref[...] + g_ref[...] * _dot(a_ref[...], w_ref[...])


def _matmul_residual(tok, n_tiles, a, w, x_all, mod, gate_chunk, name):
    k, d = w.shape
    tm = tok.tm
    tn = _pick(d, (1024, 512, 256, 128))
    return pl.pallas_call(
        _mm_res_kernel,
        out_shape=jax.ShapeDtypeStruct(x_all.shape, F32),
        grid=(n_tiles, d // tn),
        in_specs=[
            pl.BlockSpec((tm, k), lambda m, j: (m, 0)),
            pl.BlockSpec((k, tn), lambda m, j: (0, j)),
            pl.BlockSpec((tm, tn), lambda m, j: (m, j)),
            pl.BlockSpec((None, 1, tn), lambda m, j: (tok.mod_row(m), 0, gate_chunk * (d // tn) + j)),
        ],
        out_specs=pl.BlockSpec((tm, tn), lambda m, j: (m, j)),
        input_output_aliases={2: 0},
        compiler_params=_params("parallel", "parallel"),
        name=name,
    )(a, w, x_all, mod)


def _ffn_kernel(x_ref, nw_ref, sh_ref, sc_ref, g_ref, w1_ref, w3_ref, w2_ref, o_ref, h_scr, acc_scr):
    f = pl.program_id(1)

    @pl.when(f == 0)
    def _():
        h_scr[...] = _norm_mod(x_ref[...], nw_ref[...], sh_ref[...], sc_ref[...]).astype(BF16)
        acc_scr[...] = jnp.zeros_like(acc_scr)

    h = h_scr[...]
    u = _silu(_dot(h, w1_ref[...])) * _dot(h, w3_ref[...])
    acc_scr[...] += _dot(u.astype(BF16), w2_ref[...])

    @pl.when(f == pl.num_programs(1) - 1)
    def _():
        o_ref[...] = x_ref[...] + g_ref[...] * acc_scr[...]


def _ffn(tok, n_tiles, x_all, norm_w, mod, w1, w3, w2):
    d, dff = w1.shape
    tm = tok.tm
    tf = _pick(dff, (512, 256, 128))
    return pl.pallas_call(
        _ffn_kernel,
        out_shape=jax.ShapeDtypeStruct(x_all.shape, F32),
        grid=(n_tiles, dff // tf),
        in_specs=[
            pl.BlockSpec((tm, d), lambda m, f: (m, 0)),
            pl.BlockSpec((1, d), lambda m, f: (0, 0)),
            _mod_spec(tok, d, 3),
            _mod_spec(tok, d, 4),
            _mod_spec(tok, d, 5),
            pl.BlockSpec((d, tf), lambda m, f: (0, f)),
            pl.BlockSpec((d, tf), lambda m, f: (0, f)),
            pl.BlockSpec((tf, d), lambda m, f: (f, 0)),
        ],
        out_specs=pl.BlockSpec((tm, d), lambda m, f: (m, 0)),
        scratch_shapes=[pltpu.VMEM((tm, d), BF16), pltpu.VMEM((tm, d), F32)],
        input_output_aliases={0: 0},
        compiler_params=_params("parallel", "arbitrary"),
        name="ffn",
    )(x_all, norm_w.reshape(1, d), mod, mod, mod, w1, w3, w2)


def _na_bias_tables(rpb, rows):
    kh = min(WIN_H, rows)
    cols = np.arange(GRID_W)
    col_start = np.clip(cols - WIN_W // 2, 0, GRID_W - WIN_W)
    in_win = (cols[None, :] >= col_start[:, None]) & (cols[None, :] < col_start[:, None] + WIN_W)
    dc = np.clip(cols[None, :] - cols[:, None], 1 - WIN_W, WIN_W - 1) + WIN_W - 1
    off = np.zeros((3, ROW_GROUP, KEY_ROWS), np.int32)
    ok = np.zeros((3, ROW_GROUP, KEY_ROWS), bool)
    for t, (r0, start) in enumerate(((0, 0), (ROW_GROUP, 0), (rows - ROW_GROUP, rows - KEY_ROWS))):
        for i in range(ROW_GROUP):
            r = r0 + i
            rs = min(max(r - kh // 2, 0), rows - kh)
            for m in range(KEY_ROWS):
                j = start + m - rs
                ok[t, i, m] = 0 <= j < kh
                off[t, i, m] = min(max(rs - r + j + WIN_H - 1, 0), 2 * WIN_H - 2)
    vals = rpb[:, off[:, :, None, :, None], dc[None, None, :, None, :]]
    mask = ok[:, :, None, :, None] & in_win[None, None, :, None, :]
    vals = jnp.where(mask[None], vals.astype(F32), NEG_INF)
    heads = rpb.shape[0]
    return vals.reshape(heads, 3, ROW_GROUP * GRID_W, KEY_ROWS * GRID_W)


def _softmax_pv(parts):
    m = functools.reduce(jnp.maximum, [jnp.max(s, axis=-1, keepdims=True) for s, _ in parts])
    ps = [jnp.exp(s - m) for s, _ in parts]
    den = functools.reduce(jnp.add, [jnp.sum(p, axis=-1, keepdims=True) for p in ps])
    num = functools.reduce(jnp.add, [_dot(p.astype(BF16), v) for p, (_, v) in zip(ps, parts)])
    return num / den


def _na_kernel(q_ref, k_ref, v_ref, qc_ref, kc_ref, vc_ref, bias_ref, o_ref, oc_ref, *, rows):
    kc = kc_ref[...]
    vc = vc_ref[...]
    n_groups = rows // ROW_GROUP
    gq = ROW_GROUP * GRID_W
    gk = KEY_ROWS * GRID_W

    def group(gi, carry):
        r0 = gi * ROW_GROUP
        start = jnp.clip(r0 - WIN_H // 2, 0, rows - KEY_ROWS)
        table = jnp.where(gi == 0, 0, jnp.where(gi == n_groups - 1, 2, 1))
        q = q_ref[pl.ds(pl.multiple_of(r0 * GRID_W, gq), gq), :]
        koff = pl.multiple_of(start * GRID_W, GRID_W)
        ks = k_ref[pl.ds(koff, gk), :]
        vs = v_ref[pl.ds(koff, gk), :]
        s_loc = _dot_nt(q, ks) * NA_SCALE + bias_ref[table]
        s_ctx = _dot_nt(q, kc) * NA_SCALE
        o = _softmax_pv([(s_loc, vs), (s_ctx, vc)])
        o_ref[pl.ds(pl.multiple_of(r0 * GRID_W, gq), gq), :] = o.astype(o_ref.dtype)
        return carry

    lax.fori_loop(0, n_groups, group, 0)
    s_cc = _dot_nt(qc_ref[...], kc) * NA_SCALE
    oc_ref[...] = _softmax_pv([(s_cc, vc)]).astype(oc_ref.dtype)


def _na_attention(tok, qkv, bias_tables, d):
    heads = d // HEAD
    seq, ctx, batch = tok.seq, tok.ctx, tok.batch
    rows = seq // GRID_W
    assert rows % ROW_GROUP == 0 and rows >= KEY_ROWS
    ctx_blk0 = tok.n_lat // ctx
    gq, gk = ROW_GROUP * GRID_W, KEY_ROWS * GRID_W

    def lat(kind):
        return pl.BlockSpec((seq, HEAD), lambda h, b: (b, kind * heads + h))

    def cx(kind):
        return pl.BlockSpec((ctx, HEAD), lambda h, b: (ctx_blk0 + b, kind * heads + h))

    o_lat, o_ctx = pl.pallas_call(
        functools.partial(_na_kernel, rows=rows),
        out_shape=(jax.ShapeDtypeStruct((tok.n_lat, d), BF16),
                   jax.ShapeDtypeStruct((batch * ctx, d), BF16)),
        grid=(heads, batch),
        in_specs=[lat(0), lat(1), lat(2), cx(0), cx(1), cx(2),
                  pl.BlockSpec((None, 3, gq, gk), lambda h, b: (h, 0, 0, 0))],
        out_specs=(pl.BlockSpec((seq, HEAD), lambda h, b: (b, h)),
                   pl.BlockSpec((ctx, HEAD), lambda h, b: (b, h))),
        compiler_params=_params("parallel", "parallel"),
        name="na_attention",
    )(qkv, qkv, qkv, qkv, qkv, qkv, bias_tables)
    return jnp.concatenate([o_lat, o_ctx], axis=0)


def _conv_kernel(x_ref, w_ref, o_ref, pad_scr, *, n_key_blocks, seq_len, row_chunk):
    c = pl.program_id(1)
    zeros = jnp.zeros((CONV_PAD, HEAD), F32)
    pad_scr[pl.ds(0, CONV_PAD), :] = zeros
    pad_scr[pl.ds(CONV_PAD + seq_len, CONV_PAD), :] = zeros
    pad_scr[pl.ds(CONV_PAD, seq_len), :] = x_ref[...].astype(F32)
    w = w_ref[...]
    is_qk = c < 2 * n_key_blocks
    scale = jnp.where(c < n_key_blocks, GDN_SCALE, 1.0)
    for r in range(seq_len // row_chunk):
        base = CONV_PAD + r * row_chunk - (CONV_TAPS - 1) // 2
        y = sum(pad_scr[pl.ds(base + j, row_chunk), :] * w[j:j + 1, :] for j in range(CONV_TAPS))
        y = _silu(y)
        nrm = lax.rsqrt(jnp.sum(y * y, axis=-1, keepdims=True) + EPS) * scale
        o_ref[pl.ds(r * row_chunk, row_chunk), :] = jnp.where(is_qk, y * nrm, y)


def _gdn_conv(qkvz, conv_w, n_seq, seq_len, blk0, key_dim):
    conv_dim = conv_w.shape[1]
    row_chunk = min(seq_len, 512)
    return pl.pallas_call(
        functools.partial(_conv_kernel, n_key_blocks=key_dim // HEAD, seq_len=seq_len, row_chunk=row_chunk),
        out_shape=jax.ShapeDtypeStruct((n_seq * seq_len, conv_dim), F32),
        grid=(n_seq, conv_dim // HEAD),
        in_specs=[pl.BlockSpec((seq_len, HEAD), lambda s, c: (blk0 + s, c)),
                  pl.BlockSpec((CONV_TAPS, HEAD), lambda s, c: (0, c))],
        out_specs=pl.BlockSpec((seq_len, HEAD), lambda s, c: (s, c)),
        scratch_shapes=[pltpu.VMEM((seq_len + 2 * CONV_PAD, HEAD), F32)],
        compiler_params=_params("parallel", "parallel"),
        name=f"gdn_conv_{seq_len}",
    )(qkvz, conv_w)


def _gates_kernel(ba_ref, alog_ref, dt_ref, o_ref, *, n_chunks, n_vheads):
    row = lax.broadcasted_iota(jnp.int32, (CHUNK, CHUNK), 0)
    col = lax.broadcasted_iota(jnp.int32, (CHUNK, CHUNK), 1)
    prefix = (row <= col).astype(F32)
    suffix = (row >= col).astype(F32)
    is_decay = (row % (2 * n_vheads)) >= n_vheads
    neg_a = -jnp.exp(alog_ref[...])
    for c in range(n_chunks):
        bt = ba_ref[pl.ds(c * CHUNK, CHUNK), :].T
        beta = 1.0 / (1.0 + jnp.exp(-bt))
        x = bt + dt_ref[...]
        g = neg_a * (jnp.maximum(x, 0.0) + jnp.log1p(jnp.exp(-jnp.abs(x))))
        cum = jnp.where(row < 2 * n_vheads, _dot_hi(g, prefix), _dot_hi(g, suffix))
        o_ref[c] = jnp.where(is_decay, cum, beta)


def _gdn_gates(ba, a_log, dt_bias):
    n_tok, n_col = ba.shape
    n_vheads = a_log.shape[1]
    assert n_col == CHUNK and 4 * n_vheads <= n_col
    zeros = jnp.zeros_like(a_log)
    pad = ((0, n_col - 4 * n_vheads), (0, 0))
    alog_col = jnp.pad(jnp.concatenate([zeros, a_log], axis=1).reshape(4 * n_vheads, 1), pad)
    dt_col = jnp.pad(jnp.concatenate([zeros, dt_bias], axis=1).reshape(4 * n_vheads, 1), pad)
    per_step = 4
    assert (n_tok // CHUNK) % per_step == 0
    return pl.pallas_call(
        functools.partial(_gates_kernel, n_chunks=per_step, n_vheads=n_vheads),
        out_shape=jax.ShapeDtypeStruct((n_tok // CHUNK, n_col, CHUNK), F32),
        grid=(n_tok // (CHUNK * per_step),),
        in_specs=[pl.BlockSpec((CHUNK * per_step, n_col), lambda i: (i, 0)),
                  pl.BlockSpec((n_col, 1), lambda i: (0, 0)),
                  pl.BlockSpec((n_col, 1), lambda i: (0, 0))],
        out_specs=pl.BlockSpec((per_step, n_col, CHUNK), lambda i: (i, 0, 0)),
        compiler_params=_params("parallel"),
        name="gdn_gates",
    )(ba, alog_col, dt_col)


def _unit_tri_inverse(l):
    n = l.shape[0]
    eye = (lax.broadcasted_iota(jnp.int32, (n, n), 0) == lax.broadcasted_iota(jnp.int32, (n, n), 1)).astype(F32)
    p = eye - l
    q = _dot_hi(l, l)
    levels = int(np.log2(n)) - 1
    for _ in range(levels - 1):
        pq = _dot_hi(jnp.concatenate([p, q], axis=0), q)
        p = p + pq[:n]
        q = pq[n:]
    return p + _dot_hi(p, q)


def _prep_kernel(q_ref, k_ref, v_ref, gates_ref, u_ref, w_ref, qd_ref, kd_ref, a_ref, egl_ref, *, n_vheads):
    hv = pl.program_id(1)
    n_chunks = q_ref.shape[0] // CHUNK
    row = lax.broadcasted_iota(jnp.int32, (CHUNK, CHUNK), 0)
    col = lax.broadcasted_iota(jnp.int32, (CHUNK, CHUNK), 1)

    def chunk(c, carry):
        sl = pl.ds(pl.multiple_of(c * CHUNK, CHUNK), CHUNK)
        q = q_ref[sl, :]
        k = k_ref[sl, :]
        v = v_ref[sl, :]
        kb16 = k.astype(BF16)
        qk = _dot_nt(q.astype(BF16), kb16)
        for z in range(2):
            incl = (row >= col) if z == 0 else (row <= col)
            strict = (row > col) if z == 0 else (row < col)
            beta_r = jnp.broadcast_to(gates_ref[c, pl.ds(z * 2 * n_vheads + hv, 1), :], (CHUNK, CHUNK))
            g_r = jnp.broadcast_to(gates_ref[c, pl.ds(z * 2 * n_vheads + n_vheads + hv, 1), :], (CHUNK, CHUNK))
            beta_c = beta_r.T
            g_c = g_r.T
            decay = jnp.where(incl, jnp.exp(jnp.where(incl, g_c - g_r, 0.0)), 0.0)
            k_beta = k * beta_c
            v_beta = v * beta_c
            l_mat = jnp.where(strict, _dot_nt(k_beta.astype(BF16), kb16) * decay, 0.0)
            t_inv = _unit_tri_inverse(l_mat)
            exp_g = jnp.exp(g_c)
            rhs = jnp.concatenate([v_beta, k_beta * exp_g], axis=1).astype(BF16)
            uw = _dot(t_inv.astype(BF16), rhs)
            last = CHUNK - 1 if z == 0 else 0
            g_last = jnp.broadcast_to(g_c[last:last + 1, :], (CHUNK, CHUNK))
            u_ref[z, sl, :] = uw[:, :HEAD]
            w_ref[z, sl, :] = uw[:, HEAD:].astype(BF16)
            qd_ref[z, sl, :] = (q * exp_g).astype(BF16)
            kd_ref[z, sl, :] = (k * jnp.exp(g_last - g_c)).astype(BF16)
            a_ref[z, sl, :] = jnp.where(incl, qk * decay, 0.0).astype(BF16)
            egl_ref[z, pl.ds(c, 1), :] = jnp.exp(g_last[0:1, :])
        return carry

    lax.fori_loop(0, n_chunks, chunk, 0)


def _gdn_prep(qkv_conv, gates, n_seq, seq_len, chunk0, key_dim, n_vheads):
    tb = min(seq_len, 1024)
    n_tb = seq_len // tb
    cb = tb // CHUNK
    nk = key_dim // HEAD
    rep = n_vheads // nk

    def rows(col_of):
        return pl.BlockSpec((tb, HEAD), lambda s, h, t: (s * n_tb + t, col_of(h)))

    def out(dtype):
        return jax.ShapeDtypeStruct((n_seq, n_vheads, 2, seq_len, HEAD), dtype)

    out_spec = pl.BlockSpec((None, None, 2, tb, HEAD), lambda s, h, t: (s, h, 0, t, 0))
    return pl.pallas_call(
        functools.partial(_prep_kernel, n_vheads=n_vheads),
        out_shape=(out(F32), out(BF16), out(BF16), out(BF16), out(BF16),
                   jax.ShapeDtypeStruct((n_seq, n_vheads, 2, seq_len // CHUNK, HEAD), F32)),
        grid=(n_seq, n_vheads, n_tb),
        in_specs=[rows(lambda h: h // rep), rows(lambda h: nk + h // rep), rows(lambda h: 2 * nk + h),
                  pl.BlockSpec((cb, CHUNK, CHUNK), lambda s, h, t: (chunk0 // cb + s * n_tb + t, 0, 0))],
        out_specs=(out_spec,) * 5 + (pl.BlockSpec((None, None, 2, cb, HEAD), lambda s, h, t: (s, h, 0, t, 0)),),
        compiler_params=_params("parallel", "parallel", "parallel"),
        name=f"gdn_prep_{seq_len}",
    )(qkv_conv, qkv_conv, qkv_conv, gates)


def _scan_kernel(u_ref, w_ref, qd_ref, kd_ref, a_ref, egl_ref, s0_ref, z_ref, nw_ref,
                 y_ref, sf_ref, s_scr, o_scr):
    n_chunks = u_ref.shape[1] // CHUNK
    s_scr[...] = s0_ref[...]

    def step(c, carry):
        for z in range(2):
            cc = c if z == 0 else n_chunks - 1 - c
            sl = pl.ds(pl.multiple_of(cc * CHUNK, CHUNK), CHUNK)
            s = s_scr[z]
            s16 = s.astype(BF16)
            r = _dot(jnp.concatenate([w_ref[z, sl, :], qd_ref[z, sl, :]], axis=0), s16)
            v_new = (u_ref[z, sl, :] - r[:CHUNK]).astype(BF16)
            o_scr[z, sl, :] = r[CHUNK:] + _dot(a_ref[z, sl, :], v_new)
            s_scr[z] = s * egl_ref[z, pl.ds(cc, 1), :] + _dot_tn(kd_ref[z, sl, :], v_new)
        return carry

    lax.fori_loop(0, n_chunks, step, 0)
    sf_ref[...] = s_scr[...]
    o = o_scr[0] + o_scr[1]
    ms = jnp.mean(o * o, axis=-1, keepdims=True)
    y = (o * lax.rsqrt(ms + EPS) * nw_ref[...]) * _silu(z_ref[...].astype(F32))
    y_ref[...] = y.astype(y_ref.dtype)


def _gdn_scan(factors, s0, qkvz, norm_w, n_seq, seq_len, blk0, z_col0, n_vheads):
    u, w, qd, kd, a, egl = factors

    def fac(arr):
        return pl.BlockSpec((None, None) + arr.shape[2:], lambda s, h: (s, h, 0, 0, 0))

    state_spec = pl.BlockSpec((None, None, 2, HEAD, HEAD), lambda s, h: (s, h, 0, 0, 0))
    return pl.pallas_call(
        _scan_kernel,
        out_shape=(jax.ShapeDtypeStruct((n_seq * seq_len, n_vheads * HEAD), BF16),
                   jax.ShapeDtypeStruct((n_seq, n_vheads, 2, HEAD, HEAD), F32)),
        grid=(n_seq, n_vheads),
        in_specs=[fac(u), fac(w), fac(qd), fac(kd), fac(a), fac(egl), state_spec,
                  pl.BlockSpec((seq_len, HEAD), lambda s, h: (blk0 + s, z_col0 + h)),
                  pl.BlockSpec((1, HEAD), lambda s, h: (0, 0))],
        out_specs=(pl.BlockSpec((seq_len, HEAD), lambda s, h: (s, h)), state_spec),
        scratch_shapes=[pltpu.VMEM((2, HEAD, HEAD), F32), pltpu.VMEM((2, seq_len, HEAD), F32)],
        compiler_params=_params("parallel", "parallel"),
        name=f"gdn_scan_{seq_len}",
    )(u, w, qd, kd, a, egl, s0, qkvz, norm_w.reshape(1, HEAD))


def _gdn_mixer(tok, qkvz, ba, conv_w, a_log, dt_bias, norm_w):
    conv_dim = conv_w.shape[1]
    value_dim = qkvz.shape[1] - conv_dim
    key_dim = (conv_dim - value_dim) // 2
    n_vheads = value_dim // HEAD
    batch, seq, ctx = tok.batch, tok.seq, tok.ctx
    gates = _gdn_gates(ba, a_log, dt_bias)
    z_col0 = conv_dim // HEAD

    def run(n_seq, seq_len, blk0, chunk0, s0):
        conv = _gdn_conv(qkvz, conv_w, n_seq, seq_len, blk0, key_dim)
        factors = _gdn_prep(conv, gates, n_seq, seq_len, chunk0, key_dim, n_vheads)
        return _gdn_scan(factors, s0, qkvz, norm_w, n_seq, seq_len, blk0, z_col0, n_vheads)

    zero_state = jnp.zeros((batch, n_vheads, 2, HEAD, HEAD), F32)
    y_ctx, s_ctx = run(batch, ctx, tok.n_lat // ctx, tok.n_lat // CHUNK, zero_state)
    y_lat, _ = run(batch, seq, 0, 0, s_ctx)
    return jnp.concatenate([y_lat, y_ctx], axis=0)


def kernel(x, c, ctx, c_ctx, w_ada, b_ada, norm1_w, norm2_w, na_w_qkv, na_w_o, na_q_gain, na_k_gain, na_rpb, gdn_w_in, gdn_conv_w, gdn_w_ba, gdn_a_log, gdn_dt_bias, gdn_norm_w, gdn_w_o, ffn_w1, ffn_w3, ffn_w2):
    batch, seq, d = x.shape
    n_ctx = ctx.shape[1]
    depth = w_ada.shape[0]
    tok = _Tokens(batch, seq, n_ctx)
    assert batch + 1 <= 8

    x_all = jnp.concatenate([x.reshape(batch * seq, d), ctx.reshape(batch * n_ctx, d)], axis=0)
    cond = jnp.concatenate([c, c_ctx[None, :], jnp.zeros((8 - batch - 1, d), F32)], axis=0)
    mods = _ada(cond, w_ada, b_ada)

    for i in range(depth):
        update_ctx = i < depth - 1
        n_tiles = tok.all_tiles if update_ctx else tok.lat_tiles
        mod = mods[i].reshape(8, 1, 6 * d)
        j = i // 2
        if i % 2 == 0:
            qkv = _norm_mod_matmul(tok, x_all, norm1_w[i], mod, na_w_qkv[j].astype(BF16), BF16, "na_qkv",
                                   na_gains=(na_q_gain[j], na_k_gain[j]))
            tables = _na_bias_tables(na_rpb[j], seq // GRID_W)
            mixed = _na_attention(tok, qkv, tables, d)
            w_o = na_w_o[j]
        else:
            qkvz = _norm_mod_matmul(tok, x_all, norm1_w[i], mod, gdn_w_in[j].astype(BF16), BF16, "gdn_in")
            w_ba = jnp.concatenate([gdn_w_ba[j, 0], gdn_w_ba[j, 1]], axis=1).astype(BF16)
            w_ba = jnp.pad(w_ba, ((0, 0), (0, CHUNK - w_ba.shape[1])))
            ba = _norm_mod_matmul(tok, x_all, norm1_w[i], mod, w_ba, F32, "gdn_ba")
            mixed = _gdn_mixer(tok, qkvz, ba, gdn_conv_w[j], gdn_a_log[j], gdn_dt_bias[j], gdn_norm_w[j])
            w_o = gdn_w_o[j]
        x_all = _matmul_residual(tok, n_tiles, mixed, w_o.astype(BF16), x_all, mod, 2, "mixer_out")
        x_all = _ffn(tok, n_tiles, x_all, norm2_w[i], mod, ffn_w1[i].astype(BF16), ffn_w3[i].astype(BF16),
                     ffn_w2[i].astype(BF16))
    return x_all[:batch * seq].reshape(batch, seq, d)
```

```python
import functools

import jax
import jax.numpy as jnp
import numpy as np
from jax import lax
from jax.experimental import pallas as pl
from jax.experimental.pallas import tpu as pltpu

HEAD = 128
GRID_W = 64
WIN_H = 8
WIN_W = 16
ROW_GROUP = 4
KEY_ROWS = ROW_GROUP + WIN_H
CHUNK = 128
CONV_TAPS = 4
CONV_PAD = 8
EPS = 1e-6
NEG_INF = -1e30
NA_SCALE = HEAD ** -0.5
GDN_SCALE = HEAD ** -0.5
TOKEN_TILE = 512
TRI_BASE = 8
PREP_ROWS = 1024
VMEM_LIMIT = 48 * 1024 * 1024

F32 = jnp.float32
BF16 = jnp.bfloat16


def _params(*sem):
    return pltpu.CompilerParams(dimension_semantics=sem, vmem_limit_bytes=VMEM_LIMIT)


def _pick(n, cands):
    for c in cands:
        if n % c == 0:
            return c
    raise ValueError(f"no tile in {cands} divides {n}")


def _silu(x):
    return x / (1.0 + jnp.exp(-x))


def _dot(a, b):
    return jnp.dot(a, b, preferred_element_type=F32)


def _dot_nt(a, b):
    return lax.dot_general(a, b, (((1,), (1,)), ((), ())), preferred_element_type=F32)


def _dot_tn(a, b):
    return lax.dot_general(a, b, (((0,), (0,)), ((), ())), preferred_element_type=F32)


def _dot_hi(a, b):
    return jnp.dot(a, b, preferred_element_type=F32, precision=lax.Precision.HIGHEST)


def _ada_kernel(c_ref, w_ref, b_ref, o_ref):
    s = _silu(c_ref[...]).astype(BF16)
    o_ref[...] = _dot(s, w_ref[...].astype(BF16)) + b_ref[...]


def _ada(cond, w_ada, b_ada):
    depth, d, n = w_ada.shape
    tn = _pick(n, (1024, 512, 256, 128))
    return pl.pallas_call(
        _ada_kernel,
        out_shape=jax.ShapeDtypeStruct((depth, 8, n), F32),
        grid=(depth, n // tn),
        in_specs=[
            pl.BlockSpec((8, d), lambda l, j: (0, 0)),
            pl.BlockSpec((None, d, tn), lambda l, j: (l, 0, j)),
            pl.BlockSpec((None, 1, tn), lambda l, j: (l, 0, j)),
        ],
        out_specs=pl.BlockSpec((None, 8, tn), lambda l, j: (l, 0, j)),
        compiler_params=_params("parallel", "parallel"),
        name="ada_mod",
    )(cond, w_ada, b_ada.reshape(depth, 1, n))


class _Tokens:
    def __init__(self, batch, seq, ctx):
        self.batch, self.seq, self.ctx = batch, seq, ctx
        self.n_lat = batch * seq
        self.n_all = self.n_lat + batch * ctx
        self.tm = TOKEN_TILE
        assert seq % self.tm == 0 and (batch * ctx) % self.tm == 0
        self.lat_tiles = self.n_lat // self.tm
        self.all_tiles = self.n_all // self.tm
        self.tiles_per_batch = seq // self.tm

    def mod_row(self, t):
        return jnp.where(t < self.lat_tiles, t // self.tiles_per_batch, self.batch)


def _mod_spec(tok, d, chunk):
    return pl.BlockSpec((None, 1, d), lambda m, j: (tok.mod_row(m), 0, chunk))


def _norm_mod(x, nw, sh, sc):
    ms = jnp.mean(x * x, axis=-1, keepdims=True)
    return (x * lax.rsqrt(ms + EPS) * nw) * (1.0 + sc) + sh


def _nmm_kernel(x_ref, nw_ref, sh_ref, sc_ref, w_ref, o_ref, h_scr):
    @pl.when(pl.program_id(1) == 0)
    def _():
        h_scr[...] = _norm_mod(x_ref[...], nw_ref[...], sh_ref[...], sc_ref[...]).astype(BF16)

    o_ref[...] = _dot(h_scr[...], w_ref[...]).astype(o_ref.dtype)


def _nmm_na_kernel(x_ref, nw_ref, sh_ref, sc_ref, w_ref, qg_ref, kg_ref, o_ref, h_scr, *, tiles_per_kind):
    n = pl.program_id(1)

    @pl.when(n == 0)
    def _():
        h_scr[...] = _norm_mod(x_ref[...], nw_ref[...], sh_ref[...], sc_ref[...]).astype(BF16)

    acc = _dot(h_scr[...], w_ref[...])
    kind = n // tiles_per_kind

    @pl.when(kind == 2)
    def _():
        o_ref[...] = acc.astype(o_ref.dtype)

    @pl.when(kind < 2)
    def _():
        gain = jnp.where(kind == 0, qg_ref[...], kg_ref[...])
        for j in range(acc.shape[1] // HEAD):
            a = acc[:, j * HEAD:(j + 1) * HEAD]
            ms = jnp.mean(a * a, axis=-1, keepdims=True)
            o_ref[:, j * HEAD:(j + 1) * HEAD] = (a * lax.rsqrt(ms + EPS) * gain).astype(o_ref.dtype)


def _norm_mod_matmul(tok, x_all, norm_w, mod, w, out_dtype, name, na_gains=None):
    d, n = w.shape
    tm = tok.tm
    if na_gains is None:
        tn = _pick(n, (1024, 512, 256, 128))
        kern = _nmm_kernel
        extra_specs, extra_args = [], []
    else:
        tn = _pick(d, (1024, 512, 256, 128))
        kern = functools.partial(_nmm_na_kernel, tiles_per_kind=d // tn)
        extra_specs = [pl.BlockSpec((1, HEAD), lambda m, j: (0, 0))] * 2
        extra_args = [g.reshape(1, HEAD) for g in na_gains]
    return pl.pallas_call(
        kern,
        out_shape=jax.ShapeDtypeStruct((tok.n_all, n), out_dtype),
        grid=(tok.all_tiles, n // tn),
        in_specs=[
            pl.BlockSpec((tm, d), lambda m, j: (m, 0)),
            pl.BlockSpec((1, d), lambda m, j: (0, 0)),
            _mod_spec(tok, d, 0),
            _mod_spec(tok, d, 1),
            pl.BlockSpec((d, tn), lambda m, j: (0, j)),
        ] + extra_specs,
        out_specs=pl.BlockSpec((tm, tn), lambda m, j: (m, j)),
        scratch_shapes=[pltpu.VMEM((tm, d), BF16)],
        compiler_params=_params("parallel", "arbitrary"),
        name=name,
    )(x_all, norm_w.reshape(1, d), mod, mod, w, *extra_args)


def _mm_res_kernel(a_ref, w_ref, x_ref, g_ref, o_ref):
    o_ref[...] = x_ref[...] + g_ref[...] * _dot(a_ref[...], w_ref[...])


def _matmul_residual(tok, n_tiles, a, w, x_all, mod, gate_chunk, name):
    k, d = w.shape
    tm = tok.tm
    tn = _pick(d, (1024, 512, 256, 128))
    return pl.pallas_call(
        _mm_res_kernel,
        out_shape=jax.ShapeDtypeStruct(x_all.shape, F32),
        grid=(n_tiles, d // tn),
        in_specs=[
            pl.BlockSpec((tm, k), lambda m, j: (m, 0)),
            pl.BlockSpec((k, tn), lambda m, j: (0, j)),
            pl.BlockSpec((tm, tn), lambda m, j: (m, j)),
            pl.BlockSpec((None, 1, tn), lambda m, j: (tok.mod_row(m), 0, gate_chunk * (d // tn) + j)),
        ],
        out_specs=pl.BlockSpec((tm, tn), lambda m, j: (m, j)),
        input_output_aliases={2: 0},
        compiler_params=_params("parallel", "parallel"),
        name=name,
    )(a, w, x_all, mod)


def _ffn_kernel(x_ref, nw_ref, sh_ref, sc_ref, g_ref, w1_ref, w3_ref, w2_ref, o_ref, h_scr, acc_scr):
    f = pl.program_id(1)

    @pl.when(f == 0)
    def _():
        h_scr[...] = _norm_mod(x_ref[...], nw_ref[...], sh_ref[...], sc_ref[...]).astype(BF16)
        acc_scr[...] = jnp.zeros_like(acc_scr)

    h = h_scr[...]
    u = _silu(_dot(h, w1_ref[...])) * _dot(h, w3_ref[...])
    acc_scr[...] += _dot(u.astype(BF16), w2_ref[...])

    @pl.when(f == pl.num_programs(1) - 1)
    def _():
        o_ref[...] = x_ref[...] + g_ref[...] * acc_scr[...]


def _ffn(tok, n_tiles, x_all, norm_w, mod, w1, w3, w2):
    d, dff = w1.shape
    tm = tok.tm
    tf = _pick(dff, (512, 256, 128))
    return pl.pallas_call(
        _ffn_kernel,
        out_shape=jax.ShapeDtypeStruct(x_all.shape, F32),
        grid=(n_tiles, dff // tf),
        in_specs=[
            pl.BlockSpec((tm, d), lambda m, f: (m, 0)),
            pl.BlockSpec((1, d), lambda m, f: (0, 0)),
            _mod_spec(tok, d, 3),
            _mod_spec(tok, d, 4),
            _mod_spec(tok, d, 5),
            pl.BlockSpec((d, tf), lambda m, f: (0, f)),
            pl.BlockSpec((d, tf), lambda m, f: (0, f)),
            pl.BlockSpec((tf, d), lambda m, f: (f, 0)),
        ],
        out_specs=pl.BlockSpec((tm, d), lambda m, f: (m, 0)),
        scratch_shapes=[pltpu.VMEM((tm, d), BF16), pltpu.VMEM((tm, d), F32)],
        input_output_aliases={0: 0},
        compiler_params=_params("parallel", "arbitrary"),
        name="ffn",
    )(x_all, norm_w.reshape(1, d), mod, mod, mod, w1, w3, w2)


def _na_bias_tables(rpb, rows):
    kh = min(WIN_H, rows)
    cols = np.arange(GRID_W)
    col_start = np.clip(cols - WIN_W // 2, 0, GRID_W - WIN_W)
    in_win = (cols[None, :] >= col_start[:, None]) & (cols[None, :] < col_start[:, None] + WIN_W)
    dc = np.clip(cols[None, :] - cols[:, None], 1 - WIN_W, WIN_W - 1) + WIN_W - 1
    off = np.zeros((3, ROW_GROUP, KEY_ROWS), np.int32)
    ok = np.zeros((3, ROW_GROUP, KEY_ROWS), bool)
    for t, (r0, start) in enumerate(((0, 0), (ROW_GROUP, 0), (rows - ROW_GROUP, rows - KEY_ROWS))):
        for i in range(ROW_GROUP):
            r = r0 + i
            rs = min(max(r - kh // 2, 0), rows - kh)
            for m in range(KEY_ROWS):
                j = start + m - rs
                ok[t, i, m] = 0 <= j < kh
                off[t, i, m] = min(max(rs - r + j + WIN_H - 1, 0), 2 * WIN_H - 2)
    heads = rpb.shape[0]
    onehot = ((dc[None] == np.arange(2 * WIN_W - 1)[:, None, None]) & in_win[None]).astype(np.float32)
    by_col = jnp.einsum("hod,dqk->hoqk", rpb.astype(F32), onehot, precision=lax.Precision.HIGHEST)
    by_col = by_col + np.where(in_win, 0.0, NEG_INF).astype(np.float32)
    masked = jnp.full((heads, GRID_W, GRID_W), NEG_INF, F32)
    blocks = [by_col[:, off[t, i, m]] if ok[t, i, m] else masked
              for t in range(3) for i in range(ROW_GROUP) for m in range(KEY_ROWS)]
    vals = jnp.stack(blocks, axis=1).reshape(heads, 3, ROW_GROUP, KEY_ROWS, GRID_W, GRID_W)
    vals = vals.transpose(0, 1, 2, 4, 3, 5)
    return vals.reshape(heads, 3, ROW_GROUP * GRID_W, KEY_ROWS * GRID_W)


def _softmax_pv(parts):
    m = functools.reduce(jnp.maximum, [jnp.max(s, axis=-1, keepdims=True) for s, _ in parts])
    ps = [jnp.exp(s - m) for s, _ in parts]
    den = functools.reduce(jnp.add, [jnp.sum(p, axis=-1, keepdims=True) for p in ps])
    num = functools.reduce(jnp.add, [_dot(p.astype(BF16), v) for p, (_, v) in zip(ps, parts)])
    return num / den


def _na_kernel(q_ref, k_ref, v_ref, qc_ref, kc_ref, vc_ref, bias_ref, o_ref, oc_ref, *, rows):
    kc = kc_ref[...]
    vc = vc_ref[...]
    n_groups = rows // ROW_GROUP
    gq = ROW_GROUP * GRID_W
    gk = KEY_ROWS * GRID_W

    def group(gi, carry):
        r0 = gi * ROW_GROUP
        start = jnp.clip(r0 - WIN_H // 2, 0, rows - KEY_ROWS)
        table = jnp.where(gi == 0, 0, jnp.where(gi == n_groups - 1, 2, 1))
        q = q_ref[pl.ds(pl.multiple_of(r0 * GRID_W, gq), gq), :]
        koff = pl.multiple_of(start * GRID_W, GRID_W)
        ks = k_ref[pl.ds(koff, gk), :]
        vs = v_ref[pl.ds(koff, gk), :]
        s_loc = _dot_nt(q, ks) * NA_SCALE + bias_ref[table]
        s_ctx = _dot_nt(q, kc) * NA_SCALE
        o = _softmax_pv([(s_loc, vs), (s_ctx, vc)])
        o_ref[pl.ds(pl.multiple_of(r0 * GRID_W, gq), gq), :] = o.astype(o_ref.dtype)
        return carry

    lax.fori_loop(0, n_groups, group, 0)
    s_cc = _dot_nt(qc_ref[...], kc) * NA_SCALE
    oc_ref[...] = _softmax_pv([(s_cc, vc)]).astype(oc_ref.dtype)


def _na_attention(tok, qkv, bias_tables, d):
    heads = d // HEAD
    seq, ctx, batch = tok.seq, tok.ctx, tok.batch
    rows = seq // GRID_W
    assert rows % ROW_GROUP == 0 and rows >= KEY_ROWS
    ctx_blk0 = tok.n_lat // ctx
    gq, gk = ROW_GROUP * GRID_W, KEY_ROWS * GRID_W

    def lat(kind):
        return pl.BlockSpec((seq, HEAD), lambda h, b: (b, kind * heads + h))

    def cx(kind):
        return pl.BlockSpec((ctx, HEAD), lambda h, b: (ctx_blk0 + b, kind * heads + h))

    o_lat, o_ctx = pl.pallas_call(
        functools.partial(_na_kernel, rows=rows),
        out_shape=(jax.ShapeDtypeStruct((tok.n_lat, d), BF16),
                   jax.ShapeDtypeStruct((batch * ctx, d), BF16)),
        grid=(heads, batch),
        in_specs=[lat(0), lat(1), lat(2), cx(0), cx(1), cx(2),
                  pl.BlockSpec((None, 3, gq, gk), lambda h, b: (h, 0, 0, 0))],
        out_specs=(pl.BlockSpec((seq, HEAD), lambda h, b: (b, h)),
                   pl.BlockSpec((ctx, HEAD), lambda h, b: (b, h))),
        compiler_params=_params("parallel", "parallel"),
        name="na_attention",
    )(qkv, qkv, qkv, qkv, qkv, qkv, bias_tables)
    return jnp.concatenate([o_lat, o_ctx], axis=0)


def _conv_kernel(x_ref, w_ref, o_ref, pad_scr, *, n_key_blocks, seq_len, row_chunk):
    c = pl.program_id(1)
    zeros = jnp.zeros((CONV_PAD, HEAD), F32)
    pad_scr[pl.ds(0, CONV_PAD), :] = zeros
    pad_scr[pl.ds(CONV_PAD + seq_len, CONV_PAD), :] = zeros
    pad_scr[pl.ds(CONV_PAD, seq_len), :] = x_ref[...].astype(F32)
    w = w_ref[...]
    is_qk = c < 2 * n_key_blocks
    scale = jnp.where(c < n_key_blocks, GDN_SCALE, 1.0)
    for r in range(seq_len // row_chunk):
        base = CONV_PAD + r * row_chunk - (CONV_TAPS - 1) // 2
        y = sum(pad_scr[pl.ds(base + j, row_chunk), :] * w[j:j + 1, :] for j in range(CONV_TAPS))
        y = _silu(y)
        nrm = lax.rsqrt(jnp.sum(y * y, axis=-1, keepdims=True) + EPS) * scale
        o_ref[pl.ds(r * row_chunk, row_chunk), :] = jnp.where(is_qk, y * nrm, y)


def _gdn_conv(qkvz, conv_w, n_seq, seq_len, blk0, key_dim):
    conv_dim = conv_w.shape[1]
    row_chunk = min(seq_len, 512)
    return pl.pallas_call(
        functools.partial(_conv_kernel, n_key_blocks=key_dim // HEAD, seq_len=seq_len, row_chunk=row_chunk),
        out_shape=jax.ShapeDtypeStruct((n_seq * seq_len, conv_dim), F32),
        grid=(n_seq, conv_dim // HEAD),
        in_specs=[pl.BlockSpec((seq_len, HEAD), lambda s, c: (blk0 + s, c)),
                  pl.BlockSpec((CONV_TAPS, HEAD), lambda s, c: (0, c))],
        out_specs=pl.BlockSpec((seq_len, HEAD), lambda s, c: (s, c)),
        scratch_shapes=[pltpu.VMEM((seq_len + 2 * CONV_PAD, HEAD), F32)],
        compiler_params=_params("parallel", "parallel"),
        name=f"gdn_conv_{seq_len}",
    )(qkvz, conv_w)


def _gates_kernel(ba_ref, alog_ref, dt_ref, o_ref, *, n_chunks, n_vheads):
    row = lax.broadcasted_iota(jnp.int32, (CHUNK, CHUNK), 0)
    col = lax.broadcasted_iota(jnp.int32, (CHUNK, CHUNK), 1)
    prefix = (row <= col).astype(F32)
    suffix = (row >= col).astype(F32)
    is_decay = (row % (2 * n_vheads)) >= n_vheads
    neg_a = -jnp.exp(alog_ref[...])
    for c in range(n_chunks):
        bt = ba_ref[pl.ds(c * CHUNK, CHUNK), :].T
        beta = 1.0 / (1.0 + jnp.exp(-bt))
        x = bt + dt_ref[...]
        g = neg_a * (jnp.maximum(x, 0.0) + jnp.log1p(jnp.exp(-jnp.abs(x))))
        cum = jnp.where(row < 2 * n_vheads, _dot_hi(g, prefix), _dot_hi(g, suffix))
        o_ref[c] = jnp.where(is_decay, cum, beta)


def _gdn_gates(ba, a_log, dt_bias):
    n_tok, n_col = ba.shape
    n_vheads = a_log.shape[1]
    assert n_col == CHUNK and 4 * n_vheads <= n_col
    zeros = jnp.zeros_like(a_log)
    pad = ((0, n_col - 4 * n_vheads), (0, 0))
    alog_col = jnp.pad(jnp.concatenate([zeros, a_log], axis=1).reshape(4 * n_vheads, 1), pad)
    dt_col = jnp.pad(jnp.concatenate([zeros, dt_bias], axis=1).reshape(4 * n_vheads, 1), pad)
    per_step = 4
    assert (n_tok // CHUNK) % per_step == 0
    return pl.pallas_call(
        functools.partial(_gates_kernel, n_chunks=per_step, n_vheads=n_vheads),
        out_shape=jax.ShapeDtypeStruct((n_tok // CHUNK, n_col, CHUNK), F32),
        grid=(n_tok // (CHUNK * per_step),),
        in_specs=[pl.BlockSpec((CHUNK * per_step, n_col), lambda i: (i, 0)),
                  pl.BlockSpec((n_col, 1), lambda i: (0, 0)),
                  pl.BlockSpec((n_col, 1), lambda i: (0, 0))],
        out_specs=pl.BlockSpec((per_step, n_col, CHUNK), lambda i: (i, 0, 0)),
        compiler_params=_params("parallel"),
        name="gdn_gates",
    )(ba, alog_col, dt_col)


def _bdot(a, b):
    return jnp.einsum("pmk,pkn->pmn", a, b, preferred_element_type=F32)


def _bdot_nt(a, b):
    return jnp.einsum("pmk,pnk->pmn", a, b, preferred_element_type=F32)


def _split_bf16(a):
    hi = a.astype(BF16)
    return hi, (a - hi.astype(F32)).astype(BF16)


def _bdot_3pass(a, b):
    ah, al = _split_bf16(a)
    bh, bl = _split_bf16(b)
    return _bdot(ah, bh) + (_bdot(ah, bl) + _bdot(al, bh))


def _unit_tri_inverse(l, row, col):
    n = l.shape[-1]
    eye = (row == col).astype(F32)
    d = jnp.where(row // TRI_BASE == col // TRI_BASE, l, 0.0)
    d2 = _bdot_3pass(d, d)
    pq = _bdot_3pass(jnp.concatenate([eye - d, d2], axis=1), d2)
    p = (eye - d) + pq[:, :n]
    x = p + _bdot_3pass(p, pq[:, n:])
    b = TRI_BASE
    while b < n:
        pair = (row // (2 * b) == col // (2 * b)) & (row // b != col // b)
        c16 = jnp.where(pair, l, 0.0).astype(BF16)
        x16 = x.astype(BF16)
        x = x - _bdot(x16, _bdot(c16, x16).astype(BF16))
        b *= 2
    return x


def _prep_kernel(q_ref, k_ref, v_ref, gates_ref, mq_ref, na_ref, egl_ref, *, n_vheads):
    hv = pl.program_id(1)
    cb = q_ref.shape[0] // CHUNK
    row = lax.broadcasted_iota(jnp.int32, (CHUNK, CHUNK), 0)
    col = lax.broadcasted_iota(jnp.int32, (CHUNK, CHUNK), 1)
    q = q_ref[...].reshape(cb, CHUNK, HEAD)
    k = k_ref[...].reshape(cb, CHUNK, HEAD)
    v = v_ref[...].reshape(cb, CHUNK, HEAD)
    k16 = k.astype(BF16)
    qk = _bdot_nt(q.astype(BF16), k16)

    def per_row(r):
        return jnp.stack([jnp.broadcast_to(r[c], (CHUNK, CHUNK)).T for c in range(cb)], axis=0)

    for z in range(2):
        incl = (row >= col) if z == 0 else (row <= col)
        strict = (row > col) if z == 0 else (row < col)
        beta_r = gates_ref[:, pl.ds(z * 2 * n_vheads + hv, 1), :]
        g_r = gates_ref[:, pl.ds(z * 2 * n_vheads + n_vheads + hv, 1), :]
        beta_c = per_row(beta_r)
        g_c = per_row(g_r)
        decay = jnp.where(incl, jnp.exp(jnp.where(incl, g_c - g_r, 0.0)), 0.0)
        k_beta = k * beta_c
        v_beta = v * beta_c
        l_mat = jnp.where(strict, _bdot_nt(k_beta.astype(BF16), k16) * decay, 0.0)
        t_inv = _unit_tri_inverse(l_mat, row, col)
        exp_g = jnp.exp(g_c)
        rhs = jnp.concatenate([k_beta * exp_g, v_beta], axis=2).astype(BF16)
        wu16 = _bdot(t_inv.astype(BF16), rhs).astype(BF16)
        last = CHUNK - 1 if z == 0 else 0
        g_last = g_c[:, last:last + 1, :]
        kd = k * jnp.exp(g_last - g_c)
        kd_t = jnp.stack([kd[c].T for c in range(cb)], axis=0).astype(BF16)
        a16 = jnp.where(incl, qk * decay, 0.0).astype(BF16)
        r = _bdot(jnp.concatenate([kd_t, a16], axis=1), wu16)
        qe = q * exp_g - r[:, CHUNK:, :HEAD]
        mq_ref[z] = jnp.concatenate([-r[:, :CHUNK, :HEAD], qe], axis=1).astype(BF16)
        na_ref[z] = r[:, :, HEAD:]
        egl = jnp.exp(g_last)
        for c in range(cb):
            egl_ref[z, pl.ds(c, 1), :] = egl[c]


def _gdn_prep(qkv_conv, gates, n_seq, seq_len, chunk0, key_dim, n_vheads):
    tb = min(seq_len, PREP_ROWS)
    n_tb = seq_len // tb
    cb = tb // CHUNK
    n_chunks = seq_len // CHUNK
    nk = key_dim // HEAD
    rep = n_vheads // nk

    def rows(col_of):
        return pl.BlockSpec((tb, HEAD), lambda s, h, t: (s * n_tb + t, col_of(h)))

    def out(dtype):
        return jax.ShapeDtypeStruct((n_seq, n_vheads, 2, n_chunks, 2 * CHUNK, HEAD), dtype)

    out_spec = pl.BlockSpec((None, None, 2, cb, 2 * CHUNK, HEAD), lambda s, h, t: (s, h, 0, t, 0, 0))
    return pl.pallas_call(
        functools.partial(_prep_kernel, n_vheads=n_vheads),
        out_shape=(out(BF16), out(F32), jax.ShapeDtypeStruct((n_seq, n_vheads, 2, n_chunks, HEAD), F32)),
        grid=(n_seq, n_vheads, n_tb),
        in_specs=[rows(lambda h: h // rep), rows(lambda h: nk + h // rep), rows(lambda h: 2 * nk + h),
                  pl.BlockSpec((cb, CHUNK, CHUNK), lambda s, h, t: (chunk0 // cb + s * n_tb + t, 0, 0))],
        out_specs=(out_spec, out_spec,
                   pl.BlockSpec((None, None, 2, cb, HEAD), lambda s, h, t: (s, h, 0, t, 0))),
        compiler_params=_params("parallel", "parallel", "parallel"),
        name=f"gdn_prep_{seq_len}",
    )(qkv_conv, qkv_conv, qkv_conv, gates)


def _scan_kernel(mq_ref, na_ref, egl_ref, s0_ref, z_ref, nw_ref, y_ref, sf_ref, s_scr, o_scr):
    n_chunks = mq_ref.shape[1]
    s_scr[...] = s0_ref[...]

    def step(c, carry):
        for z in range(2):
            cc = c if z == 0 else n_chunks - 1 - c
            s = s_scr[z]
            r = _dot(mq_ref[z, cc], s.astype(BF16)) + na_ref[z, cc]
            o_scr[z, pl.ds(pl.multiple_of(cc * CHUNK, CHUNK), CHUNK), :] = r[CHUNK:]
            s_scr[z] = s * egl_ref[z, pl.ds(cc, 1), :] + r[:CHUNK]
        return carry

    lax.fori_loop(0, n_chunks, step, 0)
    sf_ref[...] = s_scr[...]
    o = o_scr[0] + o_scr[1]
    ms = jnp.mean(o * o, axis=-1, keepdims=True)
    y = (o * lax.rsqrt(ms + EPS) * nw_ref[...]) * _silu(z_ref[...].astype(F32))
    y_ref[...] = y.astype(y_ref.dtype)


def _gdn_scan(factors, s0, qkvz, norm_w, n_seq, seq_len, blk0, z_col0, n_vheads):
    mq, na, egl = factors

    def fac(arr):
        nd = arr.ndim
        return pl.BlockSpec((None, None) + arr.shape[2:], lambda s, h: (s, h) + (0,) * (nd - 2))

    state_spec = pl.BlockSpec((None, None, 2, HEAD, HEAD), lambda s, h: (s, h, 0, 0, 0))
    return pl.pallas_call(
        _scan_kernel,
        out_shape=(jax.ShapeDtypeStruct((n_seq * seq_len, n_vheads * HEAD), BF16),
                   jax.ShapeDtypeStruct((n_seq, n_vheads, 2, HEAD, HEAD), F32)),
        grid=(n_seq, n_vheads),
        in_specs=[fac(mq), fac(na), fac(egl), state_spec,
                  pl.BlockSpec((seq_len, HEAD), lambda s, h: (blk0 + s, z_col0 + h)),
                  pl.BlockSpec((1, HEAD), lambda s, h: (0, 0))],
        out_specs=(pl.BlockSpec((seq_len, HEAD), lambda s, h: (s, h)), state_spec),
        scratch_shapes=[pltpu.VMEM((2, HEAD, HEAD), F32), pltpu.VMEM((2, seq_len, HEAD), F32)],
        compiler_params=_params("parallel", "parallel"),
        name=f"gdn_scan_{seq_len}",
    )(mq, na, egl, s0, qkvz, norm_w.reshape(1, HEAD))


def _gdn_mixer(tok, qkvz, ba, conv_w, a_log, dt_bias, norm_w):
    conv_dim = conv_w.shape[1]
    value_dim = qkvz.shape[1] - conv_dim
    key_dim = (conv_dim - value_dim) // 2
    n_vheads = value_dim // HEAD
    batch, seq, ctx = tok.batch, tok.seq, tok.ctx
    gates = _gdn_gates(ba, a_log, dt_bias)
    z_col0 = conv_dim // HEAD

    def run(n_seq, seq_len, blk0, chunk0, s0):
        conv = _gdn_conv(qkvz, conv_w, n_seq, seq_len, blk0, key_dim)
        factors = _gdn_prep(conv, gates, n_seq, seq_len, chunk0, key_dim, n_vheads)
        return _gdn_scan(factors, s0, qkvz, norm_w, n_seq, seq_len, blk0, z_col0, n_vheads)

    zero_state = jnp.zeros((batch, n_vheads, 2, HEAD, HEAD), F32)
    y_ctx, s_ctx = run(batch, ctx, tok.n_lat // ctx, tok.n_lat // CHUNK, zero_state)
    y_lat, _ = run(batch, seq, 0, 0, s_ctx)
    return jnp.concatenate([y_lat, y_ctx], axis=0)


def kernel(x, c, ctx, c_ctx, w_ada, b_ada, norm1_w, norm2_w, na_w_qkv, na_w_o, na_q_gain, na_k_gain, na_rpb, gdn_w_in, gdn_conv_w, gdn_w_ba, gdn_a_log, gdn_dt_bias, gdn_norm_w, gdn_w_o, ffn_w1, ffn_w3, ffn_w2):
    batch, seq, d = x.shape
    n_ctx = ctx.shape[1]
    depth = w_ada.shape[0]
    tok = _Tokens(batch, seq, n_ctx)
    assert batch + 1 <= 8

    x_all = jnp.concatenate([x.reshape(batch * seq, d), ctx.reshape(batch * n_ctx, d)], axis=0)
    cond = jnp.concatenate([c, c_ctx[None, :], jnp.zeros((8 - batch - 1, d), F32)], axis=0)
    mods = _ada(cond, w_ada, b_ada)

    for i in range(depth):
        update_ctx = i < depth - 1
        n_tiles = tok.all_tiles if update_ctx else tok.lat_tiles
        mod = mods[i].reshape(8, 1, 6 * d)
        j = i // 2
        if i % 2 == 0:
            qkv = _norm_mod_matmul(tok, x_all, norm1_w[i], mod, na_w_qkv[j].astype(BF16), BF16, "na_qkv",
                                   na_gains=(na_q_gain[j], na_k_gain[j]))
            tables = _na_bias_tables(na_rpb[j], seq // GRID_W)
            mixed = _na_attention(tok, qkv, tables, d)
            w_o = na_w_o[j]
        else:
            qkvz = _norm_mod_matmul(tok, x_all, norm1_w[i], mod, gdn_w_in[j].astype(BF16), BF16, "gdn_in")
            w_ba = jnp.concatenate([gdn_w_ba[j, 0], gdn_w_ba[j, 1]], axis=1).astype(BF16)
            w_ba = jnp.pad(w_ba, ((0, 0), (0, CHUNK - w_ba.shape[1])))
            ba = _norm_mod_matmul(tok, x_all, norm1_w[i], mod, w_ba, F32, "gdn_ba")
            mixed = _gdn_mixer(tok, qkvz, ba, gdn_conv_w[j], gdn_a_log[j], gdn_dt_bias[j], gdn_norm_w[j])
            w_o = gdn_w_o[j]
        x_all = _matmul_residual(tok, n_tiles, mixed, w_o.astype(BF16), x_all, mod, 2, "mixer_out")
        x_all = _ffn(tok, n_tiles, x_all, norm2_w[i], mod, ffn_w1[i].astype(BF16), ffn_w3[i].astype(BF16),
                     ffn_w2[i].astype(BF16))
    return x_all[:batch * seq].reshape(batch, seq, d)
```

```python
import functools

import jax
import jax.numpy as jnp
import numpy as np
from jax import lax
from jax.experimental import pallas as pl
from jax.experimental.pallas import tpu as pltpu

HEAD = 128
GRID_W = 64
WIN_H = 8
WIN_W = 16
ROW_GROUP = 4
KEY_ROWS = ROW_GROUP + WIN_H
CHUNK = 128
CONV_TAPS = 4
CONV_PAD = 8
EPS = 1e-6
NEG_INF = -1e30
NA_SCALE = HEAD ** -0.5
GDN_SCALE = HEAD ** -0.5
TOKEN_TILE = 512
PREP_ROWS = 1024
CTX_HEADS_PER_STEP = 4
VMEM_LIMIT = 48 * 1024 * 1024

F32 = jnp.float32
BF16 = jnp.bfloat16


def _params(*sem):
    return pltpu.CompilerParams(dimension_semantics=sem, vmem_limit_bytes=VMEM_LIMIT)


def _pick(n, cands):
    for c in cands:
        if n % c == 0:
            return c
    raise ValueError(f"no tile in {cands} divides {n}")


def _silu(x):
    return x / (1.0 + jnp.exp(-x))


def _dot(a, b):
    return jnp.dot(a, b, preferred_element_type=F32)


def _dot_nt(a, b):
    return lax.dot_general(a, b, (((1,), (1,)), ((), ())), preferred_element_type=F32)


def _dot_tn(a, b):
    return lax.dot_general(a, b, (((0,), (0,)), ((), ())), preferred_element_type=F32)


def _dot_hi(a, b):
    return jnp.dot(a, b, preferred_element_type=F32, precision=lax.Precision.HIGHEST)


def _ada_kernel(c_ref, w_ref, b_ref, o_ref):
    s = _silu(c_ref[...]).astype(BF16)
    o_ref[...] = _dot(s, w_ref[...].astype(BF16)) + b_ref[...]


def _ada(cond, w_ada, b_ada):
    depth, d, n = w_ada.shape
    tn = _pick(n, (1024, 512, 256, 128))
    return pl.pallas_call(
        _ada_kernel,
        out_shape=jax.ShapeDtypeStruct((depth, 8, n), F32),
        grid=(depth, n // tn),
        in_specs=[
            pl.BlockSpec((8, d), lambda l, j: (0, 0)),
            pl.BlockSpec((None, d, tn), lambda l, j: (l, 0, j)),
            pl.BlockSpec((None, 1, tn), lambda l, j: (l, 0, j)),
        ],
        out_specs=pl.BlockSpec((None, 8, tn), lambda l, j: (l, 0, j)),
        compiler_params=_params("parallel", "parallel"),
        name="ada_mod",
    )(cond, w_ada, b_ada.reshape(depth, 1, n))


class _Tokens:
    def __init__(self, batch, seq, ctx):
        self.batch, self.seq, self.ctx = batch, seq, ctx
        self.n_lat = batch * seq
        self.n_all = self.n_lat + batch * ctx
        self.tm = TOKEN_TILE
        assert seq % self.tm == 0 and (batch * ctx) % self.tm == 0
        self.lat_tiles = self.n_lat // self.tm
        self.all_tiles = self.n_all // self.tm
        self.tiles_per_batch = seq // self.tm

    def mod_row(self, t):
        return jnp.where(t < self.lat_tiles, t // self.tiles_per_batch, self.batch)


def _mod_spec(tok, d, chunk):
    return pl.BlockSpec((None, 1, d), lambda m, j: (tok.mod_row(m), 0, chunk))


def _norm_mod(x, nw, sh, sc):
    ms = jnp.mean(x * x, axis=-1, keepdims=True)
    return (x * lax.rsqrt(ms + EPS) * nw) * (1.0 + sc) + sh


def _nmm_kernel(x_ref, nw_ref, sh_ref, sc_ref, w_ref, o_ref, h_scr):
    @pl.when(pl.program_id(1) == 0)
    def _():
        h_scr[...] = _norm_mod(x_ref[...], nw_ref[...], sh_ref[...], sc_ref[...]).astype(BF16)

    o_ref[...] = _dot(h_scr[...], w_ref[...]).astype(o_ref.dtype)


def _nmm_na_kernel(x_ref, nw_ref, sh_ref, sc_ref, w_ref, qg_ref, kg_ref, o_ref, h_scr, *, tiles_per_kind):
    n = pl.program_id(1)

    @pl.when(n == 0)
    def _():
        h_scr[...] = _norm_mod(x_ref[...], nw_ref[...], sh_ref[...], sc_ref[...]).astype(BF16)

    acc = _dot(h_scr[...], w_ref[...])
    kind = n // tiles_per_kind

    @pl.when(kind == 2)
    def _():
        o_ref[...] = acc.astype(o_ref.dtype)

    @pl.when(kind < 2)
    def _():
        gain = jnp.where(kind == 0, qg_ref[...], kg_ref[...])
        for j in range(acc.shape[1] // HEAD):
            a = acc[:, j * HEAD:(j + 1) * HEAD]
            ms = jnp.mean(a * a, axis=-1, keepdims=True)
            o_ref[:, j * HEAD:(j + 1) * HEAD] = (a * lax.rsqrt(ms + EPS) * gain).astype(o_ref.dtype)


def _norm_mod_matmul(tok, x_all, norm_w, mod, w, out_dtype, name, na_gains=None):
    d, n = w.shape
    tm = tok.tm
    if na_gains is None:
        tn = _pick(n, (1024, 512, 256, 128))
        kern = _nmm_kernel
        extra_specs, extra_args = [], []
    else:
        tn = _pick(d, (1024, 512, 256, 128))
        kern = functools.partial(_nmm_na_kernel, tiles_per_kind=d // tn)
        extra_specs = [pl.BlockSpec((1, HEAD), lambda m, j: (0, 0))] * 2
        extra_args = [g.reshape(1, HEAD) for g in na_gains]
    return pl.pallas_call(
        kern,
        out_shape=jax.ShapeDtypeStruct((tok.n_all, n), out_dtype),
        grid=(tok.all_tiles, n // tn),
        in_specs=[
            pl.BlockSpec((tm, d), lambda m, j: (m, 0)),
            pl.BlockSpec((1, d), lambda m, j: (0, 0)),
            _mod_spec(tok, d, 0),
            _mod_spec(tok, d, 1),
            pl.BlockSpec((d, tn), lambda m, j: (0, j)),
        ] + extra_specs,
        out_specs=pl.BlockSpec((tm, tn), lambda m, j: (m, j)),
        scratch_shapes=[pltpu.VMEM((tm, d), BF16)],
        compiler_params=_params("parallel", "arbitrary"),
        name=name,
    )(x_all, norm_w.reshape(1, d), mod, mod, w, *extra_args)


def _mm_res_kernel(a_ref, w_ref, x_ref, g_ref, o_ref):
    o_ref[...] = x_ref[...] + g_ref[...] * _dot(a_ref[...], w_ref[...])


def _matmul_residual(tok, n_tiles, a, w, x_all, mod, gate_chunk, name):
    k, d = w.shape
    tm = tok.tm
    tn = _pick(d, (1024, 512, 256, 128))
    return pl.pallas_call(
        _mm_res_kernel,
        out_shape=jax.ShapeDtypeStruct(x_all.shape, F32),
        grid=(n_tiles, d // tn),
        in_specs=[
            pl.BlockSpec((tm, k), lambda m, j: (m, 0)),
            pl.BlockSpec((k, tn), lambda m, j: (0, j)),
            pl.BlockSpec((tm, tn), lambda m, j: (m, j)),
            pl.BlockSpec((None, 1, tn), lambda m, j: (tok.mod_row(m), 0, gate_chunk * (d // tn) + j)),
        ],
        out_specs=pl.BlockSpec((tm, tn), lambda m, j: (m, j)),
        input_output_aliases={2: 0},
        compiler_params=_params("parallel", "parallel"),
        name=name,
    )(a, w, x_all, mod)


def _ffn_kernel(x_ref, nw_ref, sh_ref, sc_ref, g_ref, w1_ref, w3_ref, w2_ref, o_ref, h_scr, acc_scr):
    f = pl.program_id(1)

    @pl.when(f == 0)
    def _():
        h_scr[...] = _norm_mod(x_ref[...], nw_ref[...], sh_ref[...], sc_ref[...]).astype(BF16)
        acc_scr[...] = jnp.zeros_like(acc_scr)

    h = h_scr[...]
    u = _silu(_dot(h, w1_ref[...])) * _dot(h, w3_ref[...])
    acc_scr[...] += _dot(u.astype(BF16), w2_ref[...])

    @pl.when(f == pl.num_programs(1) - 1)
    def _():
        o_ref[...] = x_ref[...] + g_ref[...] * acc_scr[...]


def _ffn(tok, n_tiles, x_all, norm_w, mod, w1, w3, w2):
    d, dff = w1.shape
    tm = tok.tm
    tf = _pick(dff, (512, 256, 128))
    return pl.pallas_call(
        _ffn_kernel,
        out_shape=jax.ShapeDtypeStruct(x_all.shape, F32),
        grid=(n_tiles, dff // tf),
        in_specs=[
            pl.BlockSpec((tm, d), lambda m, f: (m, 0)),
            pl.BlockSpec((1, d), lambda m, f: (0, 0)),
            _mod_spec(tok, d, 3),
            _mod_spec(tok, d, 4),
            _mod_spec(tok, d, 5),
            pl.BlockSpec((d, tf), lambda m, f: (0, f)),
            pl.BlockSpec((d, tf), lambda m, f: (0, f)),
            pl.BlockSpec((tf, d), lambda m, f: (f, 0)),
        ],
        out_specs=pl.BlockSpec((tm, d), lambda m, f: (m, 0)),
        scratch_shapes=[pltpu.VMEM((tm, d), BF16), pltpu.VMEM((tm, d), F32)],
        input_output_aliases={0: 0},
        compiler_params=_params("parallel", "arbitrary"),
        name="ffn",
    )(x_all, norm_w.reshape(1, d), mod, mod, mod, w1, w3, w2)


def _na_bias_tables(rpb, rows):
    kh = min(WIN_H, rows)
    cols = np.arange(GRID_W)
    col_start = np.clip(cols - WIN_W // 2, 0, GRID_W - WIN_W)
    in_win = (cols[None, :] >= col_start[:, None]) & (cols[None, :] < col_start[:, None] + WIN_W)
    dc = np.clip(cols[None, :] - cols[:, None], 1 - WIN_W, WIN_W - 1) + WIN_W - 1
    off = np.zeros((3, ROW_GROUP, KEY_ROWS), np.int32)
    ok = np.zeros((3, ROW_GROUP, KEY_ROWS), bool)
    for t, (r0, start) in enumerate(((0, 0), (ROW_GROUP, 0), (rows - ROW_GROUP, rows - KEY_ROWS))):
        for i in range(ROW_GROUP):
            r = r0 + i
            rs = min(max(r - kh // 2, 0), rows - kh)
            for m in range(KEY_ROWS):
                j = start + m - rs
                ok[t, i, m] = 0 <= j < kh
                off[t, i, m] = min(max(rs - r + j + WIN_H - 1, 0), 2 * WIN_H - 2)
    heads = rpb.shape[0]
    onehot = ((dc[None] == np.arange(2 * WIN_W - 1)[:, None, None]) & in_win[None]).astype(np.float32)
    by_col = jnp.einsum("hod,dqk->hoqk", rpb.astype(F32), onehot, precision=lax.Precision.HIGHEST)
    by_col = by_col + np.where(in_win, 0.0, NEG_INF).astype(np.float32)
    masked = jnp.full((heads, GRID_W, GRID_W), NEG_INF, F32)
    blocks = [by_col[:, off[t, i, m]] if ok[t, i, m] else masked
              for t in range(3) for i in range(ROW_GROUP) for m in range(KEY_ROWS)]
    vals = jnp.stack(blocks, axis=1).reshape(heads, 3, ROW_GROUP, KEY_ROWS, GRID_W, GRID_W)
    vals = vals.transpose(0, 1, 2, 4, 3, 5)
    return vals.reshape(heads, 3, ROW_GROUP * GRID_W, KEY_ROWS * GRID_W)


def _softmax_pv(parts):
    m = functools.reduce(jnp.maximum, [jnp.max(s, axis=-1, keepdims=True) for s, _ in parts])
    ps = [jnp.exp(s - m) for s, _ in parts]
    den = functools.reduce(jnp.add, [jnp.sum(p, axis=-1, keepdims=True) for p in ps])
    num = functools.reduce(jnp.add, [_dot(p.astype(BF16), v) for p, (_, v) in zip(ps, parts)])
    return num / den


def _na_kernel(q_ref, k_ref, v_ref, qc_ref, kc_ref, vc_ref, bias_ref, o_ref, oc_ref, *, rows):
    kc = kc_ref[...]
    vc = vc_ref[...]
    n_groups = rows // ROW_GROUP
    gq = ROW_GROUP * GRID_W
    gk = KEY_ROWS * GRID_W

    def group(gi, carry):
        r0 = gi * ROW_GROUP
        start = jnp.clip(r0 - WIN_H // 2, 0, rows - KEY_ROWS)
        table = jnp.where(gi == 0, 0, jnp.where(gi == n_groups - 1, 2, 1))
        q = q_ref[pl.ds(pl.multiple_of(r0 * GRID_W, gq), gq), :]
        koff = pl.multiple_of(start * GRID_W, GRID_W)
        ks = k_ref[pl.ds(koff, gk), :]
        vs = v_ref[pl.ds(koff, gk), :]
        s_loc = _dot_nt(q, ks) * NA_SCALE + bias_ref[table]
        s_ctx = _dot_nt(q, kc) * NA_SCALE
        o = _softmax_pv([(s_loc, vs), (s_ctx, vc)])
        o_ref[pl.ds(pl.multiple_of(r0 * GRID_W, gq), gq), :] = o.astype(o_ref.dtype)
        return carry

    lax.fori_loop(0, n_groups, group, 0, unroll=2)
    s_cc = _dot_nt(qc_ref[...], kc) * NA_SCALE
    oc_ref[...] = _softmax_pv([(s_cc, vc)]).astype(oc_ref.dtype)


def _na_attention(tok, qkv, bias_tables, d):
    heads = d // HEAD
    seq, ctx, batch = tok.seq, tok.ctx, tok.batch
    rows = seq // GRID_W
    assert rows % ROW_GROUP == 0 and rows >= KEY_ROWS
    ctx_blk0 = tok.n_lat // ctx
    gq, gk = ROW_GROUP * GRID_W, KEY_ROWS * GRID_W

    def lat(kind):
        return pl.BlockSpec((seq, HEAD), lambda h, b: (b, kind * heads + h))

    def cx(kind):
        return pl.BlockSpec((ctx, HEAD), lambda h, b: (ctx_blk0 + b, kind * heads + h))

    o_lat, o_ctx = pl.pallas_call(
        functools.partial(_na_kernel, rows=rows),
        out_shape=(jax.ShapeDtypeStruct((tok.n_lat, d), BF16),
                   jax.ShapeDtypeStruct((batch * ctx, d), BF16)),
        grid=(heads, batch),
        in_specs=[lat(0), lat(1), lat(2), cx(0), cx(1), cx(2),
                  pl.BlockSpec((None, 3, gq, gk), lambda h, b: (h, 0, 0, 0))],
        out_specs=(pl.BlockSpec((seq, HEAD), lambda h, b: (b, h)),
                   pl.BlockSpec((ctx, HEAD), lambda h, b: (b, h))),
        compiler_params=_params("parallel", "parallel"),
        name="na_attention",
    )(qkv, qkv, qkv, qkv, qkv, qkv, bias_tables)
    return jnp.concatenate([o_lat, o_ctx], axis=0)


def _conv_kernel(x_ref, w_ref, o_ref, pad_scr, *, n_key_blocks, seq_len, row_chunk):
    c = pl.program_id(1)
    zeros = jnp.zeros((CONV_PAD, HEAD), F32)
    pad_scr[pl.ds(0, CONV_PAD), :] = zeros
    pad_scr[pl.ds(CONV_PAD + seq_len, CONV_PAD), :] = zeros
    pad_scr[pl.ds(CONV_PAD, seq_len), :] = x_ref[...].astype(F32)
    w = w_ref[...]
    is_qk = c < 2 * n_key_blocks
    scale = jnp.where(c < n_key_blocks, GDN_SCALE, 1.0)
    for r in range(seq_len // row_chunk):
        base = CONV_PAD + r * row_chunk - (CONV_TAPS - 1) // 2
        y = sum(pad_scr[pl.ds(base + j, row_chunk), :] * w[j:j + 1, :] for j in range(CONV_TAPS))
        y = _silu(y)
        nrm = lax.rsqrt(jnp.sum(y * y, axis=-1, keepdims=True) + EPS) * scale
        o_ref[pl.ds(r * row_chunk, row_chunk), :] = jnp.where(is_qk, y * nrm, y)


def _gdn_conv(qkvz, conv_w, n_seq, seq_len, blk0, key_dim):
    conv_dim = conv_w.shape[1]
    row_chunk = min(seq_len, 512)
    return pl.pallas_call(
        functools.partial(_conv_kernel, n_key_blocks=key_dim // HEAD, seq_len=seq_len, row_chunk=row_chunk),
        out_shape=jax.ShapeDtypeStruct((n_seq * seq_len, conv_dim), F32),
        grid=(n_seq, conv_dim // HEAD),
        in_specs=[pl.BlockSpec((seq_len, HEAD), lambda s, c: (blk0 + s, c)),
                  pl.BlockSpec((CONV_TAPS, HEAD), lambda s, c: (0, c))],
        out_specs=pl.BlockSpec((seq_len, HEAD), lambda s, c: (s, c)),
        scratch_shapes=[pltpu.VMEM((seq_len + 2 * CONV_PAD, HEAD), F32)],
        compiler_params=_params("parallel", "parallel"),
        name=f"gdn_conv_{seq_len}",
    )(qkvz, conv_w)


def _gates_kernel(ba_ref, alog_ref, dt_ref, o_ref, *, n_chunks, n_vheads):
    row = lax.broadcasted_iota(jnp.int32, (CHUNK, CHUNK), 0)
    col = lax.broadcasted_iota(jnp.int32, (CHUNK, CHUNK), 1)
    prefix = (row <= col).astype(F32)
    suffix = (row >= col).astype(F32)
    is_decay = (row % (2 * n_vheads)) >= n_vheads
    neg_a = -jnp.exp(alog_ref[...])
    for c in range(n_chunks):
        bt = ba_ref[pl.ds(c * CHUNK, CHUNK), :].T
        beta = 1.0 / (1.0 + jnp.exp(-bt))
        x = bt + dt_ref[...]
        g = neg_a * (jnp.maximum(x, 0.0) + jnp.log1p(jnp.exp(-jnp.abs(x))))
        cum = jnp.where(row < 2 * n_vheads, _dot_hi(g, prefix), _dot_hi(g, suffix))
        o_ref[c] = jnp.where(is_decay, cum, beta)


def _gdn_gates(ba, a_log, dt_bias):
    n_tok, n_col = ba.shape
    n_vheads = a_log.shape[1]
    assert n_col == CHUNK and 4 * n_vheads <= n_col
    zeros = jnp.zeros_like(a_log)
    pad = ((0, n_col - 4 * n_vheads), (0, 0))
    alog_col = jnp.pad(jnp.concatenate([zeros, a_log], axis=1).reshape(4 * n_vheads, 1), pad)
    dt_col = jnp.pad(jnp.concatenate([zeros, dt_bias], axis=1).reshape(4 * n_vheads, 1), pad)
    per_step = 4
    assert (n_tok // CHUNK) % per_step == 0
    return pl.pallas_call(
        functools.partial(_gates_kernel, n_chunks=per_step, n_vheads=n_vheads),
        out_shape=jax.ShapeDtypeStruct((n_tok // CHUNK, n_col, CHUNK), F32),
        grid=(n_tok // (CHUNK * per_step),),
        in_specs=[pl.BlockSpec((CHUNK * per_step, n_col), lambda i: (i, 0)),
                  pl.BlockSpec((n_col, 1), lambda i: (0, 0)),
                  pl.BlockSpec((n_col, 1), lambda i: (0, 0))],
        out_specs=pl.BlockSpec((per_step, n_col, CHUNK), lambda i: (i, 0, 0)),
        compiler_params=_params("parallel"),
        name="gdn_gates",
    )(ba, alog_col, dt_col)


def _bdot(a, b):
    return jnp.einsum("pmk,pkn->pmn", a, b, preferred_element_type=F32)


def _bdot_nt(a, b):
    return jnp.einsum("pmk,pnk->pmn", a, b, preferred_element_type=F32)


def _unit_tri_inverse(l, row, col):
    n = l.shape[-1]
    b = 2
    x = (row == col).astype(F32) - jnp.where(row // b == col // b, l, 0.0)
    while b < n:
        pair = (row // (2 * b) == col // (2 * b)) & (row // b != col // b)
        c16 = jnp.where(pair, l, 0.0).astype(BF16)
        x16 = x.astype(BF16)
        x = x - _bdot(x16, _bdot(c16, x16).astype(BF16))
        b *= 2
    return x


def _chunk_factors(q, k, v, beta_r, g_r, row, col, reverse):
    cb = q.shape[0]

    def per_row(r):
        return jnp.stack([jnp.broadcast_to(r[c], (CHUNK, CHUNK)).T for c in range(cb)], axis=0)

    incl = (row <= col) if reverse else (row >= col)
    strict = (row < col) if reverse else (row > col)
    beta_c = per_row(beta_r)
    g_c = per_row(g_r)
    decay = jnp.where(incl, jnp.exp(jnp.where(incl, g_c - g_r, 0.0)), 0.0)
    k16 = k.astype(BF16)
    k_beta = k * beta_c
    v_beta = v * beta_c
    l_mat = jnp.where(strict, _bdot_nt(k_beta.astype(BF16), k16) * decay, 0.0)
    t_inv = _unit_tri_inverse(l_mat, row, col)
    exp_g = jnp.exp(g_c)
    rhs = jnp.concatenate([k_beta * exp_g, v_beta], axis=2).astype(BF16)
    wu16 = _bdot(t_inv.astype(BF16), rhs).astype(BF16)
    last = 0 if reverse else CHUNK - 1
    g_last = g_c[:, last:last + 1, :]
    kd = k * jnp.exp(g_last - g_c)
    kd_t = jnp.stack([kd[c].T for c in range(cb)], axis=0).astype(BF16)
    a16 = jnp.where(incl, _bdot_nt(q.astype(BF16), k16) * decay, 0.0).astype(BF16)
    r = _bdot(jnp.concatenate([kd_t, a16], axis=1), wu16)
    qe = q * exp_g - r[:, CHUNK:, :HEAD]
    mq = jnp.concatenate([-r[:, :CHUNK, :HEAD], qe], axis=1).astype(BF16)
    return mq, r[:, :, HEAD:], jnp.exp(g_last)


def _gdn_core_kernel(qf_ref, kf_ref, vf_ref, qb_ref, kb_ref, vb_ref, gf_ref, gb_ref, s0_ref, z_ref, nw_ref,
                     y_ref, sf_ref, s_scr, o_scr, *, n_vheads, rep):
    grp = pl.program_id(1)
    t = pl.program_id(2)
    n_t = pl.num_programs(2)
    tb = qf_ref.shape[0]
    cb = tb // CHUNK
    heads = s_scr.shape[0]
    row = lax.broadcasted_iota(jnp.int32, (CHUNK, CHUNK), 0)
    col = lax.broadcasted_iota(jnp.int32, (CHUNK, CHUNK), 1)

    @pl.when(t == 0)
    def _():
        s_scr[...] = s0_ref[...]

    for hh in range(heads):
        hv = grp * heads + hh
        ksl = slice((hh // rep) * HEAD, (hh // rep + 1) * HEAD)
        vsl = slice(hh * HEAD, (hh + 1) * HEAD)
        for z in range(2):
            q_ref, k_ref, v_ref, gates_ref = (qf_ref, kf_ref, vf_ref, gf_ref) if z == 0 else (qb_ref, kb_ref, vb_ref, gb_ref)
            mq, na, egl = _chunk_factors(
                q_ref[:, ksl].reshape(cb, CHUNK, HEAD), k_ref[:, ksl].reshape(cb, CHUNK, HEAD),
                v_ref[:, vsl].reshape(cb, CHUNK, HEAD),
                gates_ref[:, pl.ds(z * 2 * n_vheads + hv, 1), :],
                gates_ref[:, pl.ds(z * 2 * n_vheads + n_vheads + hv, 1), :],
                row, col, reverse=(z == 1))
            blk = t if z == 0 else n_t - 1 - t
            s = s_scr[hh, z]
            for c in (range(cb) if z == 0 else reversed(range(cb))):
                r = _dot(mq[c], s.astype(BF16)) + na[c]
                o_scr[hh, z, pl.ds(pl.multiple_of(blk * tb + c * CHUNK, CHUNK), CHUNK), :] = r[CHUNK:]
                s = s * egl[c] + r[:CHUNK]
            s_scr[hh, z] = s

    @pl.when(t == n_t - 1)
    def _():
        sf_ref[...] = s_scr[...]
        for hh in range(heads):
            o = o_scr[hh, 0] + o_scr[hh, 1]
            ms = jnp.mean(o * o, axis=-1, keepdims=True)
            zz = z_ref[:, hh * HEAD:(hh + 1) * HEAD].astype(F32)
            y = (o * lax.rsqrt(ms + EPS) * nw_ref[...]) * _silu(zz)
            y_ref[:, hh * HEAD:(hh + 1) * HEAD] = y.astype(y_ref.dtype)


def _gdn_core(qkv_conv, gates, s0, qkvz, norm_w, n_seq, seq_len, blk0, chunk0, key_dim, n_vheads, heads):
    tb = min(seq_len, PREP_ROWS)
    n_tb = seq_len // tb
    cb = tb // CHUNK
    nk = key_dim // HEAD
    rep = n_vheads // nk
    kw = max(1, heads // rep)
    assert heads == 1 or heads % rep == 0
    assert n_vheads % heads == 0 and nk % kw == 0 and (2 * nk) % heads == 0 and chunk0 % cb == 0
    conv_blocks = 2 * nk + n_vheads
    assert conv_blocks % heads == 0

    def fwd(s, g, t):
        return s * n_tb + t

    def bwd(s, g, t):
        return s * n_tb + n_tb - 1 - t

    def specs(tblk):
        return [
            pl.BlockSpec((tb, kw * HEAD), lambda s, g, t: (tblk(s, g, t), (g * heads) // (rep * kw))),
            pl.BlockSpec((tb, kw * HEAD), lambda s, g, t: (tblk(s, g, t), nk // kw + (g * heads) // (rep * kw))),
            pl.BlockSpec((tb, heads * HEAD), lambda s, g, t: (tblk(s, g, t), 2 * nk // heads + g)),
        ]

    def gate_spec(tblk):
        return pl.BlockSpec((cb, CHUNK, CHUNK), lambda s, g, t: (chunk0 // cb + tblk(s, g, t), 0, 0))

    state_spec = pl.BlockSpec((None, heads, 2, HEAD, HEAD), lambda s, g, t: (s, g, 0, 0, 0))
    return pl.pallas_call(
        functools.partial(_gdn_core_kernel, n_vheads=n_vheads, rep=rep),
        out_shape=(jax.ShapeDtypeStruct((n_seq * seq_len, n_vheads * HEAD), BF16),
                   jax.ShapeDtypeStruct((n_seq, n_vheads, 2, HEAD, HEAD), F32)),
        grid=(n_seq, n_vheads // heads, n_tb),
        in_specs=specs(fwd) + specs(bwd) + [gate_spec(fwd), gate_spec(bwd), state_spec,
                  pl.BlockSpec((seq_len, heads * HEAD), lambda s, g, t: (blk0 + s, conv_blocks // heads + g)),
                  pl.BlockSpec((1, HEAD), lambda s, g, t: (0, 0))],
        out_specs=(pl.BlockSpec((seq_len, heads * HEAD), lambda s, g, t: (s, g)), state_spec),
        scratch_shapes=[pltpu.VMEM((heads, 2, HEAD, HEAD), F32), pltpu.VMEM((heads, 2, seq_len, HEAD), F32)],
        compiler_params=_params("parallel", "parallel", "arbitrary"),
        name=f"gdn_core_{seq_len}",
    )(qkv_conv, qkv_conv, qkv_conv, qkv_conv, qkv_conv, qkv_conv, gates, gates, s0, qkvz, norm_w.reshape(1, HEAD))


def _gdn_mixer(tok, qkvz, ba, conv_w, a_log, dt_bias, norm_w):
    conv_dim = conv_w.shape[1]
    value_dim = qkvz.shape[1] - conv_dim
    key_dim = (conv_dim - value_dim) // 2
    n_vheads = value_dim // HEAD
    batch, seq, ctx = tok.batch, tok.seq, tok.ctx
    gates = _gdn_gates(ba, a_log, dt_bias)

    def run(n_seq, seq_len, blk0, chunk0, s0, heads):
        conv = _gdn_conv(qkvz, conv_w, n_seq, seq_len, blk0, key_dim)
        return _gdn_core(conv, gates, s0, qkvz, norm_w, n_seq, seq_len, blk0, chunk0, key_dim, n_vheads, heads)

    zero_state = jnp.zeros((batch, n_vheads, 2, HEAD, HEAD), F32)
    y_ctx, s_ctx = run(batch, ctx, tok.n_lat // ctx, tok.n_lat // CHUNK, zero_state, CTX_HEADS_PER_STEP)
    y_lat, _ = run(batch, seq, 0, 0, s_ctx, 1)
    return jnp.concatenate([y_lat, y_ctx], axis=0)


def kernel(x, c, ctx, c_ctx, w_ada, b_ada, norm1_w, norm2_w, na_w_qkv, na_w_o, na_q_gain, na_k_gain, na_rpb, gdn_w_in, gdn_conv_w, gdn_w_ba, gdn_a_log, gdn_dt_bias, gdn_norm_w, gdn_w_o, ffn_w1, ffn_w3, ffn_w2):
    batch, seq, d = x.shape
    n_ctx = ctx.shape[1]
    depth = w_ada.shape[0]
    tok = _Tokens(batch, seq, n_ctx)
    assert batch + 1 <= 8

    x_all = jnp.concatenate([x.reshape(batch * seq, d), ctx.reshape(batch * n_ctx, d)], axis=0)
    cond = jnp.concatenate([c, c_ctx[None, :], jnp.zeros((8 - batch - 1, d), F32)], axis=0)
    mods = _ada(cond, w_ada, b_ada)

    for i in range(depth):
        update_ctx = i < depth - 1
        n_tiles = tok.all_tiles if update_ctx else tok.lat_tiles
        mod = mods[i].reshape(8, 1, 6 * d)
        j = i // 2
        if i % 2 == 0:
            qkv = _norm_mod_matmul(tok, x_all, norm1_w[i], mod, na_w_qkv[j].astype(BF16), BF16, "na_qkv",
                                   na_gains=(na_q_gain[j], na_k_gain[j]))
            tables = _na_bias_tables(na_rpb[j], seq // GRID_W)
            mixed = _na_attention(tok, qkv, tables, d)
            w_o = na_w_o[j]
        else:
            qkvz = _norm_mod_matmul(tok, x_all, norm1_w[i], mod, gdn_w_in[j].astype(BF16), BF16, "gdn_in")
            w_ba = jnp.concatenate([gdn_w_ba[j, 0], gdn_w_ba[j, 1]], axis=1).astype(BF16)
            w_ba = jnp.pad(w_ba, ((0, 0), (0, CHUNK - w_ba.shape[1])))
            ba = _norm_mod_matmul(tok, x_all, norm1_w[i], mod, w_ba, F32, "gdn_ba")
            mixed = _gdn_mixer(tok, qkvz, ba, gdn_conv_w[j], gdn_a_log[j], gdn_dt_bias[j], gdn_norm_w[j])
            w_o = gdn_w_o[j]
        x_all = _matmul_residual(tok, n_tiles, mixed, w_o.astype(BF16), x_all, mod, 2, "mixer_out")
        x_all = _ffn(tok, n_tiles, x_all, norm2_w[i], mod, ffn_w1[i].astype(BF16), ffn_w3[i].astype(BF16),
                     ffn_w2[i].astype(BF16))
    return x_all[:batch * seq].reshape(batch, seq, d)
```

```python
import functools

import jax
import jax.numpy as jnp
import numpy as np
from jax import lax
from jax.experimental import pallas as pl
from jax.experimental.pallas import tpu as pltpu

HEAD = 128
GRID_W = 64
WIN_H = 8
WIN_W = 16
ROW_GROUP = 4
KEY_ROWS = ROW_GROUP + WIN_H
CHUNK = 128
CONV_TAPS = 4
CONV_PAD = 8
EPS = 1e-6
NEG_INF = -1e30
NA_SCALE = HEAD ** -0.5
GDN_SCALE = HEAD ** -0.5
TOKEN_TILE = 512
NORM_SLAB = 256
PREP_ROWS = 1024
CTX_HEADS_PER_STEP = 4
LAT_HEADS_PER_STEP = 2
VMEM_LIMIT = 48 * 1024 * 1024

F32 = jnp.float32
BF16 = jnp.bfloat16


def _params(*sem):
    return pltpu.CompilerParams(dimension_semantics=sem, vmem_limit_bytes=VMEM_LIMIT)


def _pick(n, cands):
    for c in cands:
        if n % c == 0:
            return c
    raise ValueError(f"no tile in {cands} divides {n}")


def _silu(x):
    return x / (1.0 + jnp.exp(-x))


def _dot(a, b):
    return jnp.dot(a, b, preferred_element_type=F32)


def _dot_nt(a, b):
    return lax.dot_general(a, b, (((1,), (1,)), ((), ())), preferred_element_type=F32)


def _dot_tn(a, b):
    return lax.dot_general(a, b, (((0,), (0,)), ((), ())), preferred_element_type=F32)


def _dot_hi(a, b):
    return jnp.dot(a, b, preferred_element_type=F32, precision=lax.Precision.HIGHEST)


def _ada_kernel(c_ref, w_ref, b_ref, o_ref):
    s = _silu(c_ref[...]).astype(BF16)
    o_ref[...] = _dot(s, w_ref[...].astype(BF16)) + b_ref[...]


def _ada(cond, w_ada, b_ada):
    depth, d, n = w_ada.shape
    tn = _pick(n, (1024, 512, 256, 128))
    return pl.pallas_call(
        _ada_kernel,
        out_shape=jax.ShapeDtypeStruct((depth, 8, n), F32),
        grid=(depth, n // tn),
        in_specs=[
            pl.BlockSpec((8, d), lambda l, j: (0, 0)),
            pl.BlockSpec((None, d, tn), lambda l, j: (l, 0, j)),
            pl.BlockSpec((None, 1, tn), lambda l, j: (l, 0, j)),
        ],
        out_specs=pl.BlockSpec((None, 8, tn), lambda l, j: (l, 0, j)),
        compiler_params=_params("parallel", "parallel"),
        name="ada_mod",
    )(cond, w_ada, b_ada.reshape(depth, 1, n))


class _Tokens:
    def __init__(self, batch, seq, ctx):
        self.batch, self.seq, self.ctx = batch, seq, ctx
        self.n_lat = batch * seq
        self.n_all = self.n_lat + batch * ctx
        self.tm = TOKEN_TILE
        assert seq % self.tm == 0 and (batch * ctx) % self.tm == 0
        self.lat_tiles = self.n_lat // self.tm
        self.all_tiles = self.n_all // self.tm
        self.tiles_per_batch = seq // self.tm

    def mod_row(self, t):
        return jnp.where(t < self.lat_tiles, t // self.tiles_per_batch, self.batch)


def _mod_spec(tok, d, chunk):
    return pl.BlockSpec((None, 1, d), lambda m, j: (tok.mod_row(m), 0, chunk))


def _norm_mod(x, nw, sh, sc):
    ms = jnp.mean(x * x, axis=-1, keepdims=True)
    return (x * lax.rsqrt(ms + EPS) * nw) * (1.0 + sc) + sh


def _nmm_kernel(x_ref, nw_ref, sh_ref, sc_ref, w_ref, o_ref, h_scr):
    @pl.when(pl.program_id(1) == 0)
    def _():
        h_scr[...] = _norm_mod(x_ref[...], nw_ref[...], sh_ref[...], sc_ref[...]).astype(BF16)

    o_ref[...] = _dot(h_scr[...], w_ref[...]).astype(o_ref.dtype)


def _nmm_na_kernel(x_ref, nw_ref, sh_ref, sc_ref, w_ref, qg_ref, kg_ref, o_ref, h_scr, *, tiles_per_kind):
    n = pl.program_id(1)

    @pl.when(n == 0)
    def _():
        h_scr[...] = _norm_mod(x_ref[...], nw_ref[...], sh_ref[...], sc_ref[...]).astype(BF16)

    kind = n // tiles_per_kind

    @pl.when(kind == 2)
    def _():
        o_ref[...] = _dot(h_scr[...], w_ref[...]).astype(o_ref.dtype)

    @pl.when(kind < 2)
    def _():
        gain = jnp.where(kind == 0, qg_ref[...], kg_ref[...])
        h = h_scr[...]
        for j in range(o_ref.shape[1] // NORM_SLAB):
            acc = _dot(h, w_ref[:, j * NORM_SLAB:(j + 1) * NORM_SLAB])
            for i in range(NORM_SLAB // HEAD):
                a = acc[:, i * HEAD:(i + 1) * HEAD]
                ms = jnp.mean(a * a, axis=-1, keepdims=True)
                lo = j * NORM_SLAB + i * HEAD
                o_ref[:, lo:lo + HEAD] = (a * lax.rsqrt(ms + EPS) * gain).astype(o_ref.dtype)


def _norm_mod_matmul(tok, x_all, norm_w, mod, w, out_dtype, name, na_gains=None):
    d, n = w.shape
    tm = tok.tm
    if na_gains is None:
        tn = _pick(n, (1024, 512, 256, 128))
        kern = _nmm_kernel
        extra_specs, extra_args = [], []
    else:
        tn = _pick(d, (1024, 512, 256, 128))
        kern = functools.partial(_nmm_na_kernel, tiles_per_kind=d // tn)
        extra_specs = [pl.BlockSpec((1, HEAD), lambda m, j: (0, 0))] * 2
        extra_args = [g.reshape(1, HEAD) for g in na_gains]
    return pl.pallas_call(
        kern,
        out_shape=jax.ShapeDtypeStruct((tok.n_all, n), out_dtype),
        grid=(tok.all_tiles, n // tn),
        in_specs=[
            pl.BlockSpec((tm, d), lambda m, j: (m, 0)),
            pl.BlockSpec((1, d), lambda m, j: (0, 0)),
            _mod_spec(tok, d, 0),
            _mod_spec(tok, d, 1),
            pl.BlockSpec((d, tn), lambda m, j: (0, j)),
        ] + extra_specs,
        out_specs=pl.BlockSpec((tm, tn), lambda m, j: (m, j)),
        scratch_shapes=[pltpu.VMEM((tm, d), BF16)],
        compiler_params=_params("parallel", "arbitrary"),
        name=name,
    )(x_all, norm_w.reshape(1, d), mod, mod, w, *extra_args)


def _mm_res_kernel(a_ref, w_ref, x_ref, g_ref, o_ref):
    o_ref[...] = x_ref[...] + g_ref[...] * _dot(a_ref[...], w_ref[...])


def _matmul_residual(tok, tile0, a, w, x_all, mod, gate_chunk, name):
    k, d = w.shape
    tm = tok.tm
    tn = _pick(d, (1024, 512, 256, 128))
    n_tiles = a.shape[0] // tm
    return pl.pallas_call(
        _mm_res_kernel,
        out_shape=jax.ShapeDtypeStruct(x_all.shape, F32),
        grid=(n_tiles, d // tn),
        in_specs=[
            pl.BlockSpec((tm, k), lambda m, j: (m, 0)),
            pl.BlockSpec((k, tn), lambda m, j: (0, j)),
            pl.BlockSpec((tm, tn), lambda m, j: (tile0 + m, j)),
            pl.BlockSpec((None, 1, tn), lambda m, j: (tok.mod_row(tile0 + m), 0, gate_chunk * (d // tn) + j)),
        ],
        out_specs=pl.BlockSpec((tm, tn), lambda m, j: (tile0 + m, j)),
        input_output_aliases={2: 0},
        compiler_params=_params("parallel", "parallel"),
        name=name,
    )(a, w, x_all, mod)


def _ffn_kernel(x_ref, nw_ref, sh_ref, sc_ref, g_ref, w1_ref, w3_ref, w2_ref, o_ref, h_scr, acc_scr):
    f = pl.program_id(1)

    @pl.when(f == 0)
    def _():
        h_scr[...] = _norm_mod(x_ref[...], nw_ref[...], sh_ref[...], sc_ref[...]).astype(BF16)
        acc_scr[...] = jnp.zeros_like(acc_scr)

    h = h_scr[...]
    u = _silu(_dot(h, w1_ref[...])) * _dot(h, w3_ref[...])
    acc_scr[...] += _dot(u.astype(BF16), w2_ref[...])

    @pl.when(f == pl.num_programs(1) - 1)
    def _():
        o_ref[...] = x_ref[...] + g_ref[...] * acc_scr[...]


def _ffn(tok, n_tiles, x_all, norm_w, mod, w1, w3, w2):
    d, dff = w1.shape
    tm = tok.tm
    tf = _pick(dff, (512, 256, 128))
    return pl.pallas_call(
        _ffn_kernel,
        out_shape=jax.ShapeDtypeStruct(x_all.shape, F32),
        grid=(n_tiles, dff // tf),
        in_specs=[
            pl.BlockSpec((tm, d), lambda m, f: (m, 0)),
            pl.BlockSpec((1, d), lambda m, f: (0, 0)),
            _mod_spec(tok, d, 3),
            _mod_spec(tok, d, 4),
            _mod_spec(tok, d, 5),
            pl.BlockSpec((d, tf), lambda m, f: (0, f)),
            pl.BlockSpec((d, tf), lambda m, f: (0, f)),
            pl.BlockSpec((tf, d), lambda m, f: (f, 0)),
        ],
        out_specs=pl.BlockSpec((tm, d), lambda m, f: (m, 0)),
        scratch_shapes=[pltpu.VMEM((tm, d), BF16), pltpu.VMEM((tm, d), F32)],
        input_output_aliases={0: 0},
        compiler_params=_params("parallel", "arbitrary"),
        name="ffn",
    )(x_all, norm_w.reshape(1, d), mod, mod, mod, w1, w3, w2)


def _na_bias_tables(rpb, rows):
    kh = min(WIN_H, rows)
    cols = np.arange(GRID_W)
    col_start = np.clip(cols - WIN_W // 2, 0, GRID_W - WIN_W)
    in_win = (cols[None, :] >= col_start[:, None]) & (cols[None, :] < col_start[:, None] + WIN_W)
    dc = np.clip(cols[None, :] - cols[:, None], 1 - WIN_W, WIN_W - 1) + WIN_W - 1
    off = np.zeros((3, ROW_GROUP, KEY_ROWS), np.int32)
    ok = np.zeros((3, ROW_GROUP, KEY_ROWS), bool)
    for t, (r0, start) in enumerate(((0, 0), (ROW_GROUP, 0), (rows - ROW_GROUP, rows - KEY_ROWS))):
        for i in range(ROW_GROUP):
            r = r0 + i
            rs = min(max(r - kh // 2, 0), rows - kh)
            for m in range(KEY_ROWS):
                j = start + m - rs
                ok[t, i, m] = 0 <= j < kh
                off[t, i, m] = min(max(rs - r + j + WIN_H - 1, 0), 2 * WIN_H - 2)
    heads = rpb.shape[0]
    onehot = ((dc[None] == np.arange(2 * WIN_W - 1)[:, None, None]) & in_win[None]).astype(np.float32)
    by_col = jnp.einsum("hod,dqk->hoqk", rpb.astype(F32), onehot, precision=lax.Precision.HIGHEST)
    by_col = by_col + np.where(in_win, 0.0, NEG_INF).astype(np.float32)
    masked = jnp.full((heads, GRID_W, GRID_W), NEG_INF, F32)
    blocks = [by_col[:, off[t, i, m]] if ok[t, i, m] else masked
              for t in range(3) for i in range(ROW_GROUP) for m in range(KEY_ROWS)]
    vals = jnp.stack(blocks, axis=1).reshape(heads, 3, ROW_GROUP, KEY_ROWS, GRID_W, GRID_W)
    vals = vals.transpose(0, 1, 2, 4, 3, 5)
    return vals.reshape(heads, 3, ROW_GROUP * GRID_W, KEY_ROWS * GRID_W)


def _softmax_pv(parts):
    m = functools.reduce(jnp.maximum, [jnp.max(s, axis=-1, keepdims=True) for s, _ in parts])
    ps = [jnp.exp(s - m) for s, _ in parts]
    den = functools.reduce(jnp.add, [jnp.sum(p, axis=-1, keepdims=True) for p in ps])
    num = functools.reduce(jnp.add, [_dot(p.astype(BF16), v) for p, (_, v) in zip(ps, parts)])
    return num / den


def _na_kernel(q_ref, k_ref, v_ref, qc_ref, kc_ref, vc_ref, bias_ref, o_ref, oc_ref, *, rows):
    kc = kc_ref[...]
    vc = vc_ref[...]
    n_groups = rows // ROW_GROUP
    gq = ROW_GROUP * GRID_W
    gk = KEY_ROWS * GRID_W

    def group(gi, carry):
        r0 = gi * ROW_GROUP
        start = jnp.clip(r0 - WIN_H // 2, 0, rows - KEY_ROWS)
        table = jnp.where(gi == 0, 0, jnp.where(gi == n_groups - 1, 2, 1))
        q = q_ref[pl.ds(pl.multiple_of(r0 * GRID_W, gq), gq), :]
        koff = pl.multiple_of(start * GRID_W, GRID_W)
        ks = k_ref[pl.ds(koff, gk), :]
        vs = v_ref[pl.ds(koff, gk), :]
        s_loc = _dot_nt(q, ks) * NA_SCALE + bias_ref[table]
        s_ctx = _dot_nt(q, kc) * NA_SCALE
        o = _softmax_pv([(s_loc, vs), (s_ctx, vc)])
        o_ref[pl.ds(pl.multiple_of(r0 * GRID_W, gq), gq), :] = o.astype(o_ref.dtype)
        return carry

    lax.fori_loop(0, n_groups, group, 0, unroll=2)
    s_cc = _dot_nt(qc_ref[...], kc) * NA_SCALE
    oc_ref[...] = _softmax_pv([(s_cc, vc)]).astype(oc_ref.dtype)


def _na_attention(tok, qkv, bias_tables, d):
    heads = d // HEAD
    seq, ctx, batch = tok.seq, tok.ctx, tok.batch
    rows = seq // GRID_W
    assert rows % ROW_GROUP == 0 and rows >= KEY_ROWS
    ctx_blk0 = tok.n_lat // ctx
    gq, gk = ROW_GROUP * GRID_W, KEY_ROWS * GRID_W

    def lat(kind):
        return pl.BlockSpec((seq, HEAD), lambda h, b: (b, kind * heads + h))

    def cx(kind):
        return pl.BlockSpec((ctx, HEAD), lambda h, b: (ctx_blk0 + b, kind * heads + h))

    o_lat, o_ctx = pl.pallas_call(
        functools.partial(_na_kernel, rows=rows),
        out_shape=(jax.ShapeDtypeStruct((tok.n_lat, d), BF16),
                   jax.ShapeDtypeStruct((batch * ctx, d), BF16)),
        grid=(heads, batch),
        in_specs=[lat(0), lat(1), lat(2), cx(0), cx(1), cx(2),
                  pl.BlockSpec((None, 3, gq, gk), lambda h, b: (h, 0, 0, 0))],
        out_specs=(pl.BlockSpec((seq, HEAD), lambda h, b: (b, h)),
                   pl.BlockSpec((ctx, HEAD), lambda h, b: (b, h))),
        compiler_params=_params("parallel", "parallel"),
        name="na_attention",
    )(qkv, qkv, qkv, qkv, qkv, qkv, bias_tables)
    return o_lat, o_ctx


def _conv_kernel(x_ref, w_ref, o_ref, pad_scr, *, n_key_blocks, seq_len, row_chunk):
    c = pl.program_id(1)
    zeros = jnp.zeros((CONV_PAD, HEAD), F32)
    pad_scr[pl.ds(0, CONV_PAD), :] = zeros
    pad_scr[pl.ds(CONV_PAD + seq_len, CONV_PAD), :] = zeros
    pad_scr[pl.ds(CONV_PAD, seq_len), :] = x_ref[...].astype(F32)
    w = w_ref[...]

    def run(normalise):
        scale = jnp.where(c < n_key_blocks, GDN_SCALE, 1.0)
        for r in range(seq_len // row_chunk):
            base = CONV_PAD + r * row_chunk - (CONV_TAPS - 1) // 2
            y = sum(pad_scr[pl.ds(base + j, row_chunk), :] * w[j:j + 1, :] for j in range(CONV_TAPS))
            y = _silu(y)
            if normalise:
                y = y * (lax.rsqrt(jnp.sum(y * y, axis=-1, keepdims=True) + EPS) * scale)
            o_ref[pl.ds(r * row_chunk, row_chunk), :] = y

    pl.when(c < 2 * n_key_blocks)(functools.partial(run, True))
    pl.when(c >= 2 * n_key_blocks)(functools.partial(run, False))


def _gdn_conv(qkvz, conv_w, n_seq, seq_len, blk0, key_dim):
    conv_dim = conv_w.shape[1]
    row_chunk = min(seq_len, 512)
    return pl.pallas_call(
        functools.partial(_conv_kernel, n_key_blocks=key_dim // HEAD, seq_len=seq_len, row_chunk=row_chunk),
        out_shape=jax.ShapeDtypeStruct((n_seq * seq_len, conv_dim), F32),
        grid=(n_seq, conv_dim // HEAD),
        in_specs=[pl.BlockSpec((seq_len, HEAD), lambda s, c: (blk0 + s, c)),
                  pl.BlockSpec((CONV_TAPS, HEAD), lambda s, c: (0, c))],
        out_specs=pl.BlockSpec((seq_len, HEAD), lambda s, c: (s, c)),
        scratch_shapes=[pltpu.VMEM((seq_len + 2 * CONV_PAD, HEAD), F32)],
        compiler_params=_params("parallel", "parallel"),
        name=f"gdn_conv_{seq_len}",
    )(qkvz, conv_w)


def _gates_kernel(ba_ref, alog_ref, dt_ref, o_ref, *, n_chunks, n_vheads):
    row = lax.broadcasted_iota(jnp.int32, (CHUNK, CHUNK), 0)
    col = lax.broadcasted_iota(jnp.int32, (CHUNK, CHUNK), 1)
    prefix = (row <= col).astype(F32)
    suffix = (row >= col).astype(F32)
    is_decay = (row % (2 * n_vheads)) >= n_vheads
    neg_a = -jnp.exp(alog_ref[...])
    for c in range(n_chunks):
        bt = ba_ref[pl.ds(c * CHUNK, CHUNK), :].T
        beta = 1.0 / (1.0 + jnp.exp(-bt))
        x = bt + dt_ref[...]
        g = neg_a * (jnp.maximum(x, 0.0) + jnp.log1p(jnp.exp(-jnp.abs(x))))
        cum = jnp.where(row < 2 * n_vheads, _dot_hi(g, prefix), _dot_hi(g, suffix))
        o_ref[c] = jnp.where(is_decay, cum, beta)


def _gdn_gates(ba, a_log, dt_bias):
    n_tok, n_col = ba.shape
    n_vheads = a_log.shape[1]
    assert n_col == CHUNK and 4 * n_vheads <= n_col
    zeros = jnp.zeros_like(a_log)
    pad = ((0, n_col - 4 * n_vheads), (0, 0))
    alog_col = jnp.pad(jnp.concatenate([zeros, a_log], axis=1).reshape(4 * n_vheads, 1), pad)
    dt_col = jnp.pad(jnp.concatenate([zeros, dt_bias], axis=1).reshape(4 * n_vheads, 1), pad)
    per_step = 4
    assert (n_tok // CHUNK) % per_step == 0
    return pl.pallas_call(
        functools.partial(_gates_kernel, n_chunks=per_step, n_vheads=n_vheads),
        out_shape=jax.ShapeDtypeStruct((n_tok // CHUNK, n_col, CHUNK), F32),
        grid=(n_tok // (CHUNK * per_step),),
        in_specs=[pl.BlockSpec((CHUNK * per_step, n_col), lambda i: (i, 0)),
                  pl.BlockSpec((n_col, 1), lambda i: (0, 0)),
                  pl.BlockSpec((n_col, 1), lambda i: (0, 0))],
        out_specs=pl.BlockSpec((per_step, n_col, CHUNK), lambda i: (i, 0, 0)),
        compiler_params=_params("parallel"),
        name="gdn_gates",
    )(ba, alog_col, dt_col)


def _bdot(a, b):
    return jnp.einsum("pmk,pkn->pmn", a, b, preferred_element_type=F32)


def _bdot_nt(a, b):
    return jnp.einsum("pmk,pnk->pmn", a, b, preferred_element_type=F32)


def _unit_tri_inverse(l, row, col, tick):
    n = l.shape[-1]
    b = 2
    x = (row == col).astype(F32) - jnp.where(row // b == col // b, l, 0.0)
    while b < n:
        pair = (row // (2 * b) == col // (2 * b)) & (row // b != col // b)
        c16 = jnp.where(pair, l, 0.0).astype(BF16)
        x16 = x.astype(BF16)
        x = x - _bdot(x16, _bdot(c16, x16).astype(BF16))
        tick()
        b *= 2
    return x


def _chunk_factors(q, k, v, beta_r, g_r, row, col, reverse, tick):
    p = q.shape[0]

    def per_row(r):
        return jnp.stack([jnp.broadcast_to(r[c], (CHUNK, CHUNK)).T for c in range(p)], axis=0)

    incl = (row <= col) if reverse else (row >= col)
    strict = (row < col) if reverse else (row > col)
    beta_c = per_row(beta_r)
    g_c = per_row(g_r)
    decay = jnp.where(incl, jnp.exp(jnp.where(incl, g_c - g_r, 0.0)), 0.0)
    k16 = k.astype(BF16)
    k_beta = k * beta_c
    v_beta = v * beta_c
    l_mat = jnp.where(strict, _bdot_nt(k_beta.astype(BF16), k16) * decay, 0.0)
    tick()
    t_inv = _unit_tri_inverse(l_mat, row, col, tick)
    exp_g = jnp.exp(g_c)
    rhs = jnp.concatenate([k_beta * exp_g, v_beta], axis=2).astype(BF16)
    wu16 = _bdot(t_inv.astype(BF16), rhs).astype(BF16)
    tick()
    last = 0 if reverse else CHUNK - 1
    g_last = g_c[:, last:last + 1, :]
    kd = k * jnp.exp(g_last - g_c)
    kd_t = jnp.stack([kd[c].T for c in range(p)], axis=0).astype(BF16)
    a16 = jnp.where(incl, _bdot_nt(q.astype(BF16), k16) * decay, 0.0).astype(BF16)
    r = _bdot(jnp.concatenate([kd_t, a16], axis=1), wu16)
    tick()
    qe = q * exp_g - r[:, CHUNK:, :HEAD]
    mq = jnp.concatenate([-r[:, :CHUNK, :HEAD], qe], axis=1).astype(BF16)
    return mq, r[:, :, HEAD:], jnp.exp(g_last)


def _gdn_core_kernel(qf_ref, kf_ref, vf_ref, qb_ref, kb_ref, vb_ref, gf_ref, gb_ref, s0_ref, z_ref, nw_ref,
                     y_ref, sf_ref, s_scr, o_scr, *, n_vheads, rep, batch_heads):
    grp = pl.program_id(1)
    t = pl.program_id(2)
    n_t = pl.num_programs(2)
    tb = qf_ref.shape[0]
    cb = tb // CHUNK
    heads = s_scr.shape[0]
    row = lax.broadcasted_iota(jnp.int32, (CHUNK, CHUNK), 0)
    col = lax.broadcasted_iota(jnp.int32, (CHUNK, CHUNK), 1)

    @pl.when(t == 0)
    def _():
        s_scr[...] = s0_ref[...]

    pending = []

    def tick():
        if pending:
            pending.pop(0)()

    def recurrence(z, hs, mq, na, egl):
        state = {h: s_scr[h, z] for h in hs}
        blk = t if z == 0 else n_t - 1 - t

        def step(c):
            for j, h in enumerate(hs):
                s = state[h]
                r = _dot(mq[j * cb + c], s.astype(BF16)) + na[j * cb + c]
                o_scr[h, z, pl.ds(pl.multiple_of(blk * tb + c * CHUNK, CHUNK), CHUNK), :] = r[CHUNK:]
                state[h] = s * egl[j * cb + c] + r[:CHUNK]

        def finish():
            for h in hs:
                s_scr[h, z] = state[h]

        order = list(range(cb)) if z == 0 else list(reversed(range(cb)))
        return [functools.partial(step, c) for c in order] + [finish]

    def chunks(ref, lanes):
        return ref[:, lanes].reshape(cb, CHUNK, HEAD)

    for h0 in range(0, heads, batch_heads):
        hs = list(range(h0, h0 + batch_heads))
        for z in range(2):
            q_ref, k_ref, v_ref, gates_ref = (qf_ref, kf_ref, vf_ref, gf_ref) if z == 0 else (qb_ref, kb_ref, vb_ref, gb_ref)
            ksl = [slice((h // rep) * HEAD, (h // rep + 1) * HEAD) for h in hs]
            gate0 = z * 2 * n_vheads + grp * heads
            mq, na, egl = _chunk_factors(
                jnp.concatenate([chunks(q_ref, sl) for sl in ksl], axis=0),
                jnp.concatenate([chunks(k_ref, sl) for sl in ksl], axis=0),
                jnp.concatenate([chunks(v_ref, slice(h * HEAD, (h + 1) * HEAD)) for h in hs], axis=0),
                jnp.concatenate([gates_ref[:, pl.ds(gate0 + h, 1), :] for h in hs], axis=0),
                jnp.concatenate([gates_ref[:, pl.ds(gate0 + n_vheads + h, 1), :] for h in hs], axis=0),
                row, col, reverse=(z == 1), tick=tick)
            while pending:
                tick()
            pending.extend(recurrence(z, hs, mq, na, egl))
    while pending:
        tick()

    @pl.when(t == n_t - 1)
    def _():
        sf_ref[...] = s_scr[...]
        for hh in range(heads):
            o = o_scr[hh, 0] + o_scr[hh, 1]
            ms = jnp.mean(o * o, axis=-1, keepdims=True)
            zz = z_ref[:, hh * HEAD:(hh + 1) * HEAD].astype(F32)
            y = (o * lax.rsqrt(ms + EPS) * nw_ref[...]) * _silu(zz)
            y_ref[:, hh * HEAD:(hh + 1) * HEAD] = y.astype(y_ref.dtype)


def _gdn_core(qkv_conv, gates, s0, qkvz, norm_w, n_seq, seq_len, blk0, chunk0, key_dim, n_vheads, heads,
              batch_heads):
    tb = min(seq_len, PREP_ROWS)
    n_tb = seq_len // tb
    cb = tb // CHUNK
    nk = key_dim // HEAD
    rep = n_vheads // nk
    kw = max(1, heads // rep)
    assert (heads == 1 or heads % rep == 0) and heads % batch_heads == 0
    assert n_vheads % heads == 0 and nk % kw == 0 and (2 * nk) % heads == 0 and chunk0 % cb == 0
    conv_blocks = 2 * nk + n_vheads
    assert conv_blocks % heads == 0

    def fwd(s, g, t):
        return s * n_tb + t

    def bwd(s, g, t):
        return s * n_tb + n_tb - 1 - t

    def specs(tblk):
        return [
            pl.BlockSpec((tb, kw * HEAD), lambda s, g, t: (tblk(s, g, t), (g * heads) // (rep * kw))),
            pl.BlockSpec((tb, kw * HEAD), lambda s, g, t: (tblk(s, g, t), nk // kw + (g * heads) // (rep * kw))),
            pl.BlockSpec((tb, heads * HEAD), lambda s, g, t: (tblk(s, g, t), 2 * nk // heads + g)),
        ]

    def gate_spec(tblk):
        return pl.BlockSpec((cb, CHUNK, CHUNK), lambda s, g, t: (chunk0 // cb + tblk(s, g, t), 0, 0))

    state_spec = pl.BlockSpec((None, heads, 2, HEAD, HEAD), lambda s, g, t: (s, g, 0, 0, 0))
    return pl.pallas_call(
        functools.partial(_gdn_core_kernel, n_vheads=n_vheads, rep=rep, batch_heads=batch_heads),
        out_shape=(jax.ShapeDtypeStruct((n_seq * seq_len, n_vheads * HEAD), BF16),
                   jax.ShapeDtypeStruct((n_seq, n_vheads, 2, HEAD, HEAD), F32)),
        grid=(n_seq, n_vheads // heads, n_tb),
        in_specs=specs(fwd) + specs(bwd) + [gate_spec(fwd), gate_spec(bwd), state_spec,
                  pl.BlockSpec((seq_len, heads * HEAD), lambda s, g, t: (blk0 + s, conv_blocks // heads + g)),
                  pl.BlockSpec((1, HEAD), lambda s, g, t: (0, 0))],
        out_specs=(pl.BlockSpec((seq_len, heads * HEAD), lambda s, g, t: (s, g)), state_spec),
        scratch_shapes=[pltpu.VMEM((heads, 2, HEAD, HEAD), F32), pltpu.VMEM((heads, 2, seq_len, HEAD), F32)],
        compiler_params=_params("parallel", "parallel", "arbitrary"),
        name=f"gdn_core_{seq_len}",
    )(qkv_conv, qkv_conv, qkv_conv, qkv_conv, qkv_conv, qkv_conv, gates, gates, s0, qkvz, norm_w.reshape(1, HEAD))


def _gdn_mixer(tok, qkvz, ba, conv_w, a_log, dt_bias, norm_w):
    conv_dim = conv_w.shape[1]
    value_dim = qkvz.shape[1] - conv_dim
    key_dim = (conv_dim - value_dim) // 2
    n_vheads = value_dim // HEAD
    batch, seq, ctx = tok.batch, tok.seq, tok.ctx
    gates = _gdn_gates(ba, a_log, dt_bias)

    def run(n_seq, seq_len, blk0, chunk0, s0, heads, batch_heads):
        conv = _gdn_conv(qkvz, conv_w, n_seq, seq_len, blk0, key_dim)
        return _gdn_core(conv, gates, s0, qkvz, norm_w, n_seq, seq_len, blk0, chunk0, key_dim, n_vheads, heads,
                         batch_heads)

    zero_state = jnp.zeros((batch, n_vheads, 2, HEAD, HEAD), F32)
    y_ctx, s_ctx = run(batch, ctx, tok.n_lat // ctx, tok.n_lat // CHUNK, zero_state, CTX_HEADS_PER_STEP, CTX_HEADS_PER_STEP)
    y_lat, _ = run(batch, seq, 0, 0, s_ctx, LAT_HEADS_PER_STEP, 1)
    return y_lat, y_ctx


def kernel(x, c, ctx, c_ctx, w_ada, b_ada, norm1_w, norm2_w, na_w_qkv, na_w_o, na_q_gain, na_k_gain, na_rpb, gdn_w_in, gdn_conv_w, gdn_w_ba, gdn_a_log, gdn_dt_bias, gdn_norm_w, gdn_w_o, ffn_w1, ffn_w3, ffn_w2):
    batch, seq, d = x.shape
    n_ctx = ctx.shape[1]
    depth = w_ada.shape[0]
    tok = _Tokens(batch, seq, n_ctx)
    assert batch + 1 <= 8

    x_all = jnp.concatenate([x.reshape(batch * seq, d), ctx.reshape(batch * n_ctx, d)], axis=0)
    cond = jnp.concatenate([c, c_ctx[None, :], jnp.zeros((8 - batch - 1, d), F32)], axis=0)
    mods = _ada(cond, w_ada, b_ada)

    for i in range(depth):
        update_ctx = i < depth - 1
        n_tiles = tok.all_tiles if update_ctx else tok.lat_tiles
        mod = mods[i].reshape(8, 1, 6 * d)
        j = i // 2
        if i % 2 == 0:
            qkv = _norm_mod_matmul(tok, x_all, norm1_w[i], mod, na_w_qkv[j].astype(BF16), BF16, "na_qkv",
                                   na_gains=(na_q_gain[j], na_k_gain[j]))
            tables = _na_bias_tables(na_rpb[j], seq // GRID_W)
            mix_lat, mix_ctx = _na_attention(tok, qkv, tables, d)
            w_o = na_w_o[j]
        else:
            qkvz = _norm_mod_matmul(tok, x_all, norm1_w[i], mod, gdn_w_in[j].astype(BF16), BF16, "gdn_in")
            w_ba = jnp.concatenate([gdn_w_ba[j, 0], gdn_w_ba[j, 1]], axis=1).astype(BF16)
            w_ba = jnp.pad(w_ba, ((0, 0), (0, CHUNK - w_ba.shape[1])))
            ba = _norm_mod_matmul(tok, x_all, norm1_w[i], mod, w_ba, F32, "gdn_ba")
            mix_lat, mix_ctx = _gdn_mixer(tok, qkvz, ba, gdn_conv_w[j], gdn_a_log[j], gdn_dt_bias[j], gdn_norm_w[j])
            w_o = gdn_w_o[j]
        w_o = w_o.astype(BF16)
        x_all = _matmul_residual(tok, 0, mix_lat, w_o, x_all, mod, 2, "mixer_out")
        if update_ctx:
            x_all = _matmul_residual(tok, tok.lat_tiles, mix_ctx, w_o, x_all, mod, 2, "mixer_out_ctx")
        x_all = _ffn(tok, n_tiles, x_all, norm2_w[i], mod, ffn_w1[i].astype(BF16), ffn_w3[i].astype(BF16),
                     ffn_w2[i].astype(BF16))
    return x_all[:batch * seq].reshape(batch, seq, d)
```

```python
import functools

import jax
import jax.numpy as jnp
import numpy as np
from jax import lax
from jax.experimental import pallas as pl
from jax.experimental.pallas import tpu as pltpu

HEAD = 128
GRID_W = 64
WIN_H = 8
WIN_W = 16
ROW_GROUP = 4
NA_GROUP_UNROLL = 2
KEY_ROWS = ROW_GROUP + WIN_H
CHUNK = 128
CONV_TAPS = 4
CONV_PAD = 8
EPS = 1e-6
NEG_INF = -1e30
NA_SCALE = HEAD ** -0.5
LOG2E = 1.4426950408889634
GDN_SCALE = HEAD ** -0.5
TOKEN_TILE = 512
NORM_SLAB = 256
PREP_ROWS = 1024
CTX_HEADS_PER_STEP = 4
LAT_HEADS_PER_STEP = 2
VMEM_LIMIT = 48 * 1024 * 1024

F32 = jnp.float32
BF16 = jnp.bfloat16


def _params(*sem):
    return pltpu.CompilerParams(dimension_semantics=sem, vmem_limit_bytes=VMEM_LIMIT)


def _pick(n, cands):
    for c in cands:
        if n % c == 0:
            return c
    raise ValueError(f"no tile in {cands} divides {n}")


def _silu(x):
    return x / (1.0 + jnp.exp(-x))


def _dot(a, b):
    return jnp.dot(a, b, preferred_element_type=F32)


def _dot_nt(a, b):
    return lax.dot_general(a, b, (((1,), (1,)), ((), ())), preferred_element_type=F32)


def _dot_tn(a, b):
    return lax.dot_general(a, b, (((0,), (0,)), ((), ())), preferred_element_type=F32)


def _dot_hi(a, b):
    return jnp.dot(a, b, preferred_element_type=F32, precision=lax.Precision.HIGHEST)


def _ada_kernel(c_ref, w_ref, b_ref, o_ref):
    s = _silu(c_ref[...]).astype(BF16)
    o_ref[...] = _dot(s, w_ref[...].astype(BF16)) + b_ref[...]


def _ada(cond, w_ada, b_ada):
    depth, d, n = w_ada.shape
    tn = _pick(n, (1024, 512, 256, 128))
    return pl.pallas_call(
        _ada_kernel,
        out_shape=jax.ShapeDtypeStruct((depth, 8, n), F32),
        grid=(depth, n // tn),
        in_specs=[
            pl.BlockSpec((8, d), lambda l, j: (0, 0)),
            pl.BlockSpec((None, d, tn), lambda l, j: (l, 0, j)),
            pl.BlockSpec((None, 1, tn), lambda l, j: (l, 0, j)),
        ],
        out_specs=pl.BlockSpec((None, 8, tn), lambda l, j: (l, 0, j)),
        compiler_params=_params("parallel", "parallel"),
        name="ada_mod",
    )(cond, w_ada, b_ada.reshape(depth, 1, n))


class _Tokens:
    def __init__(self, batch, seq, ctx):
        self.batch, self.seq, self.ctx = batch, seq, ctx
        self.n_lat = batch * seq
        self.n_all = self.n_lat + batch * ctx
        self.tm = TOKEN_TILE
        assert seq % self.tm == 0 and (batch * ctx) % self.tm == 0
        self.lat_tiles = self.n_lat // self.tm
        self.all_tiles = self.n_all // self.tm
        self.tiles_per_batch = seq // self.tm

    def mod_row(self, t):
        return jnp.where(t < self.lat_tiles, t // self.tiles_per_batch, self.batch)


def _mod_spec(tok, d, chunk):
    return pl.BlockSpec((None, 1, d), lambda m, j: (tok.mod_row(m), 0, chunk))


def _norm_mod(x, nw, sh, sc):
    ms = jnp.mean(x * x, axis=-1, keepdims=True)
    return (x * lax.rsqrt(ms + EPS) * nw) * (1.0 + sc) + sh


def _nmm_kernel(x_ref, nw_ref, sh_ref, sc_ref, w_ref, o_ref, h_scr):
    @pl.when(pl.program_id(1) == 0)
    def _():
        h_scr[...] = _norm_mod(x_ref[...], nw_ref[...], sh_ref[...], sc_ref[...]).astype(BF16)

    o_ref[...] = _dot(h_scr[...], w_ref[...]).astype(o_ref.dtype)


def _nmm_na_kernel(x_ref, nw_ref, sh_ref, sc_ref, w_ref, qg_ref, kg_ref, o_ref, h_scr, *, tiles_per_kind):
    n = pl.program_id(1)

    @pl.when(n == 0)
    def _():
        h_scr[...] = _norm_mod(x_ref[...], nw_ref[...], sh_ref[...], sc_ref[...]).astype(BF16)

    kind = n // tiles_per_kind

    @pl.when(kind == 2)
    def _():
        o_ref[...] = _dot(h_scr[...], w_ref[...]).astype(o_ref.dtype)

    @pl.when(kind < 2)
    def _():
        gain = jnp.where(kind == 0, qg_ref[...] * (NA_SCALE * LOG2E), kg_ref[...])
        h = h_scr[...]
        n_slabs = o_ref.shape[1] // NORM_SLAB
        nxt = _dot(h, w_ref[:, :NORM_SLAB])
        for j in range(n_slabs):
            acc = nxt
            if j + 1 < n_slabs:
                nxt = _dot(h, w_ref[:, (j + 1) * NORM_SLAB:(j + 2) * NORM_SLAB])
            for i in range(NORM_SLAB // HEAD):
                a = acc[:, i * HEAD:(i + 1) * HEAD]
                ms = jnp.mean(a * a, axis=-1, keepdims=True)
                lo = j * NORM_SLAB + i * HEAD
                o_ref[:, lo:lo + HEAD] = (a * lax.rsqrt(ms + EPS) * gain).astype(o_ref.dtype)


def _norm_mod_matmul(tok, x_all, norm_w, mod, w, out_dtype, name, na_gains=None):
    d, n = w.shape
    tm = tok.tm
    if na_gains is None:
        tn = _pick(n, (1024, 512, 256, 128))
        kern = _nmm_kernel
        extra_specs, extra_args = [], []
    else:
        tn = _pick(d, (1024, 512, 256, 128))
        kern = functools.partial(_nmm_na_kernel, tiles_per_kind=d // tn)
        extra_specs = [pl.BlockSpec((1, HEAD), lambda m, j: (0, 0))] * 2
        extra_args = [g.reshape(1, HEAD) for g in na_gains]
    return pl.pallas_call(
        kern,
        out_shape=jax.ShapeDtypeStruct((tok.n_all, n), out_dtype),
        grid=(tok.all_tiles, n // tn),
        in_specs=[
            pl.BlockSpec((tm, d), lambda m, j: (m, 0)),
            pl.BlockSpec((1, d), lambda m, j: (0, 0)),
            _mod_spec(tok, d, 0),
            _mod_spec(tok, d, 1),
            pl.BlockSpec((d, tn), lambda m, j: (0, j)),
        ] + extra_specs,
        out_specs=pl.BlockSpec((tm, tn), lambda m, j: (m, j)),
        scratch_shapes=[pltpu.VMEM((tm, d), BF16)],
        compiler_params=_params("parallel", "arbitrary"),
        name=name,
    )(x_all, norm_w.reshape(1, d), mod, mod, w, *extra_args)


def _mm_res_kernel(a_ref, w_ref, x_ref, g_ref, o_ref):
    o_ref[...] = x_ref[...] + g_ref[...] * _dot(a_ref[...], w_ref[...])


def _matmul_residual(tok, tile0, a, w, x_all, mod, gate_chunk, name):
    k, d = w.shape
    tm = tok.tm
    tn = _pick(d, (1024, 512, 256, 128))
    n_tiles = a.shape[0] // tm
    return pl.pallas_call(
        _mm_res_kernel,
        out_shape=jax.ShapeDtypeStruct(x_all.shape, F32),
        grid=(n_tiles, d // tn),
        in_specs=[
            pl.BlockSpec((tm, k), lambda m, j: (m, 0)),
            pl.BlockSpec((k, tn), lambda m, j: (0, j)),
            pl.BlockSpec((tm, tn), lambda m, j: (tile0 + m, j)),
            pl.BlockSpec((None, 1, tn), lambda m, j: (tok.mod_row(tile0 + m), 0, gate_chunk * (d // tn) + j)),
        ],
        out_specs=pl.BlockSpec((tm, tn), lambda m, j: (tile0 + m, j)),
        input_output_aliases={2: 0},
        compiler_params=_params("parallel", "parallel"),
        name=name,
    )(a, w, x_all, mod)


def _ffn_kernel(x_ref, nw_ref, sh_ref, sc_ref, g_ref, w1_ref, w3_ref, w2_ref, o_ref, h_scr, acc_scr):
    f = pl.program_id(1)

    @pl.when(f == 0)
    def _():
        h_scr[...] = _norm_mod(x_ref[...], nw_ref[...], sh_ref[...], sc_ref[...]).astype(BF16)
        acc_scr[...] = jnp.zeros_like(acc_scr)

    h = h_scr[...]
    u = _silu(_dot(h, w1_ref[...])) * _dot(h, w3_ref[...])
    acc_scr[...] += _dot(u.astype(BF16), w2_ref[...])

    @pl.when(f == pl.num_programs(1) - 1)
    def _():
        o_ref[...] = x_ref[...] + g_ref[...] * acc_scr[...]


def _ffn(tok, n_tiles, x_all, norm_w, mod, w1, w3, w2, in_place):
    d, dff = w1.shape
    tm = tok.tm
    tf = _pick(dff, (512, 256, 128))
    return pl.pallas_call(
        _ffn_kernel,
        out_shape=jax.ShapeDtypeStruct(x_all.shape if in_place else (n_tiles * tm, d), F32),
        grid=(n_tiles, dff // tf),
        in_specs=[
            pl.BlockSpec((tm, d), lambda m, f: (m, 0)),
            pl.BlockSpec((1, d), lambda m, f: (0, 0)),
            _mod_spec(tok, d, 3),
            _mod_spec(tok, d, 4),
            _mod_spec(tok, d, 5),
            pl.BlockSpec((d, tf), lambda m, f: (0, f)),
            pl.BlockSpec((d, tf), lambda m, f: (0, f)),
            pl.BlockSpec((tf, d), lambda m, f: (f, 0)),
        ],
        out_specs=pl.BlockSpec((tm, d), lambda m, f: (m, 0)),
        scratch_shapes=[pltpu.VMEM((tm, d), BF16), pltpu.VMEM((tm, d), F32)],
        input_output_aliases={0: 0} if in_place else {},
        compiler_params=_params("parallel", "arbitrary"),
        name="ffn",
    )(x_all, norm_w.reshape(1, d), mod, mod, mod, w1, w3, w2)


def _na_bias_tables(rpb, rows):
    kh = min(WIN_H, rows)
    cols = np.arange(GRID_W)
    col_start = np.clip(cols - WIN_W // 2, 0, GRID_W - WIN_W)
    in_win = (cols[None, :] >= col_start[:, None]) & (cols[None, :] < col_start[:, None] + WIN_W)
    dc = np.clip(cols[None, :] - cols[:, None], 1 - WIN_W, WIN_W - 1) + WIN_W - 1
    off = np.zeros((3, ROW_GROUP, KEY_ROWS), np.int32)
    ok = np.zeros((3, ROW_GROUP, KEY_ROWS), bool)
    for t, (r0, start) in enumerate(((0, 0), (ROW_GROUP, 0), (rows - ROW_GROUP, rows - KEY_ROWS))):
        for i in range(ROW_GROUP):
            r = r0 + i
            rs = min(max(r - kh // 2, 0), rows - kh)
            for m in range(KEY_ROWS):
                j = start + m - rs
                ok[t, i, m] = 0 <= j < kh
                off[t, i, m] = min(max(rs - r + j + WIN_H - 1, 0), 2 * WIN_H - 2)
    heads = rpb.shape[0]
    onehot = ((dc[None] == np.arange(2 * WIN_W - 1)[:, None, None]) & in_win[None]).astype(np.float32)
    by_col = jnp.einsum("hod,dqk->hoqk", rpb.astype(F32), onehot, precision=lax.Precision.HIGHEST)
    by_col = by_col + np.where(in_win, 0.0, NEG_INF).astype(np.float32)
    masked = jnp.full((heads, GRID_W, GRID_W), NEG_INF, F32)
    blocks = [by_col[:, off[t, i, m]] if ok[t, i, m] else masked
              for t in range(3) for i in range(ROW_GROUP) for m in range(KEY_ROWS)]
    vals = jnp.stack(blocks, axis=1).reshape(heads, 3, ROW_GROUP, KEY_ROWS, GRID_W, GRID_W)
    vals = vals.transpose(0, 1, 2, 4, 3, 5)
    return vals.reshape(heads, 3, ROW_GROUP * GRID_W, KEY_ROWS * GRID_W) * LOG2E


def _softmax_pv(parts):
    m = functools.reduce(jnp.maximum, [jnp.max(s, axis=-1, keepdims=True) for s, _ in parts])
    ps = [jnp.exp2(s - m) for s, _ in parts]
    den = functools.reduce(jnp.add, [jnp.sum(p, axis=-1, keepdims=True) for p in ps])
    num = functools.reduce(jnp.add, [_dot(p.astype(BF16), v) for p, (_, v) in zip(ps, parts)])
    return num / den


def _na_kernel(q_ref, k_ref, v_ref, qc_ref, kc_ref, vc_ref, bias_ref, o_ref, oc_ref, *, rows):
    kc = kc_ref[...]
    vc = vc_ref[...]
    n_groups = rows // ROW_GROUP
    gq = ROW_GROUP * GRID_W
    gk = KEY_ROWS * GRID_W

    def group(gi, carry):
        r0 = gi * ROW_GROUP
        start = jnp.clip(r0 - WIN_H // 2, 0, rows - KEY_ROWS)
        table = jnp.where(gi == 0, 0, jnp.where(gi == n_groups - 1, 2, 1))
        q = q_ref[pl.ds(pl.multiple_of(r0 * GRID_W, gq), gq), :]
        koff = pl.multiple_of(start * GRID_W, GRID_W)
        ks = k_ref[pl.ds(koff, gk), :]
        vs = v_ref[pl.ds(koff, gk), :]
        s_loc = _dot_nt(q, ks) + bias_ref[table]
        s_ctx = _dot_nt(q, kc)
        o = _softmax_pv([(s_loc, vs), (s_ctx, vc)])
        o_ref[pl.ds(pl.multiple_of(r0 * GRID_W, gq), gq), :] = o.astype(o_ref.dtype)
        return carry

    lax.fori_loop(0, n_groups, group, 0, unroll=NA_GROUP_UNROLL)
    s_cc = _dot_nt(qc_ref[...], kc)
    oc_ref[...] = _softmax_pv([(s_cc, vc)]).astype(oc_ref.dtype)


def _na_attention(tok, qkv, bias_tables, d):
    heads = d // HEAD
    seq, ctx, batch = tok.seq, tok.ctx, tok.batch
    rows = seq // GRID_W
    assert rows % ROW_GROUP == 0 and rows >= KEY_ROWS
    ctx_blk0 = tok.n_lat // ctx
    gq, gk = ROW_GROUP * GRID_W, KEY_ROWS * GRID_W

    def lat(kind):
        return pl.BlockSpec((seq, HEAD), lambda h, b: (b, kind * heads + h))

    def cx(kind):
        return pl.BlockSpec((ctx, HEAD), lambda h, b: (ctx_blk0 + b, kind * heads + h))

    o_lat, o_ctx = pl.pallas_call(
        functools.partial(_na_kernel, rows=rows),
        out_shape=(jax.ShapeDtypeStruct((tok.n_lat, d), BF16),
                   jax.ShapeDtypeStruct((batch * ctx, d), BF16)),
        grid=(heads, batch),
        in_specs=[lat(0), lat(1), lat(2), cx(0), cx(1), cx(2),
                  pl.BlockSpec((None, 3, gq, gk), lambda h, b: (h, 0, 0, 0))],
        out_specs=(pl.BlockSpec((seq, HEAD), lambda h, b: (b, h)),
                   pl.BlockSpec((ctx, HEAD), lambda h, b: (b, h))),
        compiler_params=_params("parallel", "parallel"),
        name="na_attention",
    )(qkv, qkv, qkv, qkv, qkv, qkv, bias_tables)
    return o_lat, o_ctx


def _gdn_in_kernel(x_ref, xp_ref, xn_ref, nw_ref, sh_ref, sc_ref, w_ref, cw_ref, qkv_ref, z_ref, h_scr, acc_scr,
                   *, tok, n_key_tiles, n_conv_tiles):
    m = pl.program_id(0)
    n = pl.program_id(1)
    tm = x_ref.shape[0]
    tn = w_ref.shape[1]

    @pl.when(n == 0)
    def _():
        nw, sh, sc = nw_ref[...], sh_ref[...], sc_ref[...]
        h_scr[pl.ds(0, CONV_PAD), :] = _norm_mod(xp_ref[...], nw, sh, sc).astype(BF16)
        h_scr[pl.ds(CONV_PAD, tm), :] = _norm_mod(x_ref[...], nw, sh, sc).astype(BF16)
        h_scr[pl.ds(CONV_PAD + tm, CONV_PAD), :] = _norm_mod(xn_ref[...], nw, sh, sc).astype(BF16)

    @pl.when(n >= n_conv_tiles)
    def _():
        z_ref[...] = _dot(h_scr[pl.ds(CONV_PAD, tm), :], w_ref[...]).astype(z_ref.dtype)

    def conv_tile(normalise):
        seq_len = jnp.where(m < tok.lat_tiles, tok.seq, tok.ctx)
        pos = (m * tm + lax.broadcasted_iota(jnp.int32, (tm, 1), 0)) & (seq_len - 1)
        ok = [pos >= 1, None, pos <= seq_len - 2, pos <= seq_len - 3]
        scale = jnp.where(n < n_key_tiles, GDN_SCALE, 1.0)
        h = h_scr[...]
        n_slabs = tn // NORM_SLAB
        acc_scr[0] = _dot(h, w_ref[:, :NORM_SLAB])
        for j in range(n_slabs):
            cols = slice(j * NORM_SLAB, (j + 1) * NORM_SLAB)
            if j + 1 < n_slabs:
                acc_scr[(j + 1) % 2] = _dot(h, w_ref[:, (j + 1) * NORM_SLAB:(j + 2) * NORM_SLAB])
            cw = cw_ref[:, cols]
            y = None
            for tap in range(CONV_TAPS):
                term = acc_scr[j % 2, pl.ds(CONV_PAD - (CONV_TAPS - 1) // 2 + tap, tm), :] * cw[tap:tap + 1, :]
                if ok[tap] is not None:
                    term = jnp.where(ok[tap], term, 0.0)
                y = term if y is None else y + term
            y = _silu(y)
            for i in range(NORM_SLAB // HEAD):
                yh = y[:, i * HEAD:(i + 1) * HEAD]
                if normalise:
                    yh = yh * (lax.rsqrt(jnp.sum(yh * yh, axis=-1, keepdims=True) + EPS) * scale)
                lo = j * NORM_SLAB + i * HEAD
                qkv_ref[:, lo:lo + HEAD] = yh

    pl.when(n < 2 * n_key_tiles)(functools.partial(conv_tile, True))
    pl.when((n >= 2 * n_key_tiles) & (n < n_conv_tiles))(functools.partial(conv_tile, False))


def _gdn_in(tok, x_all, norm_w, mod, w, conv_w, key_dim):
    d, n = w.shape
    conv_dim = conv_w.shape[1]
    tm = tok.tm
    tn = _pick(np.gcd(key_dim, n - conv_dim), (1024, 512, 256))
    n_conv_tiles = conv_dim // tn
    halo_blocks = tm // CONV_PAD
    last_halo = tok.n_all // CONV_PAD - 1
    assert tok.seq & (tok.seq - 1) == 0 and tok.ctx & (tok.ctx - 1) == 0
    kern = functools.partial(_gdn_in_kernel, tok=tok, n_key_tiles=key_dim // tn, n_conv_tiles=n_conv_tiles)
    return pl.pallas_call(
        kern,
        out_shape=(jax.ShapeDtypeStruct((tok.n_all, conv_dim), F32),
                   jax.ShapeDtypeStruct((tok.n_all, n - conv_dim), BF16)),
        grid=(tok.all_tiles, n // tn),
        in_specs=[
            pl.BlockSpec((tm, d), lambda m, j: (m, 0)),
            pl.BlockSpec((CONV_PAD, d), lambda m, j: (jnp.maximum(m * halo_blocks - 1, 0), 0)),
            pl.BlockSpec((CONV_PAD, d), lambda m, j: (jnp.minimum((m + 1) * halo_blocks, last_halo), 0)),
            pl.BlockSpec((1, d), lambda m, j: (0, 0)),
            _mod_spec(tok, d, 0),
            _mod_spec(tok, d, 1),
            pl.BlockSpec((d, tn), lambda m, j: (0, j)),
            pl.BlockSpec((CONV_TAPS, tn), lambda m, j: (0, jnp.minimum(j, n_conv_tiles - 1))),
        ],
        out_specs=(pl.BlockSpec((tm, tn), lambda m, j: (m, jnp.minimum(j, n_conv_tiles - 1))),
                   pl.BlockSpec((tm, tn), lambda m, j: (m, jnp.maximum(j - n_conv_tiles, 0)))),
        scratch_shapes=[pltpu.VMEM((tm + 2 * CONV_PAD, d), BF16),
                        pltpu.VMEM((2, tm + 2 * CONV_PAD, NORM_SLAB), F32)],
        compiler_params=_params("parallel", "arbitrary"),
        name="gdn_in",
    )(x_all, x_all, x_all, norm_w.reshape(1, d), mod, mod, w, conv_w)


def _gates_kernel(ba_ref, alog_ref, dt_ref, o_ref, *, n_chunks, n_vheads):
    row = lax.broadcasted_iota(jnp.int32, (CHUNK, CHUNK), 0)
    col = lax.broadcasted_iota(jnp.int32, (CHUNK, CHUNK), 1)
    prefix = (row <= col).astype(F32)
    suffix = (row >= col).astype(F32)
    is_decay = (row % (2 * n_vheads)) >= n_vheads
    neg_a = -jnp.exp(alog_ref[...])
    for c in range(n_chunks):
        bt = ba_ref[pl.ds(c * CHUNK, CHUNK), :].T
        beta = 1.0 / (1.0 + jnp.exp(-bt))
        x = bt + dt_ref[...]
        g = neg_a * (jnp.maximum(x, 0.0) + jnp.log1p(jnp.exp(-jnp.abs(x))))
        cum = jnp.where(row < 2 * n_vheads, _dot_hi(g, prefix), _dot_hi(g, suffix))
        o_ref[c] = jnp.where(is_decay, cum, beta)


def _gdn_gates(ba, a_log, dt_bias):
    n_tok, n_col = ba.shape
    n_vheads = a_log.shape[1]
    assert n_col == CHUNK and 4 * n_vheads <= n_col
    zeros = jnp.zeros_like(a_log)
    pad = ((0, n_col - 4 * n_vheads), (0, 0))
    alog_col = jnp.pad(jnp.concatenate([zeros, a_log], axis=1).reshape(4 * n_vheads, 1), pad)
    dt_col = jnp.pad(jnp.concatenate([zeros, dt_bias], axis=1).reshape(4 * n_vheads, 1), pad)
    per_step = 4
    assert (n_tok // CHUNK) % per_step == 0
    return pl.pallas_call(
        functools.partial(_gates_kernel, n_chunks=per_step, n_vheads=n_vheads),
        out_shape=jax.ShapeDtypeStruct((n_tok // CHUNK, n_col, CHUNK), F32),
        grid=(n_tok // (CHUNK * per_step),),
        in_specs=[pl.BlockSpec((CHUNK * per_step, n_col), lambda i: (i, 0)),
                  pl.BlockSpec((n_col, 1), lambda i: (0, 0)),
                  pl.BlockSpec((n_col, 1), lambda i: (0, 0))],
        out_specs=pl.BlockSpec((per_step, n_col, CHUNK), lambda i: (i, 0, 0)),
        compiler_params=_params("parallel"),
        name="gdn_gates",
    )(ba, alog_col, dt_col)


def _bdot(a, b):
    return jnp.einsum("pmk,pkn->pmn", a, b, preferred_element_type=F32)


def _bdot_nt(a, b):
    return jnp.einsum("pmk,pnk->pmn", a, b, preferred_element_type=F32)


def _unit_tri_inverse(l, row, col, tick):
    n = l.shape[-1]
    b = 2
    x = (row == col).astype(F32) - jnp.where(row // b == col // b, l, 0.0)
    while b < n:
        pair = (row // (2 * b) == col // (2 * b)) & (row // b != col // b)
        c16 = jnp.where(pair, l, 0.0).astype(BF16)
        x16 = x.astype(BF16)
        x = x - _bdot(x16, _bdot(c16, x16).astype(BF16))
        tick()
        b *= 2
    return x


def _chunk_factors(q, k, v, beta_r, g_r, row, col, reverse, tick):
    p = q.shape[0]

    def per_row(r):
        return jnp.stack([jnp.broadcast_to(r[c], (CHUNK, CHUNK)).T for c in range(p)], axis=0)

    incl = (row <= col) if reverse else (row >= col)
    strict = (row < col) if reverse else (row > col)
    beta_c = per_row(beta_r)
    g_c = per_row(g_r)
    decay = jnp.where(incl, jnp.exp(jnp.where(incl, g_c - g_r, 0.0)), 0.0)
    k16 = k.astype(BF16)
    k_beta = k * beta_c
    v_beta = v * beta_c
    l_mat = jnp.where(strict, _bdot_nt(k_beta.astype(BF16), k16) * decay, 0.0)
    tick()
    t_inv = _unit_tri_inverse(l_mat, row, col, tick)
    exp_g = jnp.exp(g_c)
    rhs = jnp.concatenate([k_beta * exp_g, v_beta], axis=2).astype(BF16)
    wu16 = _bdot(t_inv.astype(BF16), rhs).astype(BF16)
    tick()
    last = 0 if reverse else CHUNK - 1
    g_last = g_c[:, last:last + 1, :]
    kd = k * jnp.exp(g_last - g_c)
    kd_t = jnp.stack([kd[c].T for c in range(p)], axis=0).astype(BF16)
    a16 = jnp.where(incl, _bdot_nt(q.astype(BF16), k16) * decay, 0.0).astype(BF16)
    r = _bdot(jnp.concatenate([kd_t, a16], axis=1), wu16)
    tick()
    qe = q * exp_g - r[:, CHUNK:, :HEAD]
    mq = jnp.concatenate([-r[:, :CHUNK, :HEAD], qe], axis=1).astype(BF16)
    return mq, r[:, :, HEAD:], jnp.exp(g_last)


def _gdn_core_kernel(qf_ref, kf_ref, vf_ref, qb_ref, kb_ref, vb_ref, gf_ref, gb_ref, s0_ref, z_ref, nw_ref,
                     y_ref, sf_ref, s_scr, o_scr, *, n_vheads, rep, batch_heads):
    grp = pl.program_id(1)
    t = pl.program_id(2)
    n_t = pl.num_programs(2)
    tb = qf_ref.shape[0]
    cb = tb // CHUNK
    heads = s_scr.shape[0]
    row = lax.broadcasted_iota(jnp.int32, (CHUNK, CHUNK), 0)
    col = lax.broadcasted_iota(jnp.int32, (CHUNK, CHUNK), 1)

    @pl.when(t == 0)
    def _():
        s_scr[...] = s0_ref[...]

    pending = []

    def tick():
        if pending:
            pending.pop(0)()

    def recurrence(z, hs, mq, na, egl):
        state = {h: s_scr[h, z] for h in hs}
        blk = t if z == 0 else n_t - 1 - t

        def step(c):
            for j, h in enumerate(hs):
                s = state[h]
                r = _dot(mq[j * cb + c], s.astype(BF16)) + na[j * cb + c]
                o_scr[h, z, pl.ds(pl.multiple_of(blk * tb + c * CHUNK, CHUNK), CHUNK), :] = r[CHUNK:]
                state[h] = s * egl[j * cb + c] + r[:CHUNK]

        def finish():
            for h in hs:
                s_scr[h, z] = state[h]

        order = list(range(cb)) if z == 0 else list(reversed(range(cb)))
        return [functools.partial(step, c) for c in order] + [finish]

    def chunks(ref, lanes):
        return ref[:, lanes].reshape(cb, CHUNK, HEAD)

    for h0 in range(0, heads, batch_heads):
        hs = list(range(h0, h0 + batch_heads))
        for z in range(2):
            q_ref, k_ref, v_ref, gates_ref = (qf_ref, kf_ref, vf_ref, gf_ref) if z == 0 else (qb_ref, kb_ref, vb_ref, gb_ref)
            ksl = [slice((h // rep) * HEAD, (h // rep + 1) * HEAD) for h in hs]
            gate0 = z * 2 * n_vheads + grp * heads
            mq, na, egl = _chunk_factors(
                jnp.concatenate([chunks(q_ref, sl) for sl in ksl], axis=0),
                jnp.concatenate([chunks(k_ref, sl) for sl in ksl], axis=0),
                jnp.concatenate([chunks(v_ref, slice(h * HEAD, (h + 1) * HEAD)) for h in hs], axis=0),
                jnp.concatenate([gates_ref[:, pl.ds(gate0 + h, 1), :] for h in hs], axis=0),
                jnp.concatenate([gates_ref[:, pl.ds(gate0 + n_vheads + h, 1), :] for h in hs], axis=0),
                row, col, reverse=(z == 1), tick=tick)
            while pending:
                tick()
            pending.extend(recurrence(z, hs, mq, na, egl))
    while pending:
        tick()

    @pl.when(t == n_t - 1)
    def _():
        sf_ref[...] = s_scr[...]
        for hh in range(heads):
            o = o_scr[hh, 0] + o_scr[hh, 1]
            ms = jnp.mean(o * o, axis=-1, keepdims=True)
            zz = z_ref[:, hh * HEAD:(hh + 1) * HEAD].astype(F32)
            y = (o * lax.rsqrt(ms + EPS) * nw_ref[...]) * _silu(zz)
            y_ref[:, hh * HEAD:(hh + 1) * HEAD] = y.astype(y_ref.dtype)


def _gdn_core(qkv_conv, gates, s0, z_gate, norm_w, n_seq, seq_len, blk0, chunk0, key_dim, n_vheads, heads,
              batch_heads):
    tb = min(seq_len, PREP_ROWS)
    n_tb = seq_len // tb
    cb = tb // CHUNK
    nk = key_dim // HEAD
    rep = n_vheads // nk
    kw = max(1, heads // rep)
    assert (heads == 1 or heads % rep == 0) and heads % batch_heads == 0
    assert n_vheads % heads == 0 and nk % kw == 0 and (2 * nk) % heads == 0 and chunk0 == blk0 * n_tb * cb

    def fwd(s, g, t):
        return (blk0 + s) * n_tb + t

    def bwd(s, g, t):
        return (blk0 + s) * n_tb + n_tb - 1 - t

    def specs(tblk):
        return [
            pl.BlockSpec((tb, kw * HEAD), lambda s, g, t: (tblk(s, g, t), (g * heads) // (rep * kw))),
            pl.BlockSpec((tb, kw * HEAD), lambda s, g, t: (tblk(s, g, t), nk // kw + (g * heads) // (rep * kw))),
            pl.BlockSpec((tb, heads * HEAD), lambda s, g, t: (tblk(s, g, t), 2 * nk // heads + g)),
        ]

    def gate_spec(tblk):
        return pl.BlockSpec((cb, CHUNK, CHUNK), lambda s, g, t: (tblk(s, g, t), 0, 0))

    state_spec = pl.BlockSpec((None, heads, 2, HEAD, HEAD), lambda s, g, t: (s, g, 0, 0, 0))
    return pl.pallas_call(
        functools.partial(_gdn_core_kernel, n_vheads=n_vheads, rep=rep, batch_heads=batch_heads),
        out_shape=(jax.ShapeDtypeStruct((n_seq * seq_len, n_vheads * HEAD), BF16),
                   jax.ShapeDtypeStruct((n_seq, n_vheads, 2, HEAD, HEAD), F32)),
        grid=(n_seq, n_vheads // heads, n_tb),
        in_specs=specs(fwd) + specs(bwd) + [gate_spec(fwd), gate_spec(bwd), state_spec,
                  pl.BlockSpec((seq_len, heads * HEAD), lambda s, g, t: (blk0 + s, g)),
                  pl.BlockSpec((1, HEAD), lambda s, g, t: (0, 0))],
        out_specs=(pl.BlockSpec((seq_len, heads * HEAD), lambda s, g, t: (s, g)), state_spec),
        scratch_shapes=[pltpu.VMEM((heads, 2, HEAD, HEAD), F32), pltpu.VMEM((heads, 2, seq_len, HEAD), F32)],
        compiler_params=_params("parallel", "parallel", "arbitrary"),
        name=f"gdn_core_{seq_len}",
    )(qkv_conv, qkv_conv, qkv_conv, qkv_conv, qkv_conv, qkv_conv, gates, gates, s0, z_gate, norm_w.reshape(1, HEAD))


def _gdn_mixer(tok, qkv_conv, z_gate, ba, key_dim, a_log, dt_bias, norm_w):
    n_vheads = z_gate.shape[1] // HEAD
    batch, seq, ctx = tok.batch, tok.seq, tok.ctx
    gates = _gdn_gates(ba, a_log, dt_bias)

    def run(n_seq, seq_len, blk0, s0, heads, batch_heads):
        return _gdn_core(qkv_conv, gates, s0, z_gate, norm_w, n_seq, seq_len, blk0, blk0 * seq_len // CHUNK,
                         key_dim, n_vheads, heads, batch_heads)

    zero_state = jnp.zeros((batch, n_vheads, 2, HEAD, HEAD), F32)
    y_ctx, s_ctx = run(batch, ctx, tok.n_lat // ctx, zero_state, CTX_HEADS_PER_STEP, CTX_HEADS_PER_STEP)
    y_lat, _ = run(batch, seq, 0, s_ctx, LAT_HEADS_PER_STEP, 1)
    return y_lat, y_ctx


def kernel(x, c, ctx, c_ctx, w_ada, b_ada, norm1_w, norm2_w, na_w_qkv, na_w_o, na_q_gain, na_k_gain, na_rpb, gdn_w_in, gdn_conv_w, gdn_w_ba, gdn_a_log, gdn_dt_bias, gdn_norm_w, gdn_w_o, ffn_w1, ffn_w3, ffn_w2):
    batch, seq, d = x.shape
    n_ctx = ctx.shape[1]
    depth = w_ada.shape[0]
    tok = _Tokens(batch, seq, n_ctx)
    assert batch + 1 <= 8

    x_all = jnp.concatenate([x.reshape(batch * seq, d), ctx.reshape(batch * n_ctx, d)], axis=0)
    cond = jnp.concatenate([c, c_ctx[None, :], jnp.zeros((8 - batch - 1, d), F32)], axis=0)
    mods = _ada(cond, w_ada, b_ada)

    for i in range(depth):
        update_ctx = i < depth - 1
        n_tiles = tok.all_tiles if update_ctx else tok.lat_tiles
        mod = mods[i].reshape(8, 1, 6 * d)
        j = i // 2
        if i % 2 == 0:
            qkv = _norm_mod_matmul(tok, x_all, norm1_w[i], mod, na_w_qkv[j].astype(BF16), BF16, "na_qkv",
                                   na_gains=(na_q_gain[j], na_k_gain[j]))
            tables = _na_bias_tables(na_rpb[j], seq // GRID_W)
            mix_lat, mix_ctx = _na_attention(tok, qkv, tables, d)
            w_o = na_w_o[j]
        else:
            conv_dim = gdn_conv_w.shape[2]
            key_dim = (2 * conv_dim - gdn_w_in.shape[2]) // 2
            qkv_conv, z_gate = _gdn_in(tok, x_all, norm1_w[i], mod, gdn_w_in[j].astype(BF16), gdn_conv_w[j], key_dim)
            w_ba = jnp.concatenate([gdn_w_ba[j, 0], gdn_w_ba[j, 1]], axis=1).astype(BF16)
            w_ba = jnp.pad(w_ba, ((0, 0), (0, CHUNK - w_ba.shape[1])))
            ba = _norm_mod_matmul(tok, x_all, norm1_w[i], mod, w_ba, F32, "gdn_ba")
            mix_lat, mix_ctx = _gdn_mixer(tok, qkv_conv, z_gate, ba, key_dim, gdn_a_log[j], gdn_dt_bias[j],
                                          gdn_norm_w[j])
            w_o = gdn_w_o[j]
        w_o = w_o.astype(BF16)
        x_all = _matmul_residual(tok, 0, mix_lat, w_o, x_all, mod, 2, "mixer_out")
        if update_ctx:
            x_all = _matmul_residual(tok, tok.lat_tiles, mix_ctx, w_o, x_all, mod, 2, "mixer_out_ctx")
        x_all = _ffn(tok, n_tiles, x_all, norm2_w[i], mod, ffn_w1[i].astype(BF16), ffn_w3[i].astype(BF16),
                     ffn_w2[i].astype(BF16), in_place=update_ctx)
    return x_all.reshape(batch, seq, d)
```

```python
import functools

import jax
import jax.numpy as jnp
import numpy as np
from jax import lax
from jax.experimental import pallas as pl
from jax.experimental.pallas import tpu as pltpu

HEAD = 128
GRID_W = 64
WIN_H = 8
WIN_W = 16
ROW_GROUP = 4
NA_GROUP_UNROLL = 2
KEY_ROWS = ROW_GROUP + WIN_H
CHUNK = 128
CONV_TAPS = 4
CONV_PAD = 8
EPS = 1e-6
NEG_INF = -1e30
NA_SCALE = HEAD ** -0.5
LOG2E = 1.4426950408889634
GDN_SCALE = HEAD ** -0.5
TOKEN_TILE = 512
NORM_SLAB = 256
PREP_ROWS = 1024
CTX_HEADS_PER_STEP = 8
LAT_HEADS_PER_STEP = 2
VMEM_LIMIT = 48 * 1024 * 1024

F32 = jnp.float32
BF16 = jnp.bfloat16


def _params(*sem):
    return pltpu.CompilerParams(dimension_semantics=sem, vmem_limit_bytes=VMEM_LIMIT)


def _pick(n, cands):
    for c in cands:
        if n % c == 0:
            return c
    raise ValueError(f"no tile in {cands} divides {n}")


def _silu(x):
    return x / (1.0 + jnp.exp(-x))


def _dot(a, b):
    return jnp.dot(a, b, preferred_element_type=F32)


def _dot_nt(a, b):
    return lax.dot_general(a, b, (((1,), (1,)), ((), ())), preferred_element_type=F32)


def _dot_tn(a, b):
    return lax.dot_general(a, b, (((0,), (0,)), ((), ())), preferred_element_type=F32)


def _dot_hi(a, b):
    return jnp.dot(a, b, preferred_element_type=F32, precision=lax.Precision.HIGHEST)


def _ada_kernel(c_ref, w_ref, b_ref, o_ref):
    s = _silu(c_ref[...]).astype(BF16)
    o_ref[...] = _dot(s, w_ref[...].astype(BF16)) + b_ref[...]


def _ada(cond, w_ada, b_ada):
    depth, d, n = w_ada.shape
    tn = _pick(n, (1024, 512, 256, 128))
    return pl.pallas_call(
        _ada_kernel,
        out_shape=jax.ShapeDtypeStruct((depth, 8, n), F32),
        grid=(depth, n // tn),
        in_specs=[
            pl.BlockSpec((8, d), lambda l, j: (0, 0)),
            pl.BlockSpec((None, d, tn), lambda l, j: (l, 0, j)),
            pl.BlockSpec((None, 1, tn), lambda l, j: (l, 0, j)),
        ],
        out_specs=pl.BlockSpec((None, 8, tn), lambda l, j: (l, 0, j)),
        compiler_params=_params("parallel", "parallel"),
        name="ada_mod",
    )(cond, w_ada, b_ada.reshape(depth, 1, n))


class _Tokens:
    def __init__(self, batch, seq, ctx):
        self.batch, self.seq, self.ctx = batch, seq, ctx
        self.n_lat = batch * seq
        self.n_all = self.n_lat + batch * ctx
        self.tm = TOKEN_TILE
        assert seq % self.tm == 0 and (batch * ctx) % self.tm == 0
        self.lat_tiles = self.n_lat // self.tm
        self.all_tiles = self.n_all // self.tm
        self.tiles_per_batch = seq // self.tm

    def mod_row(self, t):
        return jnp.where(t < self.lat_tiles, t // self.tiles_per_batch, self.batch)


def _mod_spec(tok, d, chunk):
    return pl.BlockSpec((None, 1, d), lambda m, j: (tok.mod_row(m), 0, chunk))


def _norm_mod(x, nw, sh, sc):
    ms = jnp.mean(x * x, axis=-1, keepdims=True)
    return (x * lax.rsqrt(ms + EPS) * nw) * (1.0 + sc) + sh


def _nmm_kernel(x_ref, nw_ref, sh_ref, sc_ref, w_ref, o_ref, h_scr):
    @pl.when(pl.program_id(1) == 0)
    def _():
        h_scr[...] = _norm_mod(x_ref[...], nw_ref[...], sh_ref[...], sc_ref[...]).astype(BF16)

    o_ref[...] = _dot(h_scr[...], w_ref[...]).astype(o_ref.dtype)


def _nmm_na_kernel(x_ref, nw_ref, sh_ref, sc_ref, w_ref, qg_ref, kg_ref, o_ref, h_scr, *, tiles_per_kind):
    n = pl.program_id(1)

    @pl.when(n == 0)
    def _():
        h_scr[...] = _norm_mod(x_ref[...], nw_ref[...], sh_ref[...], sc_ref[...]).astype(BF16)

    kind = n // tiles_per_kind

    @pl.when(kind == 2)
    def _():
        o_ref[...] = _dot(h_scr[...], w_ref[...]).astype(o_ref.dtype)

    @pl.when(kind < 2)
    def _():
        gain = jnp.where(kind == 0, qg_ref[...] * (NA_SCALE * LOG2E), kg_ref[...])
        h = h_scr[...]
        n_slabs = o_ref.shape[1] // NORM_SLAB
        nxt = _dot(h, w_ref[:, :NORM_SLAB])
        for j in range(n_slabs):
            acc = nxt
            if j + 1 < n_slabs:
                nxt = _dot(h, w_ref[:, (j + 1) * NORM_SLAB:(j + 2) * NORM_SLAB])
            for i in range(NORM_SLAB // HEAD):
                a = acc[:, i * HEAD:(i + 1) * HEAD]
                ms = jnp.mean(a * a, axis=-1, keepdims=True)
                lo = j * NORM_SLAB + i * HEAD
                o_ref[:, lo:lo + HEAD] = (a * lax.rsqrt(ms + EPS) * gain).astype(o_ref.dtype)


def _norm_mod_matmul(tok, x_all, norm_w, mod, w, out_dtype, name, na_gains=None):
    d, n = w.shape
    tm = tok.tm
    if na_gains is None:
        tn = _pick(n, (1024, 512, 256, 128))
        kern = _nmm_kernel
        extra_specs, extra_args = [], []
    else:
        tn = _pick(d, (1024, 512, 256, 128))
        kern = functools.partial(_nmm_na_kernel, tiles_per_kind=d // tn)
        extra_specs = [pl.BlockSpec((1, HEAD), lambda m, j: (0, 0))] * 2
        extra_args = [g.reshape(1, HEAD) for g in na_gains]
    return pl.pallas_call(
        kern,
        out_shape=jax.ShapeDtypeStruct((tok.n_all, n), out_dtype),
        grid=(tok.all_tiles, n // tn),
        in_specs=[
            pl.BlockSpec((tm, d), lambda m, j: (m, 0)),
            pl.BlockSpec((1, d), lambda m, j: (0, 0)),
            _mod_spec(tok, d, 0),
            _mod_spec(tok, d, 1),
            pl.BlockSpec((d, tn), lambda m, j: (0, j)),
        ] + extra_specs,
        out_specs=pl.BlockSpec((tm, tn), lambda m, j: (m, j)),
        scratch_shapes=[pltpu.VMEM((tm, d), BF16)],
        compiler_params=_params("parallel", "arbitrary"),
        name=name,
    )(x_all, norm_w.reshape(1, d), mod, mod, w, *extra_args)


def _mm_res_kernel(a_ref, w_ref, x_ref, g_ref, o_ref):
    o_ref[...] = x_ref[...] + g_ref[...] * _dot(a_ref[...], w_ref[...])


def _matmul_residual(tok, tile0, a, w, x_all, mod, gate_chunk, name):
    k, d = w.shape
    tm = tok.tm
    tn = _pick(d, (1024, 512, 256, 128))
    n_tiles = a.shape[0] // tm
    return pl.pallas_call(
        _mm_res_kernel,
        out_shape=jax.ShapeDtypeStruct(x_all.shape, F32),
        grid=(n_tiles, d // tn),
        in_specs=[
            pl.BlockSpec((tm, k), lambda m, j: (m, 0)),
            pl.BlockSpec((k, tn), lambda m, j: (0, j)),
            pl.BlockSpec((tm, tn), lambda m, j: (tile0 + m, j)),
            pl.BlockSpec((None, 1, tn), lambda m, j: (tok.mod_row(tile0 + m), 0, gate_chunk * (d // tn) + j)),
        ],
        out_specs=pl.BlockSpec((tm, tn), lambda m, j: (tile0 + m, j)),
        input_output_aliases={2: 0},
        compiler_params=_params("parallel", "parallel"),
        name=name,
    )(a, w, x_all, mod)


def _ffn_kernel(x_ref, nw_ref, sh_ref, sc_ref, g_ref, w1_ref, w3_ref, w2_ref, o_ref, h_scr, acc_scr):
    f = pl.program_id(1)

    @pl.when(f == 0)
    def _():
        h_scr[...] = _norm_mod(x_ref[...], nw_ref[...], sh_ref[...], sc_ref[...]).astype(BF16)
        acc_scr[...] = jnp.zeros_like(acc_scr)

    h = h_scr[...]
    u = _silu(_dot(h, w1_ref[...])) * _dot(h, w3_ref[...])
    acc_scr[...] += _dot(u.astype(BF16), w2_ref[...])

    @pl.when(f == pl.num_programs(1) - 1)
    def _():
        o_ref[...] = x_ref[...] + g_ref[...] * acc_scr[...]


def _ffn(tok, n_tiles, x_all, norm_w, mod, w1, w3, w2, in_place):
    d, dff = w1.shape
    tm = tok.tm
    tf = _pick(dff, (512, 256, 128))
    return pl.pallas_call(
        _ffn_kernel,
        out_shape=jax.ShapeDtypeStruct(x_all.shape if in_place else (n_tiles * tm, d), F32),
        grid=(n_tiles, dff // tf),
        in_specs=[
            pl.BlockSpec((tm, d), lambda m, f: (m, 0)),
            pl.BlockSpec((1, d), lambda m, f: (0, 0)),
            _mod_spec(tok, d, 3),
            _mod_spec(tok, d, 4),
            _mod_spec(tok, d, 5),
            pl.BlockSpec((d, tf), lambda m, f: (0, f)),
            pl.BlockSpec((d, tf), lambda m, f: (0, f)),
            pl.BlockSpec((tf, d), lambda m, f: (f, 0)),
        ],
        out_specs=pl.BlockSpec((tm, d), lambda m, f: (m, 0)),
        scratch_shapes=[pltpu.VMEM((tm, d), BF16), pltpu.VMEM((tm, d), F32)],
        input_output_aliases={0: 0} if in_place else {},
        compiler_params=_params("parallel", "arbitrary"),
        name="ffn",
    )(x_all, norm_w.reshape(1, d), mod, mod, mod, w1, w3, w2)


def _na_bias_tables(rpb, rows):
    kh = min(WIN_H, rows)
    cols = np.arange(GRID_W)
    col_start = np.clip(cols - WIN_W // 2, 0, GRID_W - WIN_W)
    in_win = (cols[None, :] >= col_start[:, None]) & (cols[None, :] < col_start[:, None] + WIN_W)
    dc = np.clip(cols[None, :] - cols[:, None], 1 - WIN_W, WIN_W - 1) + WIN_W - 1
    off = np.zeros((3, ROW_GROUP, KEY_ROWS), np.int32)
    ok = np.zeros((3, ROW_GROUP, KEY_ROWS), bool)
    for t, (r0, start) in enumerate(((0, 0), (ROW_GROUP, 0), (rows - ROW_GROUP, rows - KEY_ROWS))):
        for i in range(ROW_GROUP):
            r = r0 + i
            rs = min(max(r - kh // 2, 0), rows - kh)
            for m in range(KEY_ROWS):
                j = start + m - rs
                ok[t, i, m] = 0 <= j < kh
                off[t, i, m] = min(max(rs - r + j + WIN_H - 1, 0), 2 * WIN_H - 2)
    heads = rpb.shape[0]
    onehot = ((dc[None] == np.arange(2 * WIN_W - 1)[:, None, None]) & in_win[None]).astype(np.float32)
    by_col = jnp.einsum("hod,dqk->hoqk", rpb.astype(F32), onehot, precision=lax.Precision.HIGHEST)
    by_col = (by_col + np.where(in_win, 0.0, NEG_INF).astype(np.float32)) * LOG2E
    masked = jnp.full((heads, GRID_W, GRID_W), NEG_INF * LOG2E, F32)
    per_query_row = [
        jnp.stack([by_col[:, off[t, i, m]] if ok[t, i, m] else masked for m in range(KEY_ROWS)], axis=2)
        for t in range(3) for i in range(ROW_GROUP)]
    return jnp.stack(per_query_row, axis=1).reshape(heads, 3, ROW_GROUP * GRID_W, KEY_ROWS * GRID_W)


def _softmax_pv(parts):
    m = functools.reduce(jnp.maximum, [jnp.max(s, axis=-1, keepdims=True) for s, _ in parts])
    ps = [jnp.exp2(s - m) for s, _ in parts]
    den = functools.reduce(jnp.add, [jnp.sum(p, axis=-1, keepdims=True) for p in ps])
    num = functools.reduce(jnp.add, [_dot(p.astype(BF16), v) for p, (_, v) in zip(ps, parts)])
    return num / den


def _na_kernel(q_ref, k_ref, v_ref, qc_ref, kc_ref, vc_ref, bias_ref, o_ref, oc_ref, *, rows):
    kc = kc_ref[...]
    vc = vc_ref[...]
    n_groups = rows // ROW_GROUP
    gq = ROW_GROUP * GRID_W
    gk = KEY_ROWS * GRID_W

    def group(gi, carry):
        r0 = gi * ROW_GROUP
        start = jnp.clip(r0 - WIN_H // 2, 0, rows - KEY_ROWS)
        table = jnp.where(gi == 0, 0, jnp.where(gi == n_groups - 1, 2, 1))
        q = q_ref[pl.ds(pl.multiple_of(r0 * GRID_W, gq), gq), :]
        koff = pl.multiple_of(start * GRID_W, GRID_W)
        ks = k_ref[pl.ds(koff, gk), :]
        vs = v_ref[pl.ds(koff, gk), :]
        s_loc = _dot_nt(q, ks) + bias_ref[table]
        s_ctx = _dot_nt(q, kc)
        o = _softmax_pv([(s_loc, vs), (s_ctx, vc)])
        o_ref[pl.ds(pl.multiple_of(r0 * GRID_W, gq), gq), :] = o.astype(o_ref.dtype)
        return carry

    lax.fori_loop(0, n_groups, group, 0, unroll=NA_GROUP_UNROLL)
    s_cc = _dot_nt(qc_ref[...], kc)
    oc_ref[...] = _softmax_pv([(s_cc, vc)]).astype(oc_ref.dtype)


def _na_attention(tok, qkv, bias_tables, d):
    heads = d // HEAD
    seq, ctx, batch = tok.seq, tok.ctx, tok.batch
    rows = seq // GRID_W
    assert rows % ROW_GROUP == 0 and rows >= KEY_ROWS
    ctx_blk0 = tok.n_lat // ctx
    gq, gk = ROW_GROUP * GRID_W, KEY_ROWS * GRID_W

    def lat(kind):
        return pl.BlockSpec((seq, HEAD), lambda h, b: (b, kind * heads + h))

    def cx(kind):
        return pl.BlockSpec((ctx, HEAD), lambda h, b: (ctx_blk0 + b, kind * heads + h))

    o_lat, o_ctx = pl.pallas_call(
        functools.partial(_na_kernel, rows=rows),
        out_shape=(jax.ShapeDtypeStruct((tok.n_lat, d), BF16),
                   jax.ShapeDtypeStruct((batch * ctx, d), BF16)),
        grid=(heads, batch),
        in_specs=[lat(0), lat(1), lat(2), cx(0), cx(1), cx(2),
                  pl.BlockSpec((None, 3, gq, gk), lambda h, b: (h, 0, 0, 0))],
        out_specs=(pl.BlockSpec((seq, HEAD), lambda h, b: (b, h)),
                   pl.BlockSpec((ctx, HEAD), lambda h, b: (b, h))),
        compiler_params=_params("parallel", "parallel"),
        name="na_attention",
    )(qkv, qkv, qkv, qkv, qkv, qkv, bias_tables)
    return o_lat, o_ctx


def _gdn_in_kernel(x_ref, xp_ref, xn_ref, nw_ref, sh_ref, sc_ref, w_ref, cw_ref, qkv_ref, z_ref, h_scr, acc_scr,
                   *, tok, n_key_tiles, n_conv_tiles):
    m = pl.program_id(0)
    n = pl.program_id(1)
    tm = x_ref.shape[0]
    tn = w_ref.shape[1]

    @pl.when(n == 0)
    def _():
        nw, sh, sc = nw_ref[...], sh_ref[...], sc_ref[...]
        h_scr[pl.ds(0, CONV_PAD), :] = _norm_mod(xp_ref[...], nw, sh, sc).astype(BF16)
        h_scr[pl.ds(CONV_PAD, tm), :] = _norm_mod(x_ref[...], nw, sh, sc).astype(BF16)
        h_scr[pl.ds(CONV_PAD + tm, CONV_PAD), :] = _norm_mod(xn_ref[...], nw, sh, sc).astype(BF16)

    @pl.when(n >= n_conv_tiles)
    def _():
        z_ref[...] = _dot(h_scr[pl.ds(CONV_PAD, tm), :], w_ref[...]).astype(z_ref.dtype)

    def conv_tile(normalise):
        seq_len = jnp.where(m < tok.lat_tiles, tok.seq, tok.ctx)
        pos = (m * tm + lax.broadcasted_iota(jnp.int32, (tm, 1), 0)) & (seq_len - 1)
        ok = [pos >= 1, None, pos <= seq_len - 2, pos <= seq_len - 3]
        scale = jnp.where(n < n_key_tiles, GDN_SCALE, 1.0)
        h = h_scr[...]
        n_slabs = tn // NORM_SLAB
        acc_scr[0] = _dot(h, w_ref[:, :NORM_SLAB])
        for j in range(n_slabs):
            cols = slice(j * NORM_SLAB, (j + 1) * NORM_SLAB)
            if j + 1 < n_slabs:
                acc_scr[(j + 1) % 2] = _dot(h, w_ref[:, (j + 1) * NORM_SLAB:(j + 2) * NORM_SLAB])
            cw = cw_ref[:, cols]
            y = None
            for tap in range(CONV_TAPS):
                term = acc_scr[j % 2, pl.ds(CONV_PAD - (CONV_TAPS - 1) // 2 + tap, tm), :] * cw[tap:tap + 1, :]
                if ok[tap] is not None:
                    term = jnp.where(ok[tap], term, 0.0)
                y = term if y is None else y + term
            y = _silu(y)
            for i in range(NORM_SLAB // HEAD):
                yh = y[:, i * HEAD:(i + 1) * HEAD]
                if normalise:
                    yh = yh * (lax.rsqrt(jnp.sum(yh * yh, axis=-1, keepdims=True) + EPS) * scale)
                lo = j * NORM_SLAB + i * HEAD
                qkv_ref[:, lo:lo + HEAD] = yh

    pl.when(n < 2 * n_key_tiles)(functools.partial(conv_tile, True))
    pl.when((n >= 2 * n_key_tiles) & (n < n_conv_tiles))(functools.partial(conv_tile, False))


def _gdn_in(tok, x_all, norm_w, mod, w, conv_w, key_dim):
    d, n = w.shape
    conv_dim = conv_w.shape[1]
    tm = tok.tm
    tn = _pick(np.gcd(key_dim, n - conv_dim), (1024, 512, 256))
    n_conv_tiles = conv_dim // tn
    halo_blocks = tm // CONV_PAD
    last_halo = tok.n_all // CONV_PAD - 1
    assert tok.seq & (tok.seq - 1) == 0 and tok.ctx & (tok.ctx - 1) == 0
    kern = functools.partial(_gdn_in_kernel, tok=tok, n_key_tiles=key_dim // tn, n_conv_tiles=n_conv_tiles)
    return pl.pallas_call(
        kern,
        out_shape=(jax.ShapeDtypeStruct((tok.n_all, conv_dim), F32),
                   jax.ShapeDtypeStruct((tok.n_all, n - conv_dim), BF16)),
        grid=(tok.all_tiles, n // tn),
        in_specs=[
            pl.BlockSpec((tm, d), lambda m, j: (m, 0)),
            pl.BlockSpec((CONV_PAD, d), lambda m, j: (jnp.maximum(m * halo_blocks - 1, 0), 0)),
            pl.BlockSpec((CONV_PAD, d), lambda m, j: (jnp.minimum((m + 1) * halo_blocks, last_halo), 0)),
            pl.BlockSpec((1, d), lambda m, j: (0, 0)),
            _mod_spec(tok, d, 0),
            _mod_spec(tok, d, 1),
            pl.BlockSpec((d, tn), lambda m, j: (0, j)),
            pl.BlockSpec((CONV_TAPS, tn), lambda m, j: (0, jnp.minimum(j, n_conv_tiles - 1))),
        ],
        out_specs=(pl.BlockSpec((tm, tn), lambda m, j: (m, jnp.minimum(j, n_conv_tiles - 1))),
                   pl.BlockSpec((tm, tn), lambda m, j: (m, jnp.maximum(j - n_conv_tiles, 0)))),
        scratch_shapes=[pltpu.VMEM((tm + 2 * CONV_PAD, d), BF16),
                        pltpu.VMEM((2, tm + 2 * CONV_PAD, NORM_SLAB), F32)],
        compiler_params=_params("parallel", "arbitrary"),
        name="gdn_in",
    )(x_all, x_all, x_all, norm_w.reshape(1, d), mod, mod, w, conv_w)


def _gates_kernel(ba_ref, alog_ref, dt_ref, o_ref, *, n_chunks, n_vheads):
    row = lax.broadcasted_iota(jnp.int32, (CHUNK, CHUNK), 0)
    col = lax.broadcasted_iota(jnp.int32, (CHUNK, CHUNK), 1)
    prefix = (row <= col).astype(F32)
    suffix = (row >= col).astype(F32)
    is_decay = (row % (2 * n_vheads)) >= n_vheads
    neg_a = -jnp.exp(alog_ref[...])
    for c in range(n_chunks):
        bt = ba_ref[pl.ds(c * CHUNK, CHUNK), :].T
        beta = 1.0 / (1.0 + jnp.exp(-bt))
        x = bt + dt_ref[...]
        g = neg_a * (jnp.maximum(x, 0.0) + jnp.log1p(jnp.exp(-jnp.abs(x))))
        cum = jnp.where(row < 2 * n_vheads, _dot_hi(g, prefix), _dot_hi(g, suffix))
        o_ref[c] = jnp.where(is_decay, cum, beta)


def _gdn_gates(ba, a_log, dt_bias):
    n_tok, n_col = ba.shape
    n_vheads = a_log.shape[1]
    assert n_col == CHUNK and 4 * n_vheads <= n_col
    zeros = jnp.zeros_like(a_log)
    pad = ((0, n_col - 4 * n_vheads), (0, 0))
    alog_col = jnp.pad(jnp.concatenate([zeros, a_log], axis=1).reshape(4 * n_vheads, 1), pad)
    dt_col = jnp.pad(jnp.concatenate([zeros, dt_bias], axis=1).reshape(4 * n_vheads, 1), pad)
    per_step = 4
    assert (n_tok // CHUNK) % per_step == 0
    return pl.pallas_call(
        functools.partial(_gates_kernel, n_chunks=per_step, n_vheads=n_vheads),
        out_shape=jax.ShapeDtypeStruct((n_tok // CHUNK, n_col, CHUNK), F32),
        grid=(n_tok // (CHUNK * per_step),),
        in_specs=[pl.BlockSpec((CHUNK * per_step, n_col), lambda i: (i, 0)),
                  pl.BlockSpec((n_col, 1), lambda i: (0, 0)),
                  pl.BlockSpec((n_col, 1), lambda i: (0, 0))],
        out_specs=pl.BlockSpec((per_step, n_col, CHUNK), lambda i: (i, 0, 0)),
        compiler_params=_params("parallel"),
        name="gdn_gates",
    )(ba, alog_col, dt_col)


def _bdot(a, b):
    return jnp.einsum("pmk,pkn->pmn", a, b, preferred_element_type=F32)


def _bdot_nt(a, b):
    return jnp.einsum("pmk,pnk->pmn", a, b, preferred_element_type=F32)


def _unit_tri_inverse(l, row, col, tick):
    n = l.shape[-1]
    b = 2
    x = (row == col).astype(F32) - jnp.where(row // b == col // b, l, 0.0)
    while b < n:
        pair = (row // (2 * b) == col // (2 * b)) & (row // b != col // b)
        c16 = jnp.where(pair, l, 0.0).astype(BF16)
        x16 = x.astype(BF16)
        x = x - _bdot(x16, _bdot(c16, x16).astype(BF16))
        tick()
        b *= 2
    return x


def _chunk_factors(q, k, v, beta_r, g_r, row, col, reverse, tick):
    p = q.shape[0]

    def per_row(r):
        return jnp.stack([jnp.broadcast_to(r[c], (CHUNK, CHUNK)).T for c in range(p)], axis=0)

    incl = (row <= col) if reverse else (row >= col)
    strict = (row < col) if reverse else (row > col)
    beta_c = per_row(beta_r)
    g_c = per_row(g_r)
    decay = jnp.where(incl, jnp.exp(jnp.where(incl, g_c - g_r, 0.0)), 0.0)
    k16 = k.astype(BF16)
    k_beta = k * beta_c
    v_beta = v * beta_c
    l_mat = jnp.where(strict, _bdot_nt(k_beta.astype(BF16), k16) * decay, 0.0)
    tick()
    t_inv = _unit_tri_inverse(l_mat, row, col, tick)
    exp_g = jnp.exp(g_c)
    rhs = jnp.concatenate([k_beta * exp_g, v_beta], axis=2).astype(BF16)
    wu16 = _bdot(t_inv.astype(BF16), rhs).astype(BF16)
    tick()
    last = 0 if reverse else CHUNK - 1
    g_last = g_c[:, last:last + 1, :]
    kd = k * jnp.exp(g_last - g_c)
    kd_t = jnp.stack([kd[c].T for c in range(p)], axis=0).astype(BF16)
    a16 = jnp.where(incl, _bdot_nt(q.astype(BF16), k16) * decay, 0.0).astype(BF16)
    r = _bdot(jnp.concatenate([kd_t, a16], axis=1), wu16)
    tick()
    qe = q * exp_g - r[:, CHUNK:, :HEAD]
    mq = jnp.concatenate([-r[:, :CHUNK, :HEAD], qe], axis=1).astype(BF16)
    return mq, r[:, :, HEAD:], jnp.exp(g_last)


def _gdn_core_kernel(qf_ref, kf_ref, vf_ref, qb_ref, kb_ref, vb_ref, gf_ref, gb_ref, s0_ref, z_ref, nw_ref,
                     y_ref, sf_ref, s_scr, o_scr, *, n_vheads, rep, batch_heads):
    grp = pl.program_id(1)
    t = pl.program_id(2)
    n_t = pl.num_programs(2)
    tb = qf_ref.shape[0]
    cb = tb // CHUNK
    heads = s_scr.shape[0]
    row = lax.broadcasted_iota(jnp.int32, (CHUNK, CHUNK), 0)
    col = lax.broadcasted_iota(jnp.int32, (CHUNK, CHUNK), 1)

    @pl.when(t == 0)
    def _():
        s_scr[...] = s0_ref[...]

    pending = []

    def tick():
        if pending:
            pending.pop(0)()

    def recurrence(z, hs, mq, na, egl):
        state = {h: s_scr[h, z] for h in hs}
        blk = t if z == 0 else n_t - 1 - t

        def step(c):
            for j, h in enumerate(hs):
                s = state[h]
                r = _dot(mq[j * cb + c], s.astype(BF16)) + na[j * cb + c]
                o_scr[h, z, pl.ds(pl.multiple_of(blk * tb + c * CHUNK, CHUNK), CHUNK), :] = r[CHUNK:]
                state[h] = s * egl[j * cb + c] + r[:CHUNK]

        def finish():
            for h in hs:
                s_scr[h, z] = state[h]

        order = list(range(cb)) if z == 0 else list(reversed(range(cb)))
        return [functools.partial(step, c) for c in order] + [finish]

    def chunks(ref, lanes):
        return ref[:, lanes].reshape(cb, CHUNK, HEAD)

    for h0 in range(0, heads, batch_heads):
        hs = list(range(h0, h0 + batch_heads))
        for z in range(2):
            q_ref, k_ref, v_ref, gates_ref = (qf_ref, kf_ref, vf_ref, gf_ref) if z == 0 else (qb_ref, kb_ref, vb_ref, gb_ref)
            ksl = [slice((h // rep) * HEAD, (h // rep + 1) * HEAD) for h in hs]
            gate0 = z * 2 * n_vheads + grp * heads
            mq, na, egl = _chunk_factors(
                jnp.concatenate([chunks(q_ref, sl) for sl in ksl], axis=0),
                jnp.concatenate([chunks(k_ref, sl) for sl in ksl], axis=0),
                jnp.concatenate([chunks(v_ref, slice(h * HEAD, (h + 1) * HEAD)) for h in hs], axis=0),
                jnp.concatenate([gates_ref[:, pl.ds(gate0 + h, 1), :] for h in hs], axis=0),
                jnp.concatenate([gates_ref[:, pl.ds(gate0 + n_vheads + h, 1), :] for h in hs], axis=0),
                row, col, reverse=(z == 1), tick=tick)
            while pending:
                tick()
            pending.extend(recurrence(z, hs, mq, na, egl))
    while pending:
        tick()

    @pl.when(t == n_t - 1)
    def _():
        sf_ref[...] = s_scr[...]
        for hh in range(heads):
            o = o_scr[hh, 0] + o_scr[hh, 1]
            ms = jnp.mean(o * o, axis=-1, keepdims=True)
            zz = z_ref[:, hh * HEAD:(hh + 1) * HEAD].astype(F32)
            y = (o * lax.rsqrt(ms + EPS) * nw_ref[...]) * _silu(zz)
            y_ref[:, hh * HEAD:(hh + 1) * HEAD] = y.astype(y_ref.dtype)


def _gdn_core(qkv_conv, gates, s0, z_gate, norm_w, n_seq, seq_len, blk0, chunk0, key_dim, n_vheads, heads,
              batch_heads):
    tb = min(seq_len, PREP_ROWS)
    n_tb = seq_len // tb
    cb = tb // CHUNK
    nk = key_dim // HEAD
    rep = n_vheads // nk
    kw = max(1, heads // rep)
    assert (heads == 1 or heads % rep == 0) and heads % batch_heads == 0
    assert n_vheads % heads == 0 and nk % kw == 0 and (2 * nk) % heads == 0 and chunk0 == blk0 * n_tb * cb

    def fwd(s, g, t):
        return (blk0 + s) * n_tb + t

    def bwd(s, g, t):
        return (blk0 + s) * n_tb + n_tb - 1 - t

    def specs(tblk):
        return [
            pl.BlockSpec((tb, kw * HEAD), lambda s, g, t: (tblk(s, g, t), (g * heads) // (rep * kw))),
            pl.BlockSpec((tb, kw * HEAD), lambda s, g, t: (tblk(s, g, t), nk // kw + (g * heads) // (rep * kw))),
            pl.BlockSpec((tb, heads * HEAD), lambda s, g, t: (tblk(s, g, t), 2 * nk // heads + g)),
        ]

    def gate_spec(tblk):
        return pl.BlockSpec((cb, CHUNK, CHUNK), lambda s, g, t: (tblk(s, g, t), 0, 0))

    state_spec = pl.BlockSpec((None, heads, 2, HEAD, HEAD), lambda s, g, t: (s, g, 0, 0, 0))
    return pl.pallas_call(
        functools.partial(_gdn_core_kernel, n_vheads=n_vheads, rep=rep, batch_heads=batch_heads),
        out_shape=(jax.ShapeDtypeStruct((n_seq * seq_len, n_vheads * HEAD), BF16),
                   jax.ShapeDtypeStruct((n_seq, n_vheads, 2, HEAD, HEAD), F32)),
        grid=(n_seq, n_vheads // heads, n_tb),
        in_specs=specs(fwd) + specs(bwd) + [gate_spec(fwd), gate_spec(bwd), state_spec,
                  pl.BlockSpec((seq_len, heads * HEAD), lambda s, g, t: (blk0 + s, g)),
                  pl.BlockSpec((1, HEAD), lambda s, g, t: (0, 0))],
        out_specs=(pl.BlockSpec((seq_len, heads * HEAD), lambda s, g, t: (s, g)), state_spec),
        scratch_shapes=[pltpu.VMEM((heads, 2, HEAD, HEAD), F32), pltpu.VMEM((heads, 2, seq_len, HEAD), F32)],
        compiler_params=_params("parallel", "parallel", "arbitrary"),
        name=f"gdn_core_{seq_len}",
    )(qkv_conv, qkv_conv, qkv_conv, qkv_conv, qkv_conv, qkv_conv, gates, gates, s0, z_gate, norm_w.reshape(1, HEAD))


def _gdn_mixer(tok, qkv_conv, z_gate, ba, key_dim, a_log, dt_bias, norm_w):
    n_vheads = z_gate.shape[1] // HEAD
    batch, seq, ctx = tok.batch, tok.seq, tok.ctx
    gates = _gdn_gates(ba, a_log, dt_bias)

    def run(n_seq, seq_len, blk0, s0, heads, batch_heads):
        return _gdn_core(qkv_conv, gates, s0, z_gate, norm_w, n_seq, seq_len, blk0, blk0 * seq_len // CHUNK,
                         key_dim, n_vheads, heads, batch_heads)

    zero_state = jnp.zeros((batch, n_vheads, 2, HEAD, HEAD), F32)
    ctx_heads = min(CTX_HEADS_PER_STEP, n_vheads)
    y_ctx, s_ctx = run(batch, ctx, tok.n_lat // ctx, zero_state, ctx_heads, ctx_heads)
    y_lat, _ = run(batch, seq, 0, s_ctx, LAT_HEADS_PER_STEP, LAT_HEADS_PER_STEP)
    return y_lat, y_ctx


def kernel(x, c, ctx, c_ctx, w_ada, b_ada, norm1_w, norm2_w, na_w_qkv, na_w_o, na_q_gain, na_k_gain, na_rpb, gdn_w_in, gdn_conv_w, gdn_w_ba, gdn_a_log, gdn_dt_bias, gdn_norm_w, gdn_w_o, ffn_w1, ffn_w3, ffn_w2):
    batch, seq, d = x.shape
    n_ctx = ctx.shape[1]
    depth = w_ada.shape[0]
    tok = _Tokens(batch, seq, n_ctx)
    assert batch + 1 <= 8

    x_all = jnp.concatenate([x.reshape(batch * seq, d), ctx.reshape(batch * n_ctx, d)], axis=0)
    cond = jnp.concatenate([c, c_ctx[None, :], jnp.zeros((8 - batch - 1, d), F32)], axis=0)
    mods = _ada(cond, w_ada, b_ada)

    for i in range(depth):
        update_ctx = i < depth - 1
        n_tiles = tok.all_tiles if update_ctx else tok.lat_tiles
        mod = mods[i].reshape(8, 1, 6 * d)
        j = i // 2
        if i % 2 == 0:
            qkv = _norm_mod_matmul(tok, x_all, norm1_w[i], mod, na_w_qkv[j].astype(BF16), BF16, "na_qkv",
                                   na_gains=(na_q_gain[j], na_k_gain[j]))
            tables = _na_bias_tables(na_rpb[j], seq // GRID_W)
            mix_lat, mix_ctx = _na_attention(tok, qkv, tables, d)
            w_o = na_w_o[j]
        else:
            conv_dim = gdn_conv_w.shape[2]
            key_dim = (2 * conv_dim - gdn_w_in.shape[2]) // 2
            qkv_conv, z_gate = _gdn_in(tok, x_all, norm1_w[i], mod, gdn_w_in[j].astype(BF16), gdn_conv_w[j], key_dim)
            w_ba = jnp.concatenate([gdn_w_ba[j, 0], gdn_w_ba[j, 1]], axis=1).astype(BF16)
            w_ba = jnp.pad(w_ba, ((0, 0), (0, CHUNK - w_ba.shape[1])))
            ba = _norm_mod_matmul(tok, x_all, norm1_w[i], mod, w_ba, F32, "gdn_ba")
            mix_lat, mix_ctx = _gdn_mixer(tok, qkv_conv, z_gate, ba, key_dim, gdn_a_log[j], gdn_dt_bias[j],
                                          gdn_norm_w[j])
            w_o = gdn_w_o[j]
        w_o = w_o.astype(BF16)
        x_all = _matmul_residual(tok, 0, mix_lat, w_o, x_all, mod, 2, "mixer_out")
        if update_ctx:
            x_all = _matmul_residual(tok, tok.lat_tiles, mix_ctx, w_o, x_all, mod, 2, "mixer_out_ctx")
        x_all = _ffn(tok, n_tiles, x_all, norm2_w[i], mod, ffn_w1[i].astype(BF16), ffn_w3[i].astype(BF16),
                     ffn_w2[i].astype(BF16), in_place=update_ctx)
    return x_all.reshape(batch, seq, d)
```

```python
import functools

import jax
import jax.numpy as jnp
import numpy as np
from jax import lax
from jax.experimental import pallas as pl
from jax.experimental.pallas import tpu as pltpu

HEAD = 128
GRID_W = 64
WIN_H = 8
WIN_W = 16
ROW_GROUP = 4
NA_GROUP_UNROLL = 2
KEY_ROWS = ROW_GROUP + WIN_H
CHUNK = 128
CONV_TAPS = 4
CONV_PAD = 8
EPS = 1e-6
NEG_INF = -1e30
NA_SCALE = HEAD ** -0.5
LOG2E = 1.4426950408889634
GDN_SCALE = HEAD ** -0.5
TOKEN_TILE = 512
NORM_SLAB = 256
PREP_ROWS = 1024
CTX_HEADS_PER_STEP = 8
LAT_HEADS_PER_STEP = 2
VMEM_LIMIT = 48 * 1024 * 1024

F32 = jnp.float32
BF16 = jnp.bfloat16


def _params(*sem):
    return pltpu.CompilerParams(dimension_semantics=sem, vmem_limit_bytes=VMEM_LIMIT)


def _pick(n, cands):
    for c in cands:
        if n % c == 0:
            return c
    raise ValueError(f"no tile in {cands} divides {n}")


def _silu(x):
    return x / (1.0 + jnp.exp(-x))


def _dot(a, b):
    return jnp.dot(a, b, preferred_element_type=F32)


def _dot_nt(a, b):
    return lax.dot_general(a, b, (((1,), (1,)), ((), ())), preferred_element_type=F32)


def _dot_tn(a, b):
    return lax.dot_general(a, b, (((0,), (0,)), ((), ())), preferred_element_type=F32)


def _dot_hi(a, b):
    return jnp.dot(a, b, preferred_element_type=F32, precision=lax.Precision.HIGHEST)


def _ada_kernel(c_ref, w_ref, b_ref, o_ref):
    s = _silu(c_ref[...]).astype(BF16)
    o_ref[...] = _dot(s, w_ref[...].astype(BF16)) + b_ref[...]


def _ada(cond, w_ada, b_ada):
    depth, d, n = w_ada.shape
    tn = _pick(n, (1024, 512, 256, 128))
    return pl.pallas_call(
        _ada_kernel,
        out_shape=jax.ShapeDtypeStruct((depth, 8, n), F32),
        grid=(depth, n // tn),
        in_specs=[
            pl.BlockSpec((8, d), lambda l, j: (0, 0)),
            pl.BlockSpec((None, d, tn), lambda l, j: (l, 0, j)),
            pl.BlockSpec((None, 1, tn), lambda l, j: (l, 0, j)),
        ],
        out_specs=pl.BlockSpec((None, 8, tn), lambda l, j: (l, 0, j)),
        compiler_params=_params("parallel", "parallel"),
        name="ada_mod",
    )(cond, w_ada, b_ada.reshape(depth, 1, n))


class _Tokens:
    def __init__(self, batch, seq, ctx):
        self.batch, self.seq, self.ctx = batch, seq, ctx
        self.n_lat = batch * seq
        self.n_all = self.n_lat + batch * ctx
        self.tm = TOKEN_TILE
        assert seq % self.tm == 0 and (batch * ctx) % self.tm == 0
        self.lat_tiles = self.n_lat // self.tm
        self.all_tiles = self.n_all // self.tm
        self.tiles_per_batch = seq // self.tm

    def mod_row(self, t):
        return jnp.where(t < self.lat_tiles, t // self.tiles_per_batch, self.batch)


def _mod_spec(tok, d, chunk):
    return pl.BlockSpec((None, 1, d), lambda m, j: (tok.mod_row(m), 0, chunk))


def _norm_mod(x, nw, sh, sc):
    ms = jnp.mean(x * x, axis=-1, keepdims=True)
    return (x * lax.rsqrt(ms + EPS) * nw) * (1.0 + sc) + sh


def _nmm_kernel(x_ref, nw_ref, sh_ref, sc_ref, w_ref, o_ref, h_scr):
    @pl.when(pl.program_id(1) == 0)
    def _():
        h_scr[...] = _norm_mod(x_ref[...], nw_ref[...], sh_ref[...], sc_ref[...]).astype(BF16)

    o_ref[...] = _dot(h_scr[...], w_ref[...]).astype(o_ref.dtype)


def _nmm_na_kernel(x_ref, nw_ref, sh_ref, sc_ref, w_ref, qg_ref, kg_ref, o_ref, h_scr, *, tiles_per_kind):
    n = pl.program_id(1)

    @pl.when(n == 0)
    def _():
        h_scr[...] = _norm_mod(x_ref[...], nw_ref[...], sh_ref[...], sc_ref[...]).astype(BF16)

    kind = n // tiles_per_kind

    @pl.when(kind == 2)
    def _():
        o_ref[...] = _dot(h_scr[...], w_ref[...]).astype(o_ref.dtype)

    @pl.when(kind < 2)
    def _():
        gain = jnp.where(kind == 0, qg_ref[...] * (NA_SCALE * LOG2E), kg_ref[...])
        h = h_scr[...]
        n_slabs = o_ref.shape[1] // NORM_SLAB
        nxt = _dot(h, w_ref[:, :NORM_SLAB])
        for j in range(n_slabs):
            acc = nxt
            if j + 1 < n_slabs:
                nxt = _dot(h, w_ref[:, (j + 1) * NORM_SLAB:(j + 2) * NORM_SLAB])
            for i in range(NORM_SLAB // HEAD):
                a = acc[:, i * HEAD:(i + 1) * HEAD]
                ms = jnp.mean(a * a, axis=-1, keepdims=True)
                lo = j * NORM_SLAB + i * HEAD
                o_ref[:, lo:lo + HEAD] = (a * lax.rsqrt(ms + EPS) * gain).astype(o_ref.dtype)


def _norm_mod_matmul(tok, x_all, norm_w, mod, w, layer, out_dtype, name, na_gains=None):
    _, d, n = w.shape
    tm = tok.tm
    if na_gains is None:
        tn = _pick(n, (1024, 512, 256, 128))
        kern = _nmm_kernel
        extra_specs, extra_args = [], []
    else:
        tn = _pick(d, (1024, 512, 256, 128))
        kern = functools.partial(_nmm_na_kernel, tiles_per_kind=d // tn)
        extra_specs = [pl.BlockSpec((1, HEAD), lambda m, j: (0, 0))] * 2
        extra_args = [g.reshape(1, HEAD) for g in na_gains]
    return pl.pallas_call(
        kern,
        out_shape=jax.ShapeDtypeStruct((tok.n_all, n), out_dtype),
        grid=(tok.all_tiles, n // tn),
        in_specs=[
            pl.BlockSpec((tm, d), lambda m, j: (m, 0)),
            pl.BlockSpec((1, d), lambda m, j: (0, 0)),
            _mod_spec(tok, d, 0),
            _mod_spec(tok, d, 1),
            pl.BlockSpec((None, d, tn), lambda m, j: (layer, 0, j)),
        ] + extra_specs,
        out_specs=pl.BlockSpec((tm, tn), lambda m, j: (m, j)),
        scratch_shapes=[pltpu.VMEM((tm, d), BF16)],
        compiler_params=_params("parallel", "arbitrary"),
        name=name,
    )(x_all, norm_w.reshape(1, d), mod, mod, w, *extra_args)


def _mm_res_kernel(a_ref, w_ref, x_ref, g_ref, o_ref):
    o_ref[...] = x_ref[...] + g_ref[...] * _dot(a_ref[...], w_ref[...])


def _matmul_residual(tok, tile0, a, w, layer, x_all, mod, gate_chunk, name):
    _, k, d = w.shape
    tm = tok.tm
    tn = _pick(d, (1024, 512, 256, 128))
    n_tiles = a.shape[0] // tm
    return pl.pallas_call(
        _mm_res_kernel,
        out_shape=jax.ShapeDtypeStruct(x_all.shape, F32),
        grid=(n_tiles, d // tn),
        in_specs=[
            pl.BlockSpec((tm, k), lambda m, j: (m, 0)),
            pl.BlockSpec((None, k, tn), lambda m, j: (layer, 0, j)),
            pl.BlockSpec((tm, tn), lambda m, j: (tile0 + m, j)),
            pl.BlockSpec((None, 1, tn), lambda m, j: (tok.mod_row(tile0 + m), 0, gate_chunk * (d // tn) + j)),
        ],
        out_specs=pl.BlockSpec((tm, tn), lambda m, j: (tile0 + m, j)),
        input_output_aliases={2: 0},
        compiler_params=_params("parallel", "parallel"),
        name=name,
    )(a, w, x_all, mod)


def _ffn_kernel(x_ref, nw_ref, sh_ref, sc_ref, g_ref, w1_ref, w3_ref, w2_ref, o_ref, h_scr, acc_scr):
    f = pl.program_id(1)

    @pl.when(f == 0)
    def _():
        h_scr[...] = _norm_mod(x_ref[...], nw_ref[...], sh_ref[...], sc_ref[...]).astype(BF16)
        acc_scr[...] = jnp.zeros_like(acc_scr)

    h = h_scr[...]
    u = _silu(_dot(h, w1_ref[...])) * _dot(h, w3_ref[...])
    acc_scr[...] += _dot(u.astype(BF16), w2_ref[...])

    @pl.when(f == pl.num_programs(1) - 1)
    def _():
        o_ref[...] = x_ref[...] + g_ref[...] * acc_scr[...]


def _ffn(tok, n_tiles, x_all, norm_w, mod, w1, w3, w2, layer, in_place):
    _, d, dff = w1.shape
    tm = tok.tm
    tf = _pick(dff, (512, 256, 128))
    return pl.pallas_call(
        _ffn_kernel,
        out_shape=jax.ShapeDtypeStruct(x_all.shape if in_place else (n_tiles * tm, d), F32),
        grid=(n_tiles, dff // tf),
        in_specs=[
            pl.BlockSpec((tm, d), lambda m, f: (m, 0)),
            pl.BlockSpec((1, d), lambda m, f: (0, 0)),
            _mod_spec(tok, d, 3),
            _mod_spec(tok, d, 4),
            _mod_spec(tok, d, 5),
            pl.BlockSpec((None, d, tf), lambda m, f: (layer, 0, f)),
            pl.BlockSpec((None, d, tf), lambda m, f: (layer, 0, f)),
            pl.BlockSpec((None, tf, d), lambda m, f: (layer, f, 0)),
        ],
        out_specs=pl.BlockSpec((tm, d), lambda m, f: (m, 0)),
        scratch_shapes=[pltpu.VMEM((tm, d), BF16), pltpu.VMEM((tm, d), F32)],
        input_output_aliases={0: 0} if in_place else {},
        compiler_params=_params("parallel", "arbitrary"),
        name="ffn",
    )(x_all, norm_w.reshape(1, d), mod, mod, mod, w1, w3, w2)


def _na_bias_tables(rpb, rows):
    kh = min(WIN_H, rows)
    cols = np.arange(GRID_W)
    col_start = np.clip(cols - WIN_W // 2, 0, GRID_W - WIN_W)
    in_win = (cols[None, :] >= col_start[:, None]) & (cols[None, :] < col_start[:, None] + WIN_W)
    dc = np.clip(cols[None, :] - cols[:, None], 1 - WIN_W, WIN_W - 1) + WIN_W - 1
    off = np.zeros((3, ROW_GROUP, KEY_ROWS), np.int32)
    ok = np.zeros((3, ROW_GROUP, KEY_ROWS), bool)
    for t, r0 in enumerate((0, ROW_GROUP, rows - ROW_GROUP)):
        start = min(max(r0 - WIN_H // 2, 0), rows - KEY_ROWS)
        for i in range(ROW_GROUP):
            r = r0 + i
            rs = min(max(r - kh // 2, 0), rows - kh)
            for m in range(KEY_ROWS):
                j = start + m - rs
                ok[t, i, m] = 0 <= j < kh
                off[t, i, m] = min(max(rs - r + j + WIN_H - 1, 0), 2 * WIN_H - 2)
    heads = rpb.shape[0]
    onehot = ((dc[None] == np.arange(2 * WIN_W - 1)[:, None, None]) & in_win[None]).astype(np.float32)
    by_col = jnp.einsum("hod,dqk->hqok", rpb.astype(F32), onehot, precision=lax.Precision.HIGHEST)
    by_col = (by_col + np.where(in_win, 0.0, NEG_INF).astype(np.float32)[:, None, :]) * LOG2E
    pieces = []
    for t in range(3):
        for i in range(ROW_GROUP):
            valid = np.nonzero(ok[t, i])[0]
            m0, m1 = int(valid[0]), int(valid[-1]) + 1
            o0 = int(off[t, i, m0])
            assert np.array_equal(valid, np.arange(m0, m1))
            assert np.array_equal(off[t, i, m0:m1], o0 + np.arange(m1 - m0))
            pieces.append(jnp.pad(by_col[:, :, o0:o0 + m1 - m0, :], ((0, 0), (0, 0), (m0, KEY_ROWS - m1), (0, 0)),
                                  constant_values=NEG_INF * LOG2E))
    return jnp.stack(pieces, axis=1).reshape(heads, 3, ROW_GROUP * GRID_W, KEY_ROWS * GRID_W)


def _softmax_pv(parts):
    m = functools.reduce(jnp.maximum, [jnp.max(s, axis=-1, keepdims=True) for s, _ in parts])
    ps = [jnp.exp2(s - m) for s, _ in parts]
    den = functools.reduce(jnp.add, [jnp.sum(p, axis=-1, keepdims=True) for p in ps])
    num = functools.reduce(jnp.add, [_dot(p.astype(BF16), v) for p, (_, v) in zip(ps, parts)])
    return num / den


def _na_kernel(q_ref, k_ref, v_ref, qc_ref, kc_ref, vc_ref, bias_ref, o_ref, oc_ref, *, rows):
    kc = kc_ref[...]
    vc = vc_ref[...]
    n_groups = rows // ROW_GROUP
    gq = ROW_GROUP * GRID_W
    gk = KEY_ROWS * GRID_W

    def group(gi, carry):
        r0 = gi * ROW_GROUP
        start = jnp.clip(r0 - WIN_H // 2, 0, rows - KEY_ROWS)
        table = jnp.where(gi == 0, 0, jnp.where(gi == n_groups - 1, 2, 1))
        q = q_ref[pl.ds(pl.multiple_of(r0 * GRID_W, gq), gq), :]
        koff = pl.multiple_of(start * GRID_W, GRID_W)
        ks = k_ref[pl.ds(koff, gk), :]
        vs = v_ref[pl.ds(koff, gk), :]
        s_loc = _dot_nt(q, ks) + bias_ref[table]
        s_ctx = _dot_nt(q, kc)
        o = _softmax_pv([(s_loc, vs), (s_ctx, vc)])
        o_ref[pl.ds(pl.multiple_of(r0 * GRID_W, gq), gq), :] = o.astype(o_ref.dtype)
        return carry

    lax.fori_loop(0, n_groups, group, 0, unroll=NA_GROUP_UNROLL)
    s_cc = _dot_nt(qc_ref[...], kc)
    oc_ref[...] = _softmax_pv([(s_cc, vc)]).astype(oc_ref.dtype)


def _na_attention(tok, qkv, bias_tables, d):
    heads = d // HEAD
    seq, ctx, batch = tok.seq, tok.ctx, tok.batch
    rows = seq // GRID_W
    assert rows % ROW_GROUP == 0 and rows >= KEY_ROWS
    ctx_blk0 = tok.n_lat // ctx
    gq, gk = ROW_GROUP * GRID_W, KEY_ROWS * GRID_W

    def lat(kind):
        return pl.BlockSpec((seq, HEAD), lambda h, b: (b, kind * heads + h))

    def cx(kind):
        return pl.BlockSpec((ctx, HEAD), lambda h, b: (ctx_blk0 + b, kind * heads + h))

    o_lat, o_ctx = pl.pallas_call(
        functools.partial(_na_kernel, rows=rows),
        out_shape=(jax.ShapeDtypeStruct((tok.n_lat, d), BF16),
                   jax.ShapeDtypeStruct((batch * ctx, d), BF16)),
        grid=(heads, batch),
        in_specs=[lat(0), lat(1), lat(2), cx(0), cx(1), cx(2),
                  pl.BlockSpec((None, 3, gq, gk), lambda h, b: (h, 0, 0, 0))],
        out_specs=(pl.BlockSpec((seq, HEAD), lambda h, b: (b, h)),
                   pl.BlockSpec((ctx, HEAD), lambda h, b: (b, h))),
        compiler_params=_params("parallel", "parallel"),
        name="na_attention",
    )(qkv, qkv, qkv, qkv, qkv, qkv, bias_tables)
    return o_lat, o_ctx


def _gdn_in_kernel(x_ref, xp_ref, xn_ref, nw_ref, sh_ref, sc_ref, w_ref, cw_ref, qkv_ref, z_ref, h_scr, acc_scr,
                   *, tok, n_key_tiles, n_conv_tiles):
    m = pl.program_id(0)
    n = pl.program_id(1)
    tm = x_ref.shape[0]
    tn = w_ref.shape[1]

    @pl.when(n == 0)
    def _():
        nw, sh, sc = nw_ref[...], sh_ref[...], sc_ref[...]
        h_scr[pl.ds(0, CONV_PAD), :] = _norm_mod(xp_ref[...], nw, sh, sc).astype(BF16)
        h_scr[pl.ds(CONV_PAD, tm), :] = _norm_mod(x_ref[...], nw, sh, sc).astype(BF16)
        h_scr[pl.ds(CONV_PAD + tm, CONV_PAD), :] = _norm_mod(xn_ref[...], nw, sh, sc).astype(BF16)

    @pl.when(n >= n_conv_tiles)
    def _():
        z_ref[...] = _dot(h_scr[pl.ds(CONV_PAD, tm), :], w_ref[...]).astype(z_ref.dtype)

    def conv_tile(normalise):
        seq_len = jnp.where(m < tok.lat_tiles, tok.seq, tok.ctx)
        pos = (m * tm + lax.broadcasted_iota(jnp.int32, (tm, 1), 0)) & (seq_len - 1)
        ok = [pos >= 1, None, pos <= seq_len - 2, pos <= seq_len - 3]
        scale = jnp.where(n < n_key_tiles, GDN_SCALE, 1.0)
        h = h_scr[...]
        n_slabs = tn // NORM_SLAB
        acc_scr[0] = _dot(h, w_ref[:, :NORM_SLAB])
        for j in range(n_slabs):
            cols = slice(j * NORM_SLAB, (j + 1) * NORM_SLAB)
            if j + 1 < n_slabs:
                acc_scr[(j + 1) % 2] = _dot(h, w_ref[:, (j + 1) * NORM_SLAB:(j + 2) * NORM_SLAB])
            cw = cw_ref[:, cols]
            y = None
            for tap in range(CONV_TAPS):
                term = acc_scr[j % 2, pl.ds(CONV_PAD - (CONV_TAPS - 1) // 2 + tap, tm), :] * cw[tap:tap + 1, :]
                if ok[tap] is not None:
                    term = jnp.where(ok[tap], term, 0.0)
                y = term if y is None else y + term
            y = _silu(y)
            for i in range(NORM_SLAB // HEAD):
                yh = y[:, i * HEAD:(i + 1) * HEAD]
                if normalise:
                    yh = yh * (lax.rsqrt(jnp.sum(yh * yh, axis=-1, keepdims=True) + EPS) * scale)
                lo = j * NORM_SLAB + i * HEAD
                qkv_ref[:, lo:lo + HEAD] = yh

    pl.when(n < 2 * n_key_tiles)(functools.partial(conv_tile, True))
    pl.when((n >= 2 * n_key_tiles) & (n < n_conv_tiles))(functools.partial(conv_tile, False))


def _gdn_in(tok, x_all, norm_w, mod, w, layer, conv_w, key_dim):
    _, d, n = w.shape
    conv_dim = conv_w.shape[1]
    tm = tok.tm
    tn = _pick(np.gcd(key_dim, n - conv_dim), (1024, 512, 256))
    n_conv_tiles = conv_dim // tn
    halo_blocks = tm // CONV_PAD
    last_halo = tok.n_all // CONV_PAD - 1
    assert tok.seq & (tok.seq - 1) == 0 and tok.ctx & (tok.ctx - 1) == 0
    kern = functools.partial(_gdn_in_kernel, tok=tok, n_key_tiles=key_dim // tn, n_conv_tiles=n_conv_tiles)
    return pl.pallas_call(
        kern,
        out_shape=(jax.ShapeDtypeStruct((tok.n_all, conv_dim), F32),
                   jax.ShapeDtypeStruct((tok.n_all, n - conv_dim), BF16)),
        grid=(tok.all_tiles, n // tn),
        in_specs=[
            pl.BlockSpec((tm, d), lambda m, j: (m, 0)),
            pl.BlockSpec((CONV_PAD, d), lambda m, j: (jnp.maximum(m * halo_blocks - 1, 0), 0)),
            pl.BlockSpec((CONV_PAD, d), lambda m, j: (jnp.minimum((m + 1) * halo_blocks, last_halo), 0)),
            pl.BlockSpec((1, d), lambda m, j: (0, 0)),
            _mod_spec(tok, d, 0),
            _mod_spec(tok, d, 1),
            pl.BlockSpec((None, d, tn), lambda m, j: (layer, 0, j)),
            pl.BlockSpec((CONV_TAPS, tn), lambda m, j: (0, jnp.minimum(j, n_conv_tiles - 1))),
        ],
        out_specs=(pl.BlockSpec((tm, tn), lambda m, j: (m, jnp.minimum(j, n_conv_tiles - 1))),
                   pl.BlockSpec((tm, tn), lambda m, j: (m, jnp.maximum(j - n_conv_tiles, 0)))),
        scratch_shapes=[pltpu.VMEM((tm + 2 * CONV_PAD, d), BF16),
                        pltpu.VMEM((2, tm + 2 * CONV_PAD, NORM_SLAB), F32)],
        compiler_params=_params("parallel", "arbitrary"),
        name="gdn_in",
    )(x_all, x_all, x_all, norm_w.reshape(1, d), mod, mod, w, conv_w)


def _gates_kernel(ba_ref, alog_ref, dt_ref, o_ref, *, n_chunks, n_vheads):
    row = lax.broadcasted_iota(jnp.int32, (CHUNK, CHUNK), 0)
    col = lax.broadcasted_iota(jnp.int32, (CHUNK, CHUNK), 1)
    prefix = (row <= col).astype(F32)
    suffix = (row >= col).astype(F32)
    is_decay = (row % (2 * n_vheads)) >= n_vheads
    neg_a = -jnp.exp(alog_ref[...])
    for c in range(n_chunks):
        bt = ba_ref[pl.ds(c * CHUNK, CHUNK), :].T
        beta = 1.0 / (1.0 + jnp.exp(-bt))
        x = bt + dt_ref[...]
        g = neg_a * (jnp.maximum(x, 0.0) + jnp.log1p(jnp.exp(-jnp.abs(x))))
        cum = jnp.where(row < 2 * n_vheads, _dot_hi(g, prefix), _dot_hi(g, suffix))
        o_ref[c] = jnp.where(is_decay, cum, beta)


def _gdn_gates(ba, a_log, dt_bias):
    n_tok, n_col = ba.shape
    n_vheads = a_log.shape[1]
    assert n_col == CHUNK and 4 * n_vheads <= n_col
    zeros = jnp.zeros_like(a_log)
    pad = ((0, n_col - 4 * n_vheads), (0, 0))
    alog_col = jnp.pad(jnp.concatenate([zeros, a_log], axis=1).reshape(4 * n_vheads, 1), pad)
    dt_col = jnp.pad(jnp.concatenate([zeros, dt_bias], axis=1).reshape(4 * n_vheads, 1), pad)
    per_step = 4
    assert (n_tok // CHUNK) % per_step == 0
    return pl.pallas_call(
        functools.partial(_gates_kernel, n_chunks=per_step, n_vheads=n_vheads),
        out_shape=jax.ShapeDtypeStruct((n_tok // CHUNK, n_col, CHUNK), F32),
        grid=(n_tok // (CHUNK * per_step),),
        in_specs=[pl.BlockSpec((CHUNK * per_step, n_col), lambda i: (i, 0)),
                  pl.BlockSpec((n_col, 1), lambda i: (0, 0)),
                  pl.BlockSpec((n_col, 1), lambda i: (0, 0))],
        out_specs=pl.BlockSpec((per_step, n_col, CHUNK), lambda i: (i, 0, 0)),
        compiler_params=_params("parallel"),
        name="gdn_gates",
    )(ba, alog_col, dt_col)


def _bdot(a, b):
    return jnp.einsum("pmk,pkn->pmn", a, b, preferred_element_type=F32)


def _bdot_nt(a, b):
    return jnp.einsum("pmk,pnk->pmn", a, b, preferred_element_type=F32)


def _unit_tri_inverse(l, row, col, tick):
    n = l.shape[-1]
    b = 2
    x = (row == col).astype(F32) - jnp.where(row // b == col // b, l, 0.0)
    while b < n:
        pair = (row // (2 * b) == col // (2 * b)) & (row // b != col // b)
        c16 = jnp.where(pair, l, 0.0).astype(BF16)
        x16 = x.astype(BF16)
        x = x - _bdot(x16, _bdot(c16, x16).astype(BF16))
        tick()
        b *= 2
    return x


def _chunk_factors(q, k, v, beta_r, g_r, row, col, reverse, tick):
    p = q.shape[0]

    def per_row(r):
        return jnp.stack([jnp.broadcast_to(r[c], (CHUNK, CHUNK)).T for c in range(p)], axis=0)

    incl = (row <= col) if reverse else (row >= col)
    strict = (row < col) if reverse else (row > col)
    beta_c = per_row(beta_r)
    g_c = per_row(g_r)
    decay = jnp.where(incl, jnp.exp(jnp.where(incl, g_c - g_r, 0.0)), 0.0)
    k16 = k.astype(BF16)
    k_beta = k * beta_c
    v_beta = v * beta_c
    l_mat = jnp.where(strict, _bdot_nt(k_beta.astype(BF16), k16) * decay, 0.0)
    tick()
    t_inv = _unit_tri_inverse(l_mat, row, col, tick)
    exp_g = jnp.exp(g_c)
    rhs = jnp.concatenate([k_beta * exp_g, v_beta], axis=2).astype(BF16)
    wu16 = _bdot(t_inv.astype(BF16), rhs).astype(BF16)
    tick()
    last = 0 if reverse else CHUNK - 1
    g_last = g_c[:, last:last + 1, :]
    kd = k * jnp.exp(g_last - g_c)
    kd_t = jnp.stack([kd[c].T for c in range(p)], axis=0).astype(BF16)
    a16 = jnp.where(incl, _bdot_nt(q.astype(BF16), k16) * decay, 0.0).astype(BF16)
    r = _bdot(jnp.concatenate([kd_t, a16], axis=1), wu16)
    tick()
    qe = q * exp_g - r[:, CHUNK:, :HEAD]
    mq = jnp.concatenate([-r[:, :CHUNK, :HEAD], qe], axis=1).astype(BF16)
    return mq, r[:, :, HEAD:], jnp.exp(g_last)


def _gdn_core_kernel(qf_ref, kf_ref, vf_ref, qb_ref, kb_ref, vb_ref, gf_ref, gb_ref, s0_ref, z_ref, nw_ref,
                     y_ref, sf_ref, s_scr, o_scr, *, n_vheads, rep, batch_heads):
    grp = pl.program_id(1)
    t = pl.program_id(2)
    n_t = pl.num_programs(2)
    tb = qf_ref.shape[0]
    cb = tb // CHUNK
    heads = s_scr.shape[0]
    row = lax.broadcasted_iota(jnp.int32, (CHUNK, CHUNK), 0)
    col = lax.broadcasted_iota(jnp.int32, (CHUNK, CHUNK), 1)

    @pl.when(t == 0)
    def _():
        s_scr[...] = s0_ref[...]

    pending = []

    def tick():
        if pending:
            pending.pop(0)()

    def recurrence(z, hs, mq, na, egl):
        state = {h: s_scr[h, z] for h in hs}
        blk = t if z == 0 else n_t - 1 - t

        def step(c):
            for j, h in enumerate(hs):
                s = state[h]
                r = _dot(mq[j * cb + c], s.astype(BF16)) + na[j * cb + c]
                o_scr[h, z, pl.ds(pl.multiple_of(blk * tb + c * CHUNK, CHUNK), CHUNK), :] = r[CHUNK:]
                state[h] = s * egl[j * cb + c] + r[:CHUNK]

        def finish():
            for h in hs:
                s_scr[h, z] = state[h]

        order = list(range(cb)) if z == 0 else list(reversed(range(cb)))
        return [functools.partial(step, c) for c in order] + [finish]

    def chunks(ref, lanes):
        return ref[:, lanes].reshape(cb, CHUNK, HEAD)

    for h0 in range(0, heads, batch_heads):
        hs = list(range(h0, h0 + batch_heads))
        for z in range(2):
            q_ref, k_ref, v_ref, gates_ref = (qf_ref, kf_ref, vf_ref, gf_ref) if z == 0 else (qb_ref, kb_ref, vb_ref, gb_ref)
            ksl = [slice((h // rep) * HEAD, (h // rep + 1) * HEAD) for h in hs]
            gate0 = z * 2 * n_vheads + grp * heads
            mq, na, egl = _chunk_factors(
                jnp.concatenate([chunks(q_ref, sl) for sl in ksl], axis=0),
                jnp.concatenate([chunks(k_ref, sl) for sl in ksl], axis=0),
                jnp.concatenate([chunks(v_ref, slice(h * HEAD, (h + 1) * HEAD)) for h in hs], axis=0),
                jnp.concatenate([gates_ref[:, pl.ds(gate0 + h, 1), :] for h in hs], axis=0),
                jnp.concatenate([gates_ref[:, pl.ds(gate0 + n_vheads + h, 1), :] for h in hs], axis=0),
                row, col, reverse=(z == 1), tick=tick)
            while pending:
                tick()
            pending.extend(recurrence(z, hs, mq, na, egl))
    while pending:
        tick()

    @pl.when(t == n_t - 1)
    def _():
        sf_ref[...] = s_scr[...]
        for hh in range(heads):
            o = o_scr[hh, 0] + o_scr[hh, 1]
            ms = jnp.mean(o * o, axis=-1, keepdims=True)
            zz = z_ref[:, hh * HEAD:(hh + 1) * HEAD].astype(F32)
            y = (o * lax.rsqrt(ms + EPS) * nw_ref[...]) * _silu(zz)
            y_ref[:, hh * HEAD:(hh + 1) * HEAD] = y.astype(y_ref.dtype)


def _gdn_core(qkv_conv, gates, s0, z_gate, norm_w, n_seq, seq_len, blk0, chunk0, key_dim, n_vheads, heads,
              batch_heads):
    tb = min(seq_len, PREP_ROWS)
    n_tb = seq_len // tb
    cb = tb // CHUNK
    nk = key_dim // HEAD
    rep = n_vheads // nk
    kw = max(1, heads // rep)
    assert (heads == 1 or heads % rep == 0) and heads % batch_heads == 0
    assert n_vheads % heads == 0 and nk % kw == 0 and (2 * nk) % heads == 0 and chunk0 == blk0 * n_tb * cb

    def fwd(s, g, t):
        return (blk0 + s) * n_tb + t

    def bwd(s, g, t):
        return (blk0 + s) * n_tb + n_tb - 1 - t

    def specs(tblk):
        return [
            pl.BlockSpec((tb, kw * HEAD), lambda s, g, t: (tblk(s, g, t), (g * heads) // (rep * kw))),
            pl.BlockSpec((tb, kw * HEAD), lambda s, g, t: (tblk(s, g, t), nk // kw + (g * heads) // (rep * kw))),
            pl.BlockSpec((tb, heads * HEAD), lambda s, g, t: (tblk(s, g, t), 2 * nk // heads + g)),
        ]

    def gate_spec(tblk):
        return pl.BlockSpec((cb, CHUNK, CHUNK), lambda s, g, t: (tblk(s, g, t), 0, 0))

    state_spec = pl.BlockSpec((None, heads, 2, HEAD, HEAD), lambda s, g, t: (s, g, 0, 0, 0))
    return pl.pallas_call(
        functools.partial(_gdn_core_kernel, n_vheads=n_vheads, rep=rep, batch_heads=batch_heads),
        out_shape=(jax.ShapeDtypeStruct((n_seq * seq_len, n_vheads * HEAD), BF16),
                   jax.ShapeDtypeStruct((n_seq, n_vheads, 2, HEAD, HEAD), F32)),
        grid=(n_seq, n_vheads // heads, n_tb),
        in_specs=specs(fwd) + specs(bwd) + [gate_spec(fwd), gate_spec(bwd), state_spec,
                  pl.BlockSpec((seq_len, heads * HEAD), lambda s, g, t: (blk0 + s, g)),
                  pl.BlockSpec((1, HEAD), lambda s, g, t: (0, 0))],
        out_specs=(pl.BlockSpec((seq_len, heads * HEAD), lambda s, g, t: (s, g)), state_spec),
        scratch_shapes=[pltpu.VMEM((heads, 2, HEAD, HEAD), F32), pltpu.VMEM((heads, 2, seq_len, HEAD), F32)],
        compiler_params=_params("parallel", "parallel", "arbitrary"),
        name=f"gdn_core_{seq_len}",
    )(qkv_conv, qkv_conv, qkv_conv, qkv_conv, qkv_conv, qkv_conv, gates, gates, s0, z_gate, norm_w.reshape(1, HEAD))


def _gdn_mixer(tok, qkv_conv, z_gate, ba, key_dim, a_log, dt_bias, norm_w):
    n_vheads = z_gate.shape[1] // HEAD
    batch, seq, ctx = tok.batch, tok.seq, tok.ctx
    gates = _gdn_gates(ba, a_log, dt_bias)

    def run(n_seq, seq_len, blk0, s0, heads, batch_heads):
        return _gdn_core(qkv_conv, gates, s0, z_gate, norm_w, n_seq, seq_len, blk0, blk0 * seq_len // CHUNK,
                         key_dim, n_vheads, heads, batch_heads)

    zero_state = jnp.zeros((batch, n_vheads, 2, HEAD, HEAD), F32)
    ctx_heads = min(CTX_HEADS_PER_STEP, n_vheads)
    y_ctx, s_ctx = run(batch, ctx, tok.n_lat // ctx, zero_state, ctx_heads, ctx_heads)
    y_lat, _ = run(batch, seq, 0, s_ctx, LAT_HEADS_PER_STEP, LAT_HEADS_PER_STEP)
    return y_lat, y_ctx


def kernel(x, c, ctx, c_ctx, w_ada, b_ada, norm1_w, norm2_w, na_w_qkv, na_w_o, na_q_gain, na_k_gain, na_rpb, gdn_w_in, gdn_conv_w, gdn_w_ba, gdn_a_log, gdn_dt_bias, gdn_norm_w, gdn_w_o, ffn_w1, ffn_w3, ffn_w2):
    batch, seq, d = x.shape
    n_ctx = ctx.shape[1]
    depth = w_ada.shape[0]
    tok = _Tokens(batch, seq, n_ctx)
    assert batch + 1 <= 8

    x_all = jnp.concatenate([x.reshape(batch * seq, d), ctx.reshape(batch * n_ctx, d)], axis=0)
    cond = jnp.concatenate([c, c_ctx[None, :], jnp.zeros((8 - batch - 1, d), F32)], axis=0)
    mods = _ada(cond, w_ada, b_ada)

    na_qkv16, na_o16 = na_w_qkv.astype(BF16), na_w_o.astype(BF16)
    gdn_in16, gdn_o16 = gdn_w_in.astype(BF16), gdn_w_o.astype(BF16)
    w1_16, w3_16, w2_16 = ffn_w1.astype(BF16), ffn_w3.astype(BF16), ffn_w2.astype(BF16)

    for i in range(depth):
        update_ctx = i < depth - 1
        n_tiles = tok.all_tiles if update_ctx else tok.lat_tiles
        mod = mods[i].reshape(8, 1, 6 * d)
        j = i // 2
        if i % 2 == 0:
            qkv = _norm_mod_matmul(tok, x_all, norm1_w[i], mod, na_qkv16, j, BF16, "na_qkv",
                                   na_gains=(na_q_gain[j], na_k_gain[j]))
            tables = _na_bias_tables(na_rpb[j], seq // GRID_W)
            mix_lat, mix_ctx = _na_attention(tok, qkv, tables, d)
            w_o = na_o16
        else:
            conv_dim = gdn_conv_w.shape[2]
            key_dim = (2 * conv_dim - gdn_w_in.shape[2]) // 2
            qkv_conv, z_gate = _gdn_in(tok, x_all, norm1_w[i], mod, gdn_in16, j, gdn_conv_w[j], key_dim)
            w_ba = jnp.concatenate([gdn_w_ba[j, 0], gdn_w_ba[j, 1]], axis=1).astype(BF16)
            w_ba = jnp.pad(w_ba, ((0, 0), (0, CHUNK - w_ba.shape[1])))
            ba = _norm_mod_matmul(tok, x_all, norm1_w[i], mod, w_ba[None], 0, F32, "gdn_ba")
            mix_lat, mix_ctx = _gdn_mixer(tok, qkv_conv, z_gate, ba, key_dim, gdn_a_log[j], gdn_dt_bias[j],
                                          gdn_norm_w[j])
            w_o = gdn_o16
        x_all = _matmul_residual(tok, 0, mix_lat, w_o, j, x_all, mod, 2, "mixer_out")
        if update_ctx:
            x_all = _matmul_residual(tok, tok.lat_tiles, mix_ctx, w_o, j, x_all, mod, 2, "mixer_out_ctx")
        x_all = _ffn(tok, n_tiles, x_all, norm2_w[i], mod, w1_16, w3_16, w2_16, i, in_place=update_ctx)
    return x_all.reshape(batch, seq, d)
```

```python
import functools

import jax
import jax.numpy as jnp
import numpy as np
from jax import lax
from jax.experimental import pallas as pl
from jax.experimental.pallas import tpu as pltpu

HEAD = 128
GRID_W = 64
WIN_H = 8
WIN_W = 16
ROW_GROUP = 4
NA_GROUP_UNROLL = 2
KEY_ROWS = ROW_GROUP + WIN_H
CHUNK = 128
CONV_TAPS = 4
CONV_PAD = 8
EPS = 1e-6
NEG_INF = -1e30
NA_SCALE = HEAD ** -0.5
LOG2E = 1.4426950408889634
GDN_SCALE = HEAD ** -0.5
TOKEN_TILE = 512
NORM_SLAB = 256
PREP_ROWS = 1024
CTX_HEADS_PER_STEP = 8
LAT_HEADS_PER_STEP = 2
VMEM_LIMIT = 48 * 1024 * 1024

F32 = jnp.float32
BF16 = jnp.bfloat16


def _params(*sem):
    return pltpu.CompilerParams(dimension_semantics=sem, vmem_limit_bytes=VMEM_LIMIT)


def _pick(n, cands):
    for c in cands:
        if n % c == 0:
            return c
    raise ValueError(f"no tile in {cands} divides {n}")


def _silu(x):
    return x / (1.0 + jnp.exp(-x))


def _dot(a, b):
    return jnp.dot(a, b, preferred_element_type=F32)


def _dot_nt(a, b):
    return lax.dot_general(a, b, (((1,), (1,)), ((), ())), preferred_element_type=F32)


def _dot_hi(a, b):
    return jnp.dot(a, b, preferred_element_type=F32, precision=lax.Precision.HIGHEST)


def _ada_kernel(c_ref, w_ref, b_ref, o_ref):
    s = _silu(c_ref[...]).astype(BF16)
    o_ref[...] = _dot(s, w_ref[...].astype(BF16)) + b_ref[...]


def _ada(cond, w_ada, b_ada):
    depth, d, n = w_ada.shape
    tn = _pick(n, (1024, 512, 256, 128))
    return pl.pallas_call(
        _ada_kernel,
        out_shape=jax.ShapeDtypeStruct((depth, 8, n), F32),
        grid=(depth, n // tn),
        in_specs=[
            pl.BlockSpec((8, d), lambda l, j: (0, 0)),
            pl.BlockSpec((None, d, tn), lambda l, j: (l, 0, j)),
            pl.BlockSpec((None, 1, tn), lambda l, j: (l, 0, j)),
        ],
        out_specs=pl.BlockSpec((None, 8, tn), lambda l, j: (l, 0, j)),
        compiler_params=_params("parallel", "parallel"),
        name="ada_mod",
    )(cond, w_ada, b_ada.reshape(depth, 1, n))


class _Tokens:
    def __init__(self, batch, seq, ctx):
        self.batch, self.seq, self.ctx = batch, seq, ctx
        self.n_lat = batch * seq
        self.n_all = self.n_lat + batch * ctx
        self.tm = TOKEN_TILE
        assert seq % self.tm == 0 and (batch * ctx) % self.tm == 0
        self.lat_tiles = self.n_lat // self.tm
        self.all_tiles = self.n_all // self.tm
        self.tiles_per_batch = seq // self.tm

    def mod_row(self, t):
        return jnp.where(t < self.lat_tiles, t // self.tiles_per_batch, self.batch)


def _mod_spec(tok, d, chunk):
    return pl.BlockSpec((None, 1, d), lambda m, j: (tok.mod_row(m), 0, chunk))


def _norm_mod(x, nw, sh, sc):
    ms = jnp.mean(x * x, axis=-1, keepdims=True)
    return (x * lax.rsqrt(ms + EPS) * nw) * (1.0 + sc) + sh


def _na_qkv_kernel(x_ref, nw_ref, sh_ref, sc_ref, w_ref, qg_ref, kg_ref, o_ref, h_scr, *, tiles_per_kind):
    n = pl.program_id(1)

    @pl.when(n == 0)
    def _():
        h_scr[...] = _norm_mod(x_ref[...], nw_ref[...], sh_ref[...], sc_ref[...]).astype(BF16)

    kind = n // tiles_per_kind

    @pl.when(kind == 2)
    def _():
        o_ref[...] = _dot(h_scr[...], w_ref[...]).astype(o_ref.dtype)

    @pl.when(kind < 2)
    def _():
        gain = jnp.where(kind == 0, qg_ref[...] * (NA_SCALE * LOG2E), kg_ref[...])
        h = h_scr[...]
        n_slabs = o_ref.shape[1] // NORM_SLAB
        nxt = _dot(h, w_ref[:, :NORM_SLAB])
        for j in range(n_slabs):
            acc = nxt
            if j + 1 < n_slabs:
                nxt = _dot(h, w_ref[:, (j + 1) * NORM_SLAB:(j + 2) * NORM_SLAB])
            for i in range(NORM_SLAB // HEAD):
                a = acc[:, i * HEAD:(i + 1) * HEAD]
                ms = jnp.mean(a * a, axis=-1, keepdims=True)
                lo = j * NORM_SLAB + i * HEAD
                o_ref[:, lo:lo + HEAD] = (a * lax.rsqrt(ms + EPS) * gain).astype(o_ref.dtype)


def _na_qkv(tok, x_all, norm_w, mod, w, layer, q_gain, k_gain):
    _, d, n = w.shape
    tm = tok.tm
    tn = _pick(d, (1024, 512, 256, 128))
    return pl.pallas_call(
        functools.partial(_na_qkv_kernel, tiles_per_kind=d // tn),
        out_shape=jax.ShapeDtypeStruct((tok.n_all, n), BF16),
        grid=(tok.all_tiles, n // tn),
        in_specs=[
            pl.BlockSpec((tm, d), lambda m, j: (m, 0)),
            pl.BlockSpec((1, d), lambda m, j: (0, 0)),
            _mod_spec(tok, d, 0),
            _mod_spec(tok, d, 1),
            pl.BlockSpec((None, d, tn), lambda m, j: (layer, 0, j)),
            pl.BlockSpec((1, HEAD), lambda m, j: (0, 0)),
            pl.BlockSpec((1, HEAD), lambda m, j: (0, 0)),
        ],
        out_specs=pl.BlockSpec((tm, tn), lambda m, j: (m, j)),
        scratch_shapes=[pltpu.VMEM((tm, d), BF16)],
        compiler_params=_params("parallel", "arbitrary"),
        name="na_qkv",
    )(x_all, norm_w.reshape(1, d), mod, mod, w, q_gain.reshape(1, HEAD), k_gain.reshape(1, HEAD))


def _mm_res_kernel(a_ref, w_ref, x_ref, g_ref, o_ref):
    o_ref[...] = x_ref[...] + g_ref[...] * _dot(a_ref[...], w_ref[...])


def _matmul_residual(tok, tile0, a, w, layer, x_all, mod, gate_chunk, name):
    _, k, d = w.shape
    tm = tok.tm
    tn = _pick(d, (1024, 512, 256, 128))
    n_tiles = a.shape[0] // tm
    return pl.pallas_call(
        _mm_res_kernel,
        out_shape=jax.ShapeDtypeStruct(x_all.shape, F32),
        grid=(n_tiles, d // tn),
        in_specs=[
            pl.BlockSpec((tm, k), lambda m, j: (m, 0)),
            pl.BlockSpec((None, k, tn), lambda m, j: (layer, 0, j)),
            pl.BlockSpec((tm, tn), lambda m, j: (tile0 + m, j)),
            pl.BlockSpec((None, 1, tn), lambda m, j: (tok.mod_row(tile0 + m), 0, gate_chunk * (d // tn) + j)),
        ],
        out_specs=pl.BlockSpec((tm, tn), lambda m, j: (tile0 + m, j)),
        input_output_aliases={2: 0},
        compiler_params=_params("parallel", "parallel"),
        name=name,
    )(a, w, x_all, mod)


def _ffn_kernel(x_ref, nw_ref, sh_ref, sc_ref, g_ref, w1_ref, w3_ref, w2_ref, o_ref, h_scr, acc_scr):
    f = pl.program_id(1)

    @pl.when(f == 0)
    def _():
        h_scr[...] = _norm_mod(x_ref[...], nw_ref[...], sh_ref[...], sc_ref[...]).astype(BF16)
        acc_scr[...] = jnp.zeros_like(acc_scr)

    h = h_scr[...]
    u = _silu(_dot(h, w1_ref[...])) * _dot(h, w3_ref[...])
    acc_scr[...] += _dot(u.astype(BF16), w2_ref[...])

    @pl.when(f == pl.num_programs(1) - 1)
    def _():
        o_ref[...] = x_ref[...] + g_ref[...] * acc_scr[...]


def _ffn(tok, n_tiles, x_all, norm_w, mod, w1, w3, w2, layer, in_place):
    _, d, dff = w1.shape
    tm = tok.tm
    tf = _pick(dff, (512, 256, 128))
    return pl.pallas_call(
        _ffn_kernel,
        out_shape=jax.ShapeDtypeStruct(x_all.shape if in_place else (n_tiles * tm, d), F32),
        grid=(n_tiles, dff // tf),
        in_specs=[
            pl.BlockSpec((tm, d), lambda m, f: (m, 0)),
            pl.BlockSpec((1, d), lambda m, f: (0, 0)),
            _mod_spec(tok, d, 3),
            _mod_spec(tok, d, 4),
            _mod_spec(tok, d, 5),
            pl.BlockSpec((None, d, tf), lambda m, f: (layer, 0, f)),
            pl.BlockSpec((None, d, tf), lambda m, f: (layer, 0, f)),
            pl.BlockSpec((None, tf, d), lambda m, f: (layer, f, 0)),
        ],
        out_specs=pl.BlockSpec((tm, d), lambda m, f: (m, 0)),
        scratch_shapes=[pltpu.VMEM((tm, d), BF16), pltpu.VMEM((tm, d), F32)],
        input_output_aliases={0: 0} if in_place else {},
        compiler_params=_params("parallel", "arbitrary"),
        name="ffn",
    )(x_all, norm_w.reshape(1, d), mod, mod, mod, w1, w3, w2)


def _na_bias_tables(rpb, rows):
    kh = min(WIN_H, rows)
    cols = np.arange(GRID_W)
    col_start = np.clip(cols - WIN_W // 2, 0, GRID_W - WIN_W)
    in_win = (cols[None, :] >= col_start[:, None]) & (cols[None, :] < col_start[:, None] + WIN_W)
    dc = np.clip(cols[None, :] - cols[:, None], 1 - WIN_W, WIN_W - 1) + WIN_W - 1
    off = np.zeros((3, ROW_GROUP, KEY_ROWS), np.int32)
    ok = np.zeros((3, ROW_GROUP, KEY_ROWS), bool)
    for t, r0 in enumerate((0, ROW_GROUP, rows - ROW_GROUP)):
        start = min(max(r0 - WIN_H // 2, 0), rows - KEY_ROWS)
        for i in range(ROW_GROUP):
            r = r0 + i
            rs = min(max(r - kh // 2, 0), rows - kh)
            for m in range(KEY_ROWS):
                j = start + m - rs
                ok[t, i, m] = 0 <= j < kh
                off[t, i, m] = min(max(rs - r + j + WIN_H - 1, 0), 2 * WIN_H - 2)
    heads = rpb.shape[0]
    onehot = ((dc[None] == np.arange(2 * WIN_W - 1)[:, None, None]) & in_win[None]).astype(np.float32)
    by_col = jnp.einsum("hod,dqk->hqok", rpb.astype(F32), onehot, precision=lax.Precision.HIGHEST)
    by_col = (by_col + np.where(in_win, 0.0, NEG_INF).astype(np.float32)[:, None, :]) * LOG2E
    by_col = by_col.reshape(heads, GRID_W, (2 * WIN_H - 1) * GRID_W)
    pieces = []
    for t in range(3):
        for i in range(ROW_GROUP):
            valid = np.nonzero(ok[t, i])[0]
            m0, m1 = int(valid[0]), int(valid[-1]) + 1
            o0 = int(off[t, i, m0])
            assert np.array_equal(valid, np.arange(m0, m1))
            assert np.array_equal(off[t, i, m0:m1], o0 + np.arange(m1 - m0))
            pieces.append(jnp.pad(by_col[:, :, o0 * GRID_W:(o0 + m1 - m0) * GRID_W],
                                  ((0, 0), (0, 0), (m0 * GRID_W, (KEY_ROWS - m1) * GRID_W)),
                                  constant_values=NEG_INF * LOG2E))
    return jnp.stack(pieces, axis=1).reshape(heads, 3, ROW_GROUP * GRID_W, KEY_ROWS * GRID_W)


def _softmax_pv(parts):
    m = functools.reduce(jnp.maximum, [jnp.max(s, axis=-1, keepdims=True) for s, _ in parts])
    ps = [jnp.exp2(s - m) for s, _ in parts]
    den = functools.reduce(jnp.add, [jnp.sum(p, axis=-1, keepdims=True) for p in ps])
    num = functools.reduce(jnp.add, [_dot(p.astype(BF16), v) for p, (_, v) in zip(ps, parts)])
    return num / den


def _na_kernel(q_ref, k_ref, v_ref, qc_ref, kc_ref, vc_ref, bias_ref, o_ref, oc_ref, *, rows):
    kc = kc_ref[...]
    vc = vc_ref[...]
    n_groups = rows // ROW_GROUP
    gq = ROW_GROUP * GRID_W
    gk = KEY_ROWS * GRID_W

    def group(gi, carry):
        r0 = gi * ROW_GROUP
        start = jnp.clip(r0 - WIN_H // 2, 0, rows - KEY_ROWS)
        table = jnp.where(gi == 0, 0, jnp.where(gi == n_groups - 1, 2, 1))
        q = q_ref[pl.ds(pl.multiple_of(r0 * GRID_W, gq), gq), :]
        koff = pl.multiple_of(start * GRID_W, GRID_W)
        ks = k_ref[pl.ds(koff, gk), :]
        vs = v_ref[pl.ds(koff, gk), :]
        s_loc = _dot_nt(q, ks) + bias_ref[table]
        s_ctx = _dot_nt(q, kc)
        o = _softmax_pv([(s_loc, vs), (s_ctx, vc)])
        o_ref[pl.ds(pl.multiple_of(r0 * GRID_W, gq), gq), :] = o.astype(o_ref.dtype)
        return carry

    lax.fori_loop(0, n_groups, group, 0, unroll=NA_GROUP_UNROLL)
    s_cc = _dot_nt(qc_ref[...], kc)
    oc_ref[...] = _softmax_pv([(s_cc, vc)]).astype(oc_ref.dtype)


def _na_attention(tok, qkv, bias_tables, d):
    heads = d // HEAD
    seq, ctx, batch = tok.seq, tok.ctx, tok.batch
    rows = seq // GRID_W
    assert rows % ROW_GROUP == 0 and rows >= KEY_ROWS
    ctx_blk0 = tok.n_lat // ctx
    gq, gk = ROW_GROUP * GRID_W, KEY_ROWS * GRID_W

    def lat(kind):
        return pl.BlockSpec((seq, HEAD), lambda h, b: (b, kind * heads + h))

    def cx(kind):
        return pl.BlockSpec((ctx, HEAD), lambda h, b: (ctx_blk0 + b, kind * heads + h))

    o_lat, o_ctx = pl.pallas_call(
        functools.partial(_na_kernel, rows=rows),
        out_shape=(jax.ShapeDtypeStruct((tok.n_lat, d), BF16),
                   jax.ShapeDtypeStruct((batch * ctx, d), BF16)),
        grid=(heads, batch),
        in_specs=[lat(0), lat(1), lat(2), cx(0), cx(1), cx(2),
                  pl.BlockSpec((None, 3, gq, gk), lambda h, b: (h, 0, 0, 0))],
        out_specs=(pl.BlockSpec((seq, HEAD), lambda h, b: (b, h)),
                   pl.BlockSpec((ctx, HEAD), lambda h, b: (b, h))),
        compiler_params=_params("parallel", "parallel"),
        name="na_attention",
    )(qkv, qkv, qkv, qkv, qkv, qkv, bias_tables)
    return o_lat, o_ctx


def _gdn_in_kernel(x_ref, xp_ref, xn_ref, nw_ref, sh_ref, sc_ref, w_ref, cw_ref, wba_ref, qkv_ref, z_ref, ba_ref,
                   h_scr, acc_scr, *, tok, n_key_tiles, n_conv_tiles):
    m = pl.program_id(0)
    n = pl.program_id(1)
    tm = x_ref.shape[0]
    tn = w_ref.shape[1]

    @pl.when(n == 0)
    def _():
        nw, sh, sc = nw_ref[...], sh_ref[...], sc_ref[...]
        h_scr[pl.ds(0, CONV_PAD), :] = _norm_mod(xp_ref[...], nw, sh, sc).astype(BF16)
        h_scr[pl.ds(CONV_PAD, tm), :] = _norm_mod(x_ref[...], nw, sh, sc).astype(BF16)
        h_scr[pl.ds(CONV_PAD + tm, CONV_PAD), :] = _norm_mod(xn_ref[...], nw, sh, sc).astype(BF16)
        ba_ref[...] = _dot(h_scr[pl.ds(CONV_PAD, tm), :], wba_ref[...])

    @pl.when(n >= n_conv_tiles)
    def _():
        z_ref[...] = _dot(h_scr[pl.ds(CONV_PAD, tm), :], w_ref[...]).astype(z_ref.dtype)

    def conv_tile(normalise):
        seq_len = jnp.where(m < tok.lat_tiles, tok.seq, tok.ctx)
        pos = (m * tm + lax.broadcasted_iota(jnp.int32, (tm, 1), 0)) & (seq_len - 1)
        ok = [pos >= 1, None, pos <= seq_len - 2, pos <= seq_len - 3]
        scale = jnp.where(n < n_key_tiles, GDN_SCALE, 1.0)
        h = h_scr[...]
        n_slabs = tn // NORM_SLAB
        acc_scr[0] = _dot(h, w_ref[:, :NORM_SLAB])
        for j in range(n_slabs):
            cols = slice(j * NORM_SLAB, (j + 1) * NORM_SLAB)
            if j + 1 < n_slabs:
                acc_scr[(j + 1) % 2] = _dot(h, w_ref[:, (j + 1) * NORM_SLAB:(j + 2) * NORM_SLAB])
            cw = cw_ref[:, cols]
            y = None
            for tap in range(CONV_TAPS):
                term = acc_scr[j % 2, pl.ds(CONV_PAD - (CONV_TAPS - 1) // 2 + tap, tm), :] * cw[tap:tap + 1, :]
                if ok[tap] is not None:
                    term = jnp.where(ok[tap], term, 0.0)
                y = term if y is None else y + term
            y = _silu(y)
            for i in range(NORM_SLAB // HEAD):
                yh = y[:, i * HEAD:(i + 1) * HEAD]
                if normalise:
                    yh = yh * (lax.rsqrt(jnp.sum(yh * yh, axis=-1, keepdims=True) + EPS) * scale)
                lo = j * NORM_SLAB + i * HEAD
                qkv_ref[:, lo:lo + HEAD] = yh

    pl.when(n < 2 * n_key_tiles)(functools.partial(conv_tile, True))
    pl.when((n >= 2 * n_key_tiles) & (n < n_conv_tiles))(functools.partial(conv_tile, False))


def _gdn_in(tok, x_all, norm_w, mod, w, layer, conv_w, w_ba, key_dim):
    _, d, n = w.shape
    conv_dim = conv_w.shape[1]
    tm = tok.tm
    tn = _pick(np.gcd(key_dim, n - conv_dim), (1024, 512, 256))
    n_conv_tiles = conv_dim // tn
    halo_blocks = tm // CONV_PAD
    last_halo = tok.n_all // CONV_PAD - 1
    assert tok.seq & (tok.seq - 1) == 0 and tok.ctx & (tok.ctx - 1) == 0
    kern = functools.partial(_gdn_in_kernel, tok=tok, n_key_tiles=key_dim // tn, n_conv_tiles=n_conv_tiles)
    return pl.pallas_call(
        kern,
        out_shape=(jax.ShapeDtypeStruct((tok.n_all, conv_dim), F32),
                   jax.ShapeDtypeStruct((tok.n_all, n - conv_dim), BF16),
                   jax.ShapeDtypeStruct((tok.n_all, w_ba.shape[1]), F32)),
        grid=(tok.all_tiles, n // tn),
        in_specs=[
            pl.BlockSpec((tm, d), lambda m, j: (m, 0)),
            pl.BlockSpec((CONV_PAD, d), lambda m, j: (jnp.maximum(m * halo_blocks - 1, 0), 0)),
            pl.BlockSpec((CONV_PAD, d), lambda m, j: (jnp.minimum((m + 1) * halo_blocks, last_halo), 0)),
            pl.BlockSpec((1, d), lambda m, j: (0, 0)),
            _mod_spec(tok, d, 0),
            _mod_spec(tok, d, 1),
            pl.BlockSpec((None, d, tn), lambda m, j: (layer, 0, j)),
            pl.BlockSpec((CONV_TAPS, tn), lambda m, j: (0, jnp.minimum(j, n_conv_tiles - 1))),
            pl.BlockSpec(w_ba.shape, lambda m, j: (0, 0)),
        ],
        out_specs=(pl.BlockSpec((tm, tn), lambda m, j: (m, jnp.minimum(j, n_conv_tiles - 1))),
                   pl.BlockSpec((tm, tn), lambda m, j: (m, jnp.maximum(j - n_conv_tiles, 0))),
                   pl.BlockSpec((tm, w_ba.shape[1]), lambda m, j: (m, 0))),
        scratch_shapes=[pltpu.VMEM((tm + 2 * CONV_PAD, d), BF16),
                        pltpu.VMEM((2, tm + 2 * CONV_PAD, NORM_SLAB), F32)],
        compiler_params=_params("parallel", "arbitrary"),
        name="gdn_in",
    )(x_all, x_all, x_all, norm_w.reshape(1, d), mod, mod, w, conv_w, w_ba)


def _gates_kernel(ba_ref, alog_ref, dt_ref, o_ref, *, n_chunks, n_vheads):
    row = lax.broadcasted_iota(jnp.int32, (CHUNK, CHUNK), 0)
    col = lax.broadcasted_iota(jnp.int32, (CHUNK, CHUNK), 1)
    prefix = (row <= col).astype(F32)
    suffix = (row >= col).astype(F32)
    is_decay = (row % (2 * n_vheads)) >= n_vheads
    neg_a = -jnp.exp(alog_ref[...])
    for c in range(n_chunks):
        bt = ba_ref[pl.ds(c * CHUNK, CHUNK), :].T
        beta = 1.0 / (1.0 + jnp.exp(-bt))
        x = bt + dt_ref[...]
        g = neg_a * (jnp.maximum(x, 0.0) + jnp.log1p(jnp.exp(-jnp.abs(x))))
        cum = jnp.where(row < 2 * n_vheads, _dot_hi(g, prefix), _dot_hi(g, suffix))
        o_ref[c] = jnp.where(is_decay, cum, beta)


def _gdn_gates(ba, a_log, dt_bias):
    n_tok, n_col = ba.shape
    n_vheads = a_log.shape[1]
    assert n_col == CHUNK and 4 * n_vheads <= n_col
    zeros = jnp.zeros_like(a_log)
    pad = ((0, n_col - 4 * n_vheads), (0, 0))
    alog_col = jnp.pad(jnp.concatenate([zeros, a_log], axis=1).reshape(4 * n_vheads, 1), pad)
    dt_col = jnp.pad(jnp.concatenate([zeros, dt_bias], axis=1).reshape(4 * n_vheads, 1), pad)
    per_step = 4
    assert (n_tok // CHUNK) % per_step == 0
    return pl.pallas_call(
        functools.partial(_gates_kernel, n_chunks=per_step, n_vheads=n_vheads),
        out_shape=jax.ShapeDtypeStruct((n_tok // CHUNK, n_col, CHUNK), F32),
        grid=(n_tok // (CHUNK * per_step),),
        in_specs=[pl.BlockSpec((CHUNK * per_step, n_col), lambda i: (i, 0)),
                  pl.BlockSpec((n_col, 1), lambda i: (0, 0)),
                  pl.BlockSpec((n_col, 1), lambda i: (0, 0))],
        out_specs=pl.BlockSpec((per_step, n_col, CHUNK), lambda i: (i, 0, 0)),
        compiler_params=_params("parallel"),
        name="gdn_gates",
    )(ba, alog_col, dt_col)


def _bdot(a, b):
    return jnp.einsum("pmk,pkn->pmn", a, b, preferred_element_type=F32)


def _bdot_nt(a, b):
    return jnp.einsum("pmk,pnk->pmn", a, b, preferred_element_type=F32)


def _unit_tri_inverse(l, row, col, tick):
    n = l.shape[-1]
    b = 2
    x = (row == col).astype(F32) - jnp.where(row // b == col // b, l, 0.0)
    while b < n:
        pair = (row // (2 * b) == col // (2 * b)) & (row // b != col // b)
        c16 = jnp.where(pair, l, 0.0).astype(BF16)
        x16 = x.astype(BF16)
        x = x - _bdot(x16, _bdot(c16, x16).astype(BF16))
        tick()
        b *= 2
    return x


def _chunk_factors(q, k, v, beta_r, g_r, row, col, reverse, tick):
    p = q.shape[0]

    def per_row(r):
        return jnp.stack([jnp.broadcast_to(r[c], (CHUNK, CHUNK)).T for c in range(p)], axis=0)

    incl = (row <= col) if reverse else (row >= col)
    strict = (row < col) if reverse else (row > col)
    beta_c = per_row(beta_r)
    g_c = per_row(g_r)
    decay = jnp.where(incl, jnp.exp(jnp.where(incl, g_c - g_r, 0.0)), 0.0)
    k16 = k.astype(BF16)
    k_beta = k * beta_c
    v_beta = v * beta_c
    l_mat = jnp.where(strict, _bdot_nt(k_beta.astype(BF16), k16) * decay, 0.0)
    tick()
    t_inv = _unit_tri_inverse(l_mat, row, col, tick)
    exp_g = jnp.exp(g_c)
    rhs = jnp.concatenate([k_beta * exp_g, v_beta], axis=2).astype(BF16)
    wu16 = _bdot(t_inv.astype(BF16), rhs).astype(BF16)
    tick()
    last = 0 if reverse else CHUNK - 1
    g_last = g_c[:, last:last + 1, :]
    kd = k * jnp.exp(g_last - g_c)
    kd_t = jnp.stack([kd[c].T for c in range(p)], axis=0).astype(BF16)
    a16 = jnp.where(incl, _bdot_nt(q.astype(BF16), k16) * decay, 0.0).astype(BF16)
    r = _bdot(jnp.concatenate([kd_t, a16], axis=1), wu16)
    tick()
    qe = q * exp_g - r[:, CHUNK:, :HEAD]
    mq = jnp.concatenate([-r[:, :CHUNK, :HEAD], qe], axis=1).astype(BF16)
    return mq, r[:, :, HEAD:], jnp.exp(g_last)


def _gdn_core_kernel(qf_ref, kf_ref, vf_ref, qb_ref, kb_ref, vb_ref, gf_ref, gb_ref, s0_ref, z_ref, nw_ref,
                     y_ref, sf_ref, s_scr, o_scr, *, n_vheads, rep, batch_heads):
    grp = pl.program_id(1)
    t = pl.program_id(2)
    n_t = pl.num_programs(2)
    tb = qf_ref.shape[0]
    cb = tb // CHUNK
    heads = s_scr.shape[0]
    row = lax.broadcasted_iota(jnp.int32, (CHUNK, CHUNK), 0)
    col = lax.broadcasted_iota(jnp.int32, (CHUNK, CHUNK), 1)

    @pl.when(t == 0)
    def _():
        s_scr[...] = s0_ref[...]

    pending = []

    def tick():
        if pending:
            pending.pop(0)()

    def recurrence(z, hs, mq, na, egl):
        state = {h: s_scr[h, z] for h in hs}
        blk = t if z == 0 else n_t - 1 - t

        def step(c):
            for j, h in enumerate(hs):
                s = state[h]
                r = _dot(mq[j * cb + c], s.astype(BF16)) + na[j * cb + c]
                o_scr[h, z, pl.ds(pl.multiple_of(blk * tb + c * CHUNK, CHUNK), CHUNK), :] = r[CHUNK:]
                state[h] = s * egl[j * cb + c] + r[:CHUNK]

        def finish():
            for h in hs:
                s_scr[h, z] = state[h]

        order = list(range(cb)) if z == 0 else list(reversed(range(cb)))
        return [functools.partial(step, c) for c in order] + [finish]

    def chunks(ref, lanes):
        return ref[:, lanes].reshape(cb, CHUNK, HEAD)

    for h0 in range(0, heads, batch_heads):
        hs = list(range(h0, h0 + batch_heads))
        for z in range(2):
            q_ref, k_ref, v_ref, gates_ref = (qf_ref, kf_ref, vf_ref, gf_ref) if z == 0 else (qb_ref, kb_ref, vb_ref, gb_ref)
            ksl = [slice((h // rep) * HEAD, (h // rep + 1) * HEAD) for h in hs]
            gate0 = z * 2 * n_vheads + grp * heads
            mq, na, egl = _chunk_factors(
                jnp.concatenate([chunks(q_ref, sl) for sl in ksl], axis=0),
                jnp.concatenate([chunks(k_ref, sl) for sl in ksl], axis=0),
                jnp.concatenate([chunks(v_ref, slice(h * HEAD, (h + 1) * HEAD)) for h in hs], axis=0),
                jnp.concatenate([gates_ref[:, pl.ds(gate0 + h, 1), :] for h in hs], axis=0),
                jnp.concatenate([gates_ref[:, pl.ds(gate0 + n_vheads + h, 1), :] for h in hs], axis=0),
                row, col, reverse=(z == 1), tick=tick)
            while pending:
                tick()
            pending.extend(recurrence(z, hs, mq, na, egl))
    while pending:
        tick()

    @pl.when(t == n_t - 1)
    def _():
        sf_ref[...] = s_scr[...]
        for hh in range(heads):
            o = o_scr[hh, 0] + o_scr[hh, 1]
            ms = jnp.mean(o * o, axis=-1, keepdims=True)
            zz = z_ref[:, hh * HEAD:(hh + 1) * HEAD].astype(F32)
            y = (o * lax.rsqrt(ms + EPS) * nw_ref[...]) * _silu(zz)
            y_ref[:, hh * HEAD:(hh + 1) * HEAD] = y.astype(y_ref.dtype)


def _gdn_core(qkv_conv, gates, s0, z_gate, norm_w, n_seq, seq_len, blk0, chunk0, key_dim, n_vheads, heads,
              batch_heads):
    tb = min(seq_len, PREP_ROWS)
    n_tb = seq_len // tb
    cb = tb // CHUNK
    nk = key_dim // HEAD
    rep = n_vheads // nk
    kw = max(1, heads // rep)
    assert (heads == 1 or heads % rep == 0) and heads % batch_heads == 0
    assert n_vheads % heads == 0 and nk % kw == 0 and (2 * nk) % heads == 0 and chunk0 == blk0 * n_tb * cb

    def fwd(s, g, t):
        return (blk0 + s) * n_tb + t

    def bwd(s, g, t):
        return (blk0 + s) * n_tb + n_tb - 1 - t

    def specs(tblk):
        return [
            pl.BlockSpec((tb, kw * HEAD), lambda s, g, t: (tblk(s, g, t), (g * heads) // (rep * kw))),
            pl.BlockSpec((tb, kw * HEAD), lambda s, g, t: (tblk(s, g, t), nk // kw + (g * heads) // (rep * kw))),
            pl.BlockSpec((tb, heads * HEAD), lambda s, g, t: (tblk(s, g, t), 2 * nk // heads + g)),
        ]

    def gate_spec(tblk):
        return pl.BlockSpec((cb, CHUNK, CHUNK), lambda s, g, t: (tblk(s, g, t), 0, 0))

    state_spec = pl.BlockSpec((None, heads, 2, HEAD, HEAD), lambda s, g, t: (s, g, 0, 0, 0))
    return pl.pallas_call(
        functools.partial(_gdn_core_kernel, n_vheads=n_vheads, rep=rep, batch_heads=batch_heads),
        out_shape=(jax.ShapeDtypeStruct((n_seq * seq_len, n_vheads * HEAD), BF16),
                   jax.ShapeDtypeStruct((n_seq, n_vheads, 2, HEAD, HEAD), F32)),
        grid=(n_seq, n_vheads // heads, n_tb),
        in_specs=specs(fwd) + specs(bwd) + [gate_spec(fwd), gate_spec(bwd), state_spec,
                  pl.BlockSpec((seq_len, heads * HEAD), lambda s, g, t: (blk0 + s, g)),
                  pl.BlockSpec((1, HEAD), lambda s, g, t: (0, 0))],
        out_specs=(pl.BlockSpec((seq_len, heads * HEAD), lambda s, g, t: (s, g)), state_spec),
        scratch_shapes=[pltpu.VMEM((heads, 2, HEAD, HEAD), F32), pltpu.VMEM((heads, 2, seq_len, HEAD), F32)],
        compiler_params=_params("parallel", "parallel", "arbitrary"),
        name=f"gdn_core_{seq_len}",
    )(qkv_conv, qkv_conv, qkv_conv, qkv_conv, qkv_conv, qkv_conv, gates, gates, s0, z_gate, norm_w.reshape(1, HEAD))


def _gdn_mixer(tok, qkv_conv, z_gate, ba, key_dim, a_log, dt_bias, norm_w):
    n_vheads = z_gate.shape[1] // HEAD
    batch, seq, ctx = tok.batch, tok.seq, tok.ctx
    gates = _gdn_gates(ba, a_log, dt_bias)

    def run(n_seq, seq_len, blk0, s0, heads, batch_heads):
        return _gdn_core(qkv_conv, gates, s0, z_gate, norm_w, n_seq, seq_len, blk0, blk0 * seq_len // CHUNK,
                         key_dim, n_vheads, heads, batch_heads)

    zero_state = jnp.zeros((batch, n_vheads, 2, HEAD, HEAD), F32)
    ctx_heads = min(CTX_HEADS_PER_STEP, n_vheads)
    y_ctx, s_ctx = run(batch, ctx, tok.n_lat // ctx, zero_state, ctx_heads, ctx_heads)
    y_lat, _ = run(batch, seq, 0, s_ctx, LAT_HEADS_PER_STEP, LAT_HEADS_PER_STEP)
    return y_lat, y_ctx


def kernel(x, c, ctx, c_ctx, w_ada, b_ada, norm1_w, norm2_w, na_w_qkv, na_w_o, na_q_gain, na_k_gain, na_rpb, gdn_w_in, gdn_conv_w, gdn_w_ba, gdn_a_log, gdn_dt_bias, gdn_norm_w, gdn_w_o, ffn_w1, ffn_w3, ffn_w2):
    batch, seq, d = x.shape
    n_ctx = ctx.shape[1]
    depth = w_ada.shape[0]
    tok = _Tokens(batch, seq, n_ctx)
    assert batch + 1 <= 8

    x_all = jnp.concatenate([x.reshape(batch * seq, d), ctx.reshape(batch * n_ctx, d)], axis=0)
    cond = jnp.concatenate([c, c_ctx[None, :], jnp.zeros((8 - batch - 1, d), F32)], axis=0)
    mods = _ada(cond, w_ada, b_ada)

    na_qkv16, na_o16 = na_w_qkv.astype(BF16), na_w_o.astype(BF16)
    gdn_in16, gdn_o16 = gdn_w_in.astype(BF16), gdn_w_o.astype(BF16)
    w1_16, w3_16, w2_16 = ffn_w1.astype(BF16), ffn_w3.astype(BF16), ffn_w2.astype(BF16)

    for i in range(depth):
        update_ctx = i < depth - 1
        n_tiles = tok.all_tiles if update_ctx else tok.lat_tiles
        mod = mods[i].reshape(8, 1, 6 * d)
        j = i // 2
        if i % 2 == 0:
            qkv = _na_qkv(tok, x_all, norm1_w[i], mod, na_qkv16, j, na_q_gain[j], na_k_gain[j])
            tables = _na_bias_tables(na_rpb[j], seq // GRID_W)
            mix_lat, mix_ctx = _na_attention(tok, qkv, tables, d)
            w_o = na_o16
        else:
            conv_dim = gdn_conv_w.shape[2]
            key_dim = (2 * conv_dim - gdn_w_in.shape[2]) // 2
            w_ba = jnp.concatenate([gdn_w_ba[j, 0], gdn_w_ba[j, 1]], axis=1).astype(BF16)
            w_ba = jnp.pad(w_ba, ((0, 0), (0, CHUNK - w_ba.shape[1])))
            qkv_conv, z_gate, ba = _gdn_in(tok, x_all, norm1_w[i], mod, gdn_in16, j, gdn_conv_w[j], w_ba, key_dim)
            mix_lat, mix_ctx = _gdn_mixer(tok, qkv_conv, z_gate, ba, key_dim, gdn_a_log[j], gdn_dt_bias[j],
                                          gdn_norm_w[j])
            w_o = gdn_o16
        x_all = _matmul_residual(tok, 0, mix_lat, w_o, j, x_all, mod, 2, "mixer_out")
        if update_ctx:
            x_all = _matmul_residual(tok, tok.lat_tiles, mix_ctx, w_o, j, x_all, mod, 2, "mixer_out_ctx")
        x_all = _ffn(tok, n_tiles, x_all, norm2_w[i], mod, w1_16, w3_16, w2_16, i, in_place=update_ctx)
    return x_all.reshape(batch, seq, d)
```

```python
import functools

import jax
import jax.numpy as jnp
import numpy as np
from jax import lax
from jax.experimental import pallas as pl
from jax.experimental.pallas import tpu as pltpu

HEAD = 128
GRID_W = 64
WIN_H = 8
WIN_W = 16
ROW_GROUP = 4
NA_GROUP_UNROLL = 2
KEY_ROWS = ROW_GROUP + WIN_H
CHUNK = 128
CONV_TAPS = 4
CONV_PAD = 8
EPS = 1e-6
NEG_INF = -1e30
NA_SCALE = HEAD ** -0.5
LOG2E = 1.4426950408889634
GDN_SCALE = HEAD ** -0.5
TOKEN_TILE = 512
NORM_SLAB = 256
PREP_ROWS = 1024
CTX_HEADS_PER_STEP = 8
LAT_HEADS_PER_STEP = 2
VMEM_LIMIT = 48 * 1024 * 1024

F32 = jnp.float32
BF16 = jnp.bfloat16


def _params(*sem):
    return pltpu.CompilerParams(dimension_semantics=sem, vmem_limit_bytes=VMEM_LIMIT)


def _pick(n, cands):
    for c in cands:
        if n % c == 0:
            return c
    raise ValueError(f"no tile in {cands} divides {n}")


def _silu(x):
    return x / (1.0 + jnp.exp(-x))


def _dot(a, b):
    return jnp.dot(a, b, preferred_element_type=F32)


def _dot_nt(a, b):
    return lax.dot_general(a, b, (((1,), (1,)), ((), ())), preferred_element_type=F32)


def _dot_hi(a, b):
    return jnp.dot(a, b, preferred_element_type=F32, precision=lax.Precision.HIGHEST)


def _ada_kernel(c_ref, w_ref, b_ref, o_ref):
    s = _silu(c_ref[...]).astype(BF16)
    o_ref[...] = _dot(s, w_ref[...].astype(BF16)) + b_ref[...]


def _ada(cond, w_ada, b_ada):
    depth, d, n = w_ada.shape
    tn = _pick(n, (1024, 512, 256, 128))
    return pl.pallas_call(
        _ada_kernel,
        out_shape=jax.ShapeDtypeStruct((depth, 8, n), F32),
        grid=(depth, n // tn),
        in_specs=[
            pl.BlockSpec((8, d), lambda l, j: (0, 0)),
            pl.BlockSpec((None, d, tn), lambda l, j: (l, 0, j)),
            pl.BlockSpec((None, 1, tn), lambda l, j: (l, 0, j)),
        ],
        out_specs=pl.BlockSpec((None, 8, tn), lambda l, j: (l, 0, j)),
        compiler_params=_params("parallel", "parallel"),
        name="ada_mod",
    )(cond, w_ada, b_ada.reshape(depth, 1, n))


class _Tokens:
    def __init__(self, batch, seq, ctx):
        self.batch, self.seq, self.ctx = batch, seq, ctx
        self.n_lat = batch * seq
        self.n_all = self.n_lat + batch * ctx
        self.tm = TOKEN_TILE
        assert seq % self.tm == 0 and (batch * ctx) % self.tm == 0
        self.lat_tiles = self.n_lat // self.tm
        self.all_tiles = self.n_all // self.tm
        self.tiles_per_batch = seq // self.tm

    def mod_row(self, t):
        return jnp.where(t < self.lat_tiles, t // self.tiles_per_batch, self.batch)


def _mod_spec(tok, d, chunk):
    return pl.BlockSpec((None, 1, d), lambda m, j: (tok.mod_row(m), 0, chunk))


def _norm_mod(x, nw, sh, sc):
    ms = jnp.mean(x * x, axis=-1, keepdims=True)
    return (x * lax.rsqrt(ms + EPS) * nw) * (1.0 + sc) + sh


def _na_qkv_kernel(x_ref, nw_ref, sh_ref, sc_ref, w_ref, qg_ref, kg_ref, o_ref, h_scr, *, tiles_per_kind):
    n = pl.program_id(1)

    @pl.when(n == 0)
    def _():
        h_scr[...] = _norm_mod(x_ref[...], nw_ref[...], sh_ref[...], sc_ref[...]).astype(BF16)

    kind = n // tiles_per_kind

    @pl.when(kind == 2)
    def _():
        o_ref[...] = _dot(h_scr[...], w_ref[...]).astype(o_ref.dtype)

    @pl.when(kind < 2)
    def _():
        gain = jnp.where(kind == 0, qg_ref[...] * (NA_SCALE * LOG2E), kg_ref[...])
        h = h_scr[...]
        n_slabs = o_ref.shape[1] // NORM_SLAB
        nxt = _dot(h, w_ref[:, :NORM_SLAB])
        for j in range(n_slabs):
            acc = nxt
            if j + 1 < n_slabs:
                nxt = _dot(h, w_ref[:, (j + 1) * NORM_SLAB:(j + 2) * NORM_SLAB])
            for i in range(NORM_SLAB // HEAD):
                a = acc[:, i * HEAD:(i + 1) * HEAD]
                ms = jnp.mean(a * a, axis=-1, keepdims=True)
                lo = j * NORM_SLAB + i * HEAD
                o_ref[:, lo:lo + HEAD] = (a * lax.rsqrt(ms + EPS) * gain).astype(o_ref.dtype)


def _na_qkv(tok, x_all, norm_w, mod, w, layer, q_gain, k_gain):
    _, d, n = w.shape
    tm = tok.tm
    tn = _pick(d, (1024, 512, 256, 128))
    return pl.pallas_call(
        functools.partial(_na_qkv_kernel, tiles_per_kind=d // tn),
        out_shape=jax.ShapeDtypeStruct((tok.n_all, n), BF16),
        grid=(tok.all_tiles, n // tn),
        in_specs=[
            pl.BlockSpec((tm, d), lambda m, j: (m, 0)),
            pl.BlockSpec((1, d), lambda m, j: (0, 0)),
            _mod_spec(tok, d, 0),
            _mod_spec(tok, d, 1),
            pl.BlockSpec((None, d, tn), lambda m, j: (layer, 0, j)),
            pl.BlockSpec((1, HEAD), lambda m, j: (0, 0)),
            pl.BlockSpec((1, HEAD), lambda m, j: (0, 0)),
        ],
        out_specs=pl.BlockSpec((tm, tn), lambda m, j: (m, j)),
        scratch_shapes=[pltpu.VMEM((tm, d), BF16)],
        compiler_params=_params("parallel", "arbitrary"),
        name="na_qkv",
    )(x_all, norm_w.reshape(1, d), mod, mod, w, q_gain.reshape(1, HEAD), k_gain.reshape(1, HEAD))


def _mm_res_kernel(a_ref, w_ref, x_ref, g_ref, o_ref):
    o_ref[...] = x_ref[...] + g_ref[...] * _dot(a_ref[...], w_ref[...])


def _matmul_residual(tok, tile0, a, w, layer, x_all, mod, gate_chunk, name):
    _, k, d = w.shape
    tm = tok.tm
    tn = _pick(d, (1024, 512, 256, 128))
    n_tiles = a.shape[0] // tm
    return pl.pallas_call(
        _mm_res_kernel,
        out_shape=jax.ShapeDtypeStruct(x_all.shape, F32),
        grid=(n_tiles, d // tn),
        in_specs=[
            pl.BlockSpec((tm, k), lambda m, j: (m, 0)),
            pl.BlockSpec((None, k, tn), lambda m, j: (layer, 0, j)),
            pl.BlockSpec((tm, tn), lambda m, j: (tile0 + m, j)),
            pl.BlockSpec((None, 1, tn), lambda m, j: (tok.mod_row(tile0 + m), 0, gate_chunk * (d // tn) + j)),
        ],
        out_specs=pl.BlockSpec((tm, tn), lambda m, j: (tile0 + m, j)),
        input_output_aliases={2: 0},
        compiler_params=_params("parallel", "parallel"),
        name=name,
    )(a, w, x_all, mod)


def _ffn_kernel(x_ref, nw_ref, sh_ref, sc_ref, g_ref, w1_ref, w3_ref, w2_ref, o_ref, h_scr, acc_scr):
    f = pl.program_id(1)

    @pl.when(f == 0)
    def _():
        h_scr[...] = _norm_mod(x_ref[...], nw_ref[...], sh_ref[...], sc_ref[...]).astype(BF16)
        acc_scr[...] = jnp.zeros_like(acc_scr)

    h = h_scr[...]
    u = _silu(_dot(h, w1_ref[...])) * _dot(h, w3_ref[...])
    acc_scr[...] += _dot(u.astype(BF16), w2_ref[...])

    @pl.when(f == pl.num_programs(1) - 1)
    def _():
        o_ref[...] = x_ref[...] + g_ref[...] * acc_scr[...]


def _ffn(tok, n_tiles, x_all, norm_w, mod, w1, w3, w2, layer, in_place):
    _, d, dff = w1.shape
    tm = tok.tm
    tf = _pick(dff, (512, 256, 128))
    return pl.pallas_call(
        _ffn_kernel,
        out_shape=jax.ShapeDtypeStruct(x_all.shape if in_place else (n_tiles * tm, d), F32),
        grid=(n_tiles, dff // tf),
        in_specs=[
            pl.BlockSpec((tm, d), lambda m, f: (m, 0)),
            pl.BlockSpec((1, d), lambda m, f: (0, 0)),
            _mod_spec(tok, d, 3),
            _mod_spec(tok, d, 4),
            _mod_spec(tok, d, 5),
            pl.BlockSpec((None, d, tf), lambda m, f: (layer, 0, f)),
            pl.BlockSpec((None, d, tf), lambda m, f: (layer, 0, f)),
            pl.BlockSpec((None, tf, d), lambda m, f: (layer, f, 0)),
        ],
        out_specs=pl.BlockSpec((tm, d), lambda m, f: (m, 0)),
        scratch_shapes=[pltpu.VMEM((tm, d), BF16), pltpu.VMEM((tm, d), F32)],
        input_output_aliases={0: 0} if in_place else {},
        compiler_params=_params("parallel", "arbitrary"),
        name="ffn",
    )(x_all, norm_w.reshape(1, d), mod, mod, mod, w1, w3, w2)


def _na_bias_tables(rpb, rows):
    kh = min(WIN_H, rows)
    cols = np.arange(GRID_W)
    col_start = np.clip(cols - WIN_W // 2, 0, GRID_W - WIN_W)
    in_win = (cols[None, :] >= col_start[:, None]) & (cols[None, :] < col_start[:, None] + WIN_W)
    dc = np.clip(cols[None, :] - cols[:, None], 1 - WIN_W, WIN_W - 1) + WIN_W - 1
    off = np.zeros((3, ROW_GROUP, KEY_ROWS), np.int32)
    ok = np.zeros((3, ROW_GROUP, KEY_ROWS), bool)
    for t, r0 in enumerate((0, ROW_GROUP, rows - ROW_GROUP)):
        start = min(max(r0 - WIN_H // 2, 0), rows - KEY_ROWS)
        for i in range(ROW_GROUP):
            r = r0 + i
            rs = min(max(r - kh // 2, 0), rows - kh)
            for m in range(KEY_ROWS):
                j = start + m - rs
                ok[t, i, m] = 0 <= j < kh
                off[t, i, m] = min(max(rs - r + j + WIN_H - 1, 0), 2 * WIN_H - 2)
    heads = rpb.shape[0]
    onehot = ((dc[None] == np.arange(2 * WIN_W - 1)[:, None, None]) & in_win[None]).astype(np.float32)
    by_col = jnp.einsum("hod,dqk->hqok", rpb.astype(F32), onehot, precision=lax.Precision.HIGHEST)
    by_col = (by_col + np.where(in_win, 0.0, NEG_INF).astype(np.float32)[:, None, :]) * LOG2E
    by_col = by_col.reshape(heads, GRID_W, (2 * WIN_H - 1) * GRID_W)
    pieces = []
    for t in range(3):
        for i in range(ROW_GROUP):
            valid = np.nonzero(ok[t, i])[0]
            m0, m1 = int(valid[0]), int(valid[-1]) + 1
            o0 = int(off[t, i, m0])
            assert np.array_equal(valid, np.arange(m0, m1))
            assert np.array_equal(off[t, i, m0:m1], o0 + np.arange(m1 - m0))
            pieces.append(jnp.pad(by_col[:, :, o0 * GRID_W:(o0 + m1 - m0) * GRID_W],
                                  ((0, 0), (0, 0), (m0 * GRID_W, (KEY_ROWS - m1) * GRID_W)),
                                  constant_values=NEG_INF * LOG2E))
    return jnp.stack(pieces, axis=1).reshape(heads, 3, ROW_GROUP * GRID_W, KEY_ROWS * GRID_W)


def _softmax_pv(parts):
    m = functools.reduce(jnp.maximum, [jnp.max(s, axis=-1, keepdims=True) for s, _ in parts])
    ps = [jnp.exp2(s - m) for s, _ in parts]
    den = functools.reduce(jnp.add, [jnp.sum(p, axis=-1, keepdims=True) for p in ps])
    num = functools.reduce(jnp.add, [_dot(p.astype(BF16), v) for p, (_, v) in zip(ps, parts)])
    return num / den


def _na_kernel(q_ref, k_ref, v_ref, qc_ref, kc_ref, vc_ref, bias_ref, o_ref, oc_ref, *, rows):
    kc = kc_ref[...]
    vc = vc_ref[...]
    n_groups = rows // ROW_GROUP
    gq = ROW_GROUP * GRID_W
    gk = KEY_ROWS * GRID_W

    def group(gi, carry):
        r0 = gi * ROW_GROUP
        start = jnp.clip(r0 - WIN_H // 2, 0, rows - KEY_ROWS)
        table = jnp.where(gi == 0, 0, jnp.where(gi == n_groups - 1, 2, 1))
        q = q_ref[pl.ds(pl.multiple_of(r0 * GRID_W, gq), gq), :]
        koff = pl.multiple_of(start * GRID_W, GRID_W)
        ks = k_ref[pl.ds(koff, gk), :]
        vs = v_ref[pl.ds(koff, gk), :]
        s_loc = _dot_nt(q, ks) + bias_ref[table]
        s_ctx = _dot_nt(q, kc)
        o = _softmax_pv([(s_loc, vs), (s_ctx, vc)])
        o_ref[pl.ds(pl.multiple_of(r0 * GRID_W, gq), gq), :] = o.astype(o_ref.dtype)
        return carry

    lax.fori_loop(0, n_groups, group, 0, unroll=NA_GROUP_UNROLL)
    s_cc = _dot_nt(qc_ref[...], kc)
    oc_ref[...] = _softmax_pv([(s_cc, vc)]).astype(oc_ref.dtype)


def _na_attention(tok, qkv, bias_tables, d):
    heads = d // HEAD
    seq, ctx, batch = tok.seq, tok.ctx, tok.batch
    rows = seq // GRID_W
    assert rows % ROW_GROUP == 0 and rows >= KEY_ROWS
    ctx_blk0 = tok.n_lat // ctx
    gq, gk = ROW_GROUP * GRID_W, KEY_ROWS * GRID_W

    def lat(kind):
        return pl.BlockSpec((seq, HEAD), lambda h, b: (b, kind * heads + h))

    def cx(kind):
        return pl.BlockSpec((ctx, HEAD), lambda h, b: (ctx_blk0 + b, kind * heads + h))

    o_lat, o_ctx = pl.pallas_call(
        functools.partial(_na_kernel, rows=rows),
        out_shape=(jax.ShapeDtypeStruct((tok.n_lat, d), BF16),
                   jax.ShapeDtypeStruct((batch * ctx, d), BF16)),
        grid=(heads, batch),
        in_specs=[lat(0), lat(1), lat(2), cx(0), cx(1), cx(2),
                  pl.BlockSpec((None, 3, gq, gk), lambda h, b: (h, 0, 0, 0))],
        out_specs=(pl.BlockSpec((seq, HEAD), lambda h, b: (b, h)),
                   pl.BlockSpec((ctx, HEAD), lambda h, b: (b, h))),
        compiler_params=_params("parallel", "parallel"),
        name="na_attention",
    )(qkv, qkv, qkv, qkv, qkv, qkv, bias_tables)
    return o_lat, o_ctx


def _gdn_in_kernel(x_ref, xp_ref, xn_ref, nw_ref, sh_ref, sc_ref, w_ref, cw_ref, wba_ref, qkv_ref, z_ref, ba_ref,
                   h_scr, *acc_refs, tok, n_key_tiles, n_conv_tiles):
    m = pl.program_id(0)
    n = pl.program_id(1)
    tm = x_ref.shape[0]
    tn = w_ref.shape[1]
    heads_per_slab = NORM_SLAB // HEAD
    acc = [acc_refs[slot * heads_per_slab:(slot + 1) * heads_per_slab] for slot in range(2)]

    def put(slot, res):
        for i in range(heads_per_slab):
            acc[slot][i][...] = res[:, i * HEAD:(i + 1) * HEAD]

    @pl.when(n == 0)
    def _():
        nw, sh, sc = nw_ref[...], sh_ref[...], sc_ref[...]
        h_scr[pl.ds(0, CONV_PAD), :] = _norm_mod(xp_ref[...], nw, sh, sc).astype(BF16)
        h_scr[pl.ds(CONV_PAD, tm), :] = _norm_mod(x_ref[...], nw, sh, sc).astype(BF16)
        h_scr[pl.ds(CONV_PAD + tm, CONV_PAD), :] = _norm_mod(xn_ref[...], nw, sh, sc).astype(BF16)
        ba_ref[...] = _dot(h_scr[pl.ds(CONV_PAD, tm), :], wba_ref[...])

    @pl.when(n >= n_conv_tiles)
    def _():
        z_ref[...] = _dot(h_scr[pl.ds(CONV_PAD, tm), :], w_ref[...]).astype(z_ref.dtype)

    def conv_tile(normalise):
        seq_len = jnp.where(m < tok.lat_tiles, tok.seq, tok.ctx)
        pos = (m * tm + lax.broadcasted_iota(jnp.int32, (tm, 1), 0)) & (seq_len - 1)
        ok = [pos >= 1, None, pos <= seq_len - 2, pos <= seq_len - 3]
        scale = jnp.where(n < n_key_tiles, GDN_SCALE, 1.0)
        h = h_scr[...]
        n_slabs = tn // NORM_SLAB
        put(0, _dot(h, w_ref[:, :NORM_SLAB]))
        for j in range(n_slabs):
            cols = slice(j * NORM_SLAB, (j + 1) * NORM_SLAB)
            if j + 1 < n_slabs:
                put((j + 1) % 2, _dot(h, w_ref[:, (j + 1) * NORM_SLAB:(j + 2) * NORM_SLAB]))
            cw = cw_ref[:, cols]
            for i in range(heads_per_slab):
                yh = None
                for tap in range(CONV_TAPS):
                    rows = pl.ds(CONV_PAD - (CONV_TAPS - 1) // 2 + tap, tm)
                    term = acc[j % 2][i][rows, :] * cw[tap:tap + 1, i * HEAD:(i + 1) * HEAD]
                    if ok[tap] is not None:
                        term = jnp.where(ok[tap], term, 0.0)
                    yh = term if yh is None else yh + term
                yh = _silu(yh)
                if normalise:
                    yh = yh * (lax.rsqrt(jnp.sum(yh * yh, axis=-1, keepdims=True) + EPS) * scale)
                lo = j * NORM_SLAB + i * HEAD
                qkv_ref[:, lo:lo + HEAD] = yh

    pl.when(n < 2 * n_key_tiles)(functools.partial(conv_tile, True))
    pl.when((n >= 2 * n_key_tiles) & (n < n_conv_tiles))(functools.partial(conv_tile, False))


def _gdn_in(tok, x_all, norm_w, mod, w, layer, conv_w, w_ba, key_dim):
    _, d, n = w.shape
    conv_dim = conv_w.shape[1]
    tm = tok.tm
    tn = _pick(np.gcd(key_dim, n - conv_dim), (1024, 512, 256))
    n_conv_tiles = conv_dim // tn
    halo_blocks = tm // CONV_PAD
    last_halo = tok.n_all // CONV_PAD - 1
    assert tok.seq & (tok.seq - 1) == 0 and tok.ctx & (tok.ctx - 1) == 0
    kern = functools.partial(_gdn_in_kernel, tok=tok, n_key_tiles=key_dim // tn, n_conv_tiles=n_conv_tiles)
    return pl.pallas_call(
        kern,
        out_shape=(jax.ShapeDtypeStruct((tok.n_all, conv_dim), F32),
                   jax.ShapeDtypeStruct((tok.n_all, n - conv_dim), BF16),
                   jax.ShapeDtypeStruct((tok.n_all, w_ba.shape[1]), F32)),
        grid=(tok.all_tiles, n // tn),
        in_specs=[
            pl.BlockSpec((tm, d), lambda m, j: (m, 0)),
            pl.BlockSpec((CONV_PAD, d), lambda m, j: (jnp.maximum(m * halo_blocks - 1, 0), 0)),
            pl.BlockSpec((CONV_PAD, d), lambda m, j: (jnp.minimum((m + 1) * halo_blocks, last_halo), 0)),
            pl.BlockSpec((1, d), lambda m, j: (0, 0)),
            _mod_spec(tok, d, 0),
            _mod_spec(tok, d, 1),
            pl.BlockSpec((None, d, tn), lambda m, j: (layer, 0, j)),
            pl.BlockSpec((CONV_TAPS, tn), lambda m, j: (0, jnp.minimum(j, n_conv_tiles - 1))),
            pl.BlockSpec(w_ba.shape, lambda m, j: (0, 0)),
        ],
        out_specs=(pl.BlockSpec((tm, tn), lambda m, j: (m, jnp.minimum(j, n_conv_tiles - 1))),
                   pl.BlockSpec((tm, tn), lambda m, j: (m, jnp.maximum(j - n_conv_tiles, 0))),
                   pl.BlockSpec((tm, w_ba.shape[1]), lambda m, j: (m, 0))),
        scratch_shapes=[pltpu.VMEM((tm + 2 * CONV_PAD, d), BF16)]
        + [pltpu.VMEM((tm + 2 * CONV_PAD, HEAD), F32)] * (2 * NORM_SLAB // HEAD),
        compiler_params=_params("parallel", "arbitrary"),
        name="gdn_in",
    )(x_all, x_all, x_all, norm_w.reshape(1, d), mod, mod, w, conv_w, w_ba)


def _gates_kernel(ba_ref, alog_ref, dt_ref, o_ref, *, n_chunks, n_vheads):
    row = lax.broadcasted_iota(jnp.int32, (CHUNK, CHUNK), 0)
    col = lax.broadcasted_iota(jnp.int32, (CHUNK, CHUNK), 1)
    prefix = (row <= col).astype(F32)
    suffix = (row >= col).astype(F32)
    is_decay = (row % (2 * n_vheads)) >= n_vheads
    neg_a = -jnp.exp(alog_ref[...])
    for c in range(n_chunks):
        bt = ba_ref[pl.ds(c * CHUNK, CHUNK), :].T
        beta = 1.0 / (1.0 + jnp.exp(-bt))
        x = bt + dt_ref[...]
        g = neg_a * (jnp.maximum(x, 0.0) + jnp.log1p(jnp.exp(-jnp.abs(x))))
        cum = jnp.where(row < 2 * n_vheads, _dot_hi(g, prefix), _dot_hi(g, suffix))
        o_ref[c] = jnp.where(is_decay, cum, beta)


def _gdn_gates(ba, a_log, dt_bias):
    n_tok, n_col = ba.shape
    n_vheads = a_log.shape[1]
    assert n_col == CHUNK and 4 * n_vheads <= n_col
    zeros = jnp.zeros_like(a_log)
    pad = ((0, n_col - 4 * n_vheads), (0, 0))
    alog_col = jnp.pad(jnp.concatenate([zeros, a_log], axis=1).reshape(4 * n_vheads, 1), pad)
    dt_col = jnp.pad(jnp.concatenate([zeros, dt_bias], axis=1).reshape(4 * n_vheads, 1), pad)
    per_step = 4
    assert (n_tok // CHUNK) % per_step == 0
    return pl.pallas_call(
        functools.partial(_gates_kernel, n_chunks=per_step, n_vheads=n_vheads),
        out_shape=jax.ShapeDtypeStruct((n_tok // CHUNK, n_col, CHUNK), F32),
        grid=(n_tok // (CHUNK * per_step),),
        in_specs=[pl.BlockSpec((CHUNK * per_step, n_col), lambda i: (i, 0)),
                  pl.BlockSpec((n_col, 1), lambda i: (0, 0)),
                  pl.BlockSpec((n_col, 1), lambda i: (0, 0))],
        out_specs=pl.BlockSpec((per_step, n_col, CHUNK), lambda i: (i, 0, 0)),
        compiler_params=_params("parallel"),
        name="gdn_gates",
    )(ba, alog_col, dt_col)


def _bdot(a, b):
    return jnp.einsum("pmk,pkn->pmn", a, b, preferred_element_type=F32)


def _bdot_nt(a, b):
    return jnp.einsum("pmk,pnk->pmn", a, b, preferred_element_type=F32)


def _unit_tri_inverse(l, row, col, tick):
    n = l.shape[-1]
    b = 2
    x = (row == col).astype(F32) - jnp.where(row // b == col // b, l, 0.0)
    while b < n:
        pair = (row // (2 * b) == col // (2 * b)) & (row // b != col // b)
        c16 = jnp.where(pair, l, 0.0).astype(BF16)
        x16 = x.astype(BF16)
        x = x - _bdot(x16, _bdot(c16, x16).astype(BF16))
        tick()
        b *= 2
    return x


def _chunk_factors(q, k, v, beta_r, g_r, row, col, reverse, tick):
    p = q.shape[0]

    def per_row(r):
        return jnp.stack([jnp.broadcast_to(r[c], (CHUNK, CHUNK)).T for c in range(p)], axis=0)

    incl = (row <= col) if reverse else (row >= col)
    strict = (row < col) if reverse else (row > col)
    beta_c = per_row(beta_r)
    g_c = per_row(g_r)
    decay = jnp.where(incl, jnp.exp(jnp.where(incl, g_c - g_r, 0.0)), 0.0)
    k16 = k.astype(BF16)
    k_beta = k * beta_c
    v_beta = v * beta_c
    l_mat = jnp.where(strict, _bdot_nt(k_beta.astype(BF16), k16) * decay, 0.0)
    tick()
    t_inv = _unit_tri_inverse(l_mat, row, col, tick)
    exp_g = jnp.exp(g_c)
    rhs = jnp.concatenate([k_beta * exp_g, v_beta], axis=2).astype(BF16)
    wu16 = _bdot(t_inv.astype(BF16), rhs).astype(BF16)
    tick()
    last = 0 if reverse else CHUNK - 1
    g_last = g_c[:, last:last + 1, :]
    kd = k * jnp.exp(g_last - g_c)
    kd_t = jnp.stack([kd[c].T for c in range(p)], axis=0).astype(BF16)
    a16 = jnp.where(incl, _bdot_nt(q.astype(BF16), k16) * decay, 0.0).astype(BF16)
    r = _bdot(jnp.concatenate([kd_t, a16], axis=1), wu16)
    tick()
    qe = q * exp_g - r[:, CHUNK:, :HEAD]
    mq = jnp.concatenate([-r[:, :CHUNK, :HEAD], qe], axis=1).astype(BF16)
    return mq, r[:, :, HEAD:], jnp.exp(g_last)


def _gdn_core_kernel(qf_ref, kf_ref, vf_ref, qb_ref, kb_ref, vb_ref, gf_ref, gb_ref, s0_ref, z_ref, nw_ref,
                     y_ref, sf_ref, s_scr, o_scr, *, n_vheads, rep, batch_heads):
    grp = pl.program_id(1)
    t = pl.program_id(2)
    n_t = pl.num_programs(2)
    tb = qf_ref.shape[0]
    cb = tb // CHUNK
    heads = s_scr.shape[0]
    row = lax.broadcasted_iota(jnp.int32, (CHUNK, CHUNK), 0)
    col = lax.broadcasted_iota(jnp.int32, (CHUNK, CHUNK), 1)

    @pl.when(t == 0)
    def _():
        s_scr[...] = s0_ref[...]

    pending = []

    def tick():
        if pending:
            pending.pop(0)()

    def recurrence(z, hs, mq, na, egl):
        state = {h: s_scr[h, z] for h in hs}
        blk = t if z == 0 else n_t - 1 - t

        def step(c):
            for j, h in enumerate(hs):
                s = state[h]
                r = _dot(mq[j * cb + c], s.astype(BF16)) + na[j * cb + c]
                o_scr[h, z, pl.ds(pl.multiple_of(blk * tb + c * CHUNK, CHUNK), CHUNK), :] = r[CHUNK:]
                state[h] = s * egl[j * cb + c] + r[:CHUNK]

        def finish():
            for h in hs:
                s_scr[h, z] = state[h]

        order = list(range(cb)) if z == 0 else list(reversed(range(cb)))
        return [functools.partial(step, c) for c in order] + [finish]

    def chunks(ref, lanes):
        return ref[:, lanes].reshape(cb, CHUNK, HEAD)

    for h0 in range(0, heads, batch_heads):
        hs = list(range(h0, h0 + batch_heads))
        for z in range(2):
            q_ref, k_ref, v_ref, gates_ref = (qf_ref, kf_ref, vf_ref, gf_ref) if z == 0 else (qb_ref, kb_ref, vb_ref, gb_ref)
            ksl = [slice((h // rep) * HEAD, (h // rep + 1) * HEAD) for h in hs]
            gate0 = z * 2 * n_vheads + grp * heads
            mq, na, egl = _chunk_factors(
                jnp.concatenate([chunks(q_ref, sl) for sl in ksl], axis=0),
                jnp.concatenate([chunks(k_ref, sl) for sl in ksl], axis=0),
                jnp.concatenate([chunks(v_ref, slice(h * HEAD, (h + 1) * HEAD)) for h in hs], axis=0),
                jnp.concatenate([gates_ref[:, pl.ds(gate0 + h, 1), :] for h in hs], axis=0),
                jnp.concatenate([gates_ref[:, pl.ds(gate0 + n_vheads + h, 1), :] for h in hs], axis=0),
                row, col, reverse=(z == 1), tick=tick)
            while pending:
                tick()
            pending.extend(recurrence(z, hs, mq, na, egl))
    while pending:
        tick()

    @pl.when(t == n_t - 1)
    def _():
        sf_ref[...] = s_scr[...]
        for hh in range(heads):
            o = o_scr[hh, 0] + o_scr[hh, 1]
            ms = jnp.mean(o * o, axis=-1, keepdims=True)
            zz = z_ref[:, hh * HEAD:(hh + 1) * HEAD].astype(F32)
            y = (o * lax.rsqrt(ms + EPS) * nw_ref[...]) * _silu(zz)
            y_ref[:, hh * HEAD:(hh + 1) * HEAD] = y.astype(y_ref.dtype)


def _gdn_core(qkv_conv, gates, s0, z_gate, norm_w, n_seq, seq_len, blk0, chunk0, key_dim, n_vheads, heads,
              batch_heads):
    tb = min(seq_len, PREP_ROWS)
    n_tb = seq_len // tb
    cb = tb // CHUNK
    nk = key_dim // HEAD
    rep = n_vheads // nk
    kw = max(1, heads // rep)
    assert (heads == 1 or heads % rep == 0) and heads % batch_heads == 0
    assert n_vheads % heads == 0 and nk % kw == 0 and (2 * nk) % heads == 0 and chunk0 == blk0 * n_tb * cb

    def fwd(s, g, t):
        return (blk0 + s) * n_tb + t

    def bwd(s, g, t):
        return (blk0 + s) * n_tb + n_tb - 1 - t

    def specs(tblk):
        return [
            pl.BlockSpec((tb, kw * HEAD), lambda s, g, t: (tblk(s, g, t), (g * heads) // (rep * kw))),
            pl.BlockSpec((tb, kw * HEAD), lambda s, g, t: (tblk(s, g, t), nk // kw + (g * heads) // (rep * kw))),
            pl.BlockSpec((tb, heads * HEAD), lambda s, g, t: (tblk(s, g, t), 2 * nk // heads + g)),
        ]

    def gate_spec(tblk):
        return pl.BlockSpec((cb, CHUNK, CHUNK), lambda s, g, t: (tblk(s, g, t), 0, 0))

    state_spec = pl.BlockSpec((None, heads, 2, HEAD, HEAD), lambda s, g, t: (s, g, 0, 0, 0))
    return pl.pallas_call(
        functools.partial(_gdn_core_kernel, n_vheads=n_vheads, rep=rep, batch_heads=batch_heads),
        out_shape=(jax.ShapeDtypeStruct((n_seq * seq_len, n_vheads * HEAD), BF16),
                   jax.ShapeDtypeStruct((n_seq, n_vheads, 2, HEAD, HEAD), F32)),
        grid=(n_seq, n_vheads // heads, n_tb),
        in_specs=specs(fwd) + specs(bwd) + [gate_spec(fwd), gate_spec(bwd), state_spec,
                  pl.BlockSpec((seq_len, heads * HEAD), lambda s, g, t: (blk0 + s, g)),
                  pl.BlockSpec((1, HEAD), lambda s, g, t: (0, 0))],
        out_specs=(pl.BlockSpec((seq_len, heads * HEAD), lambda s, g, t: (s, g)), state_spec),
        scratch_shapes=[pltpu.VMEM((heads, 2, HEAD, HEAD), F32), pltpu.VMEM((heads, 2, seq_len, HEAD), F32)],
        compiler_params=_params("parallel", "parallel", "arbitrary"),
        name=f"gdn_core_{seq_len}",
    )(qkv_conv, qkv_conv, qkv_conv, qkv_conv, qkv_conv, qkv_conv, gates, gates, s0, z_gate, norm_w.reshape(1, HEAD))


def _gdn_mixer(tok, qkv_conv, z_gate, ba, key_dim, a_log, dt_bias, norm_w):
    n_vheads = z_gate.shape[1] // HEAD
    batch, seq, ctx = tok.batch, tok.seq, tok.ctx
    gates = _gdn_gates(ba, a_log, dt_bias)

    def run(n_seq, seq_len, blk0, s0, heads, batch_heads):
        return _gdn_core(qkv_conv, gates, s0, z_gate, norm_w, n_seq, seq_len, blk0, blk0 * seq_len // CHUNK,
                         key_dim, n_vheads, heads, batch_heads)

    zero_state = jnp.zeros((batch, n_vheads, 2, HEAD, HEAD), F32)
    ctx_heads = min(CTX_HEADS_PER_STEP, n_vheads)
    y_ctx, s_ctx = run(batch, ctx, tok.n_lat // ctx, zero_state, ctx_heads, ctx_heads)
    y_lat, _ = run(batch, seq, 0, s_ctx, LAT_HEADS_PER_STEP, LAT_HEADS_PER_STEP)
    return y_lat, y_ctx


def kernel(x, c, ctx, c_ctx, w_ada, b_ada, norm1_w, norm2_w, na_w_qkv, na_w_o, na_q_gain, na_k_gain, na_rpb, gdn_w_in, gdn_conv_w, gdn_w_ba, gdn_a_log, gdn_dt_bias, gdn_norm_w, gdn_w_o, ffn_w1, ffn_w3, ffn_w2):
    batch, seq, d = x.shape
    n_ctx = ctx.shape[1]
    depth = w_ada.shape[0]
    tok = _Tokens(batch, seq, n_ctx)
    assert batch + 1 <= 8

    x_all = jnp.concatenate([x.reshape(batch * seq, d), ctx.reshape(batch * n_ctx, d)], axis=0)
    cond = jnp.concatenate([c, c_ctx[None, :], jnp.zeros((8 - batch - 1, d), F32)], axis=0)
    mods = _ada(cond, w_ada, b_ada)

    na_qkv16, na_o16 = na_w_qkv.astype(BF16), na_w_o.astype(BF16)
    gdn_in16, gdn_o16 = gdn_w_in.astype(BF16), gdn_w_o.astype(BF16)
    w1_16, w3_16, w2_16 = ffn_w1.astype(BF16), ffn_w3.astype(BF16), ffn_w2.astype(BF16)

    for i in range(depth):
        update_ctx = i < depth - 1
        n_tiles = tok.all_tiles if update_ctx else tok.lat_tiles
        mod = mods[i].reshape(8, 1, 6 * d)
        j = i // 2
        if i % 2 == 0:
            qkv = _na_qkv(tok, x_all, norm1_w[i], mod, na_qkv16, j, na_q_gain[j], na_k_gain[j])
            tables = _na_bias_tables(na_rpb[j], seq // GRID_W)
            mix_lat, mix_ctx = _na_attention(tok, qkv, tables, d)
            w_o = na_o16
        else:
            conv_dim = gdn_conv_w.shape[2]
            key_dim = (2 * conv_dim - gdn_w_in.shape[2]) // 2
            w_ba = jnp.concatenate([gdn_w_ba[j, 0], gdn_w_ba[j, 1]], axis=1).astype(BF16)
            w_ba = jnp.pad(w_ba, ((0, 0), (0, CHUNK - w_ba.shape[1])))
            qkv_conv, z_gate, ba = _gdn_in(tok, x_all, norm1_w[i], mod, gdn_in16, j, gdn_conv_w[j], w_ba, key_dim)
            mix_lat, mix_ctx = _gdn_mixer(tok, qkv_conv, z_gate, ba, key_dim, gdn_a_log[j], gdn_dt_bias[j],
                                          gdn_norm_w[j])
            w_o = gdn_o16
        x_all = _matmul_residual(tok, 0, mix_lat, w_o, j, x_all, mod, 2, "mixer_out")
        if update_ctx:
            x_all = _matmul_residual(tok, tok.lat_tiles, mix_ctx, w_o, j, x_all, mod, 2, "mixer_out_ctx")
        x_all = _ffn(tok, n_tiles, x_all, norm2_w[i], mod, w1_16, w3_16, w2_16, i, in_place=update_ctx)
    return x_all.reshape(batch, seq, d)
```

```python
import functools

import jax
import jax.numpy as jnp
import numpy as np
from jax import lax
from jax.experimental import pallas as pl
from jax.experimental.pallas import tpu as pltpu

HEAD = 128
GRID_W = 64
WIN_H = 8
WIN_W = 16
ROW_GROUP = 4
NA_GROUP_UNROLL = 2
KEY_ROWS = ROW_GROUP + WIN_H
CHUNK = 128
CONV_TAPS = 4
CONV_PAD = 8
EPS = 1e-6
NEG_INF = -1e30
NA_SCALE = HEAD ** -0.5
LOG2E = 1.4426950408889634
GDN_SCALE = HEAD ** -0.5
TOKEN_TILE = 512
NORM_SLAB = 256
PREP_ROWS = 1024
CTX_HEADS_PER_STEP = 8
LAT_HEADS_PER_STEP = 2
VMEM_LIMIT = 48 * 1024 * 1024

F32 = jnp.float32
BF16 = jnp.bfloat16


def _params(*sem):
    return pltpu.CompilerParams(dimension_semantics=sem, vmem_limit_bytes=VMEM_LIMIT)


def _pick(n, cands):
    for c in cands:
        if n % c == 0:
            return c
    raise ValueError(f"no tile in {cands} divides {n}")


def _silu(x):
    return x / (1.0 + jnp.exp(-x))


def _dot(a, b):
    return jnp.dot(a, b, preferred_element_type=F32)


def _dot_nt(a, b):
    return lax.dot_general(a, b, (((1,), (1,)), ((), ())), preferred_element_type=F32)


def _dot_hi(a, b):
    return jnp.dot(a, b, preferred_element_type=F32, precision=lax.Precision.HIGHEST)


def _ada_kernel(c_ref, w_ref, b_ref, o_ref):
    s = _silu(c_ref[...]).astype(BF16)
    o_ref[...] = _dot(s, w_ref[...].astype(BF16)) + b_ref[...]


def _ada(cond, w_ada, b_ada):
    depth, d, n = w_ada.shape
    tn = _pick(n, (1024, 512, 256, 128))
    return pl.pallas_call(
        _ada_kernel,
        out_shape=jax.ShapeDtypeStruct((depth, 8, n), F32),
        grid=(depth, n // tn),
        in_specs=[
            pl.BlockSpec((8, d), lambda l, j: (0, 0)),
            pl.BlockSpec((None, d, tn), lambda l, j: (l, 0, j)),
            pl.BlockSpec((None, 1, tn), lambda l, j: (l, 0, j)),
        ],
        out_specs=pl.BlockSpec((None, 8, tn), lambda l, j: (l, 0, j)),
        compiler_params=_params("parallel", "parallel"),
        name="ada_mod",
    )(cond, w_ada, b_ada.reshape(depth, 1, n))


class _Tokens:
    def __init__(self, batch, seq, ctx):
        self.batch, self.seq, self.ctx = batch, seq, ctx
        self.n_lat = batch * seq
        self.n_all = self.n_lat + batch * ctx
        self.tm = TOKEN_TILE
        assert seq % self.tm == 0 and (batch * ctx) % self.tm == 0
        self.lat_tiles = self.n_lat // self.tm
        self.all_tiles = self.n_all // self.tm
        self.tiles_per_batch = seq // self.tm

    def mod_row(self, t):
        return jnp.where(t < self.lat_tiles, t // self.tiles_per_batch, self.batch)


def _mod_spec(tok, d, chunk):
    return pl.BlockSpec((None, 1, d), lambda m, j: (tok.mod_row(m), 0, chunk))


def _norm_mod(x, nw, sh, sc):
    ms = jnp.mean(x * x, axis=-1, keepdims=True)
    return (x * lax.rsqrt(ms + EPS) * nw) * (1.0 + sc) + sh


def _na_qkv_kernel(x_ref, nw_ref, sh_ref, sc_ref, w_ref, qg_ref, kg_ref, o_ref, h_scr, *, tiles_per_kind):
    n = pl.program_id(1)

    @pl.when(n == 0)
    def _():
        h_scr[...] = _norm_mod(x_ref[...], nw_ref[...], sh_ref[...], sc_ref[...]).astype(BF16)

    kind = n // tiles_per_kind

    @pl.when(kind == 2)
    def _():
        o_ref[...] = _dot(h_scr[...], w_ref[...]).astype(o_ref.dtype)

    @pl.when(kind < 2)
    def _():
        gain = jnp.where(kind == 0, qg_ref[...] * (NA_SCALE * LOG2E), kg_ref[...])
        h = h_scr[...]
        n_slabs = o_ref.shape[1] // NORM_SLAB
        nxt = _dot(h, w_ref[:, :NORM_SLAB])
        for j in range(n_slabs):
            acc = nxt
            if j + 1 < n_slabs:
                nxt = _dot(h, w_ref[:, (j + 1) * NORM_SLAB:(j + 2) * NORM_SLAB])
            for i in range(NORM_SLAB // HEAD):
                a = acc[:, i * HEAD:(i + 1) * HEAD]
                ms = jnp.mean(a * a, axis=-1, keepdims=True)
                lo = j * NORM_SLAB + i * HEAD
                o_ref[:, lo:lo + HEAD] = (a * lax.rsqrt(ms + EPS) * gain).astype(o_ref.dtype)


def _na_qkv(tok, x_all, norm_w, mod, w, layer, q_gain, k_gain):
    _, d, n = w.shape
    tm = tok.tm
    tn = _pick(d, (1024, 512, 256, 128))
    return pl.pallas_call(
        functools.partial(_na_qkv_kernel, tiles_per_kind=d // tn),
        out_shape=jax.ShapeDtypeStruct((tok.n_all, n), BF16),
        grid=(tok.all_tiles, n // tn),
        in_specs=[
            pl.BlockSpec((tm, d), lambda m, j: (m, 0)),
            pl.BlockSpec((1, d), lambda m, j: (0, 0)),
            _mod_spec(tok, d, 0),
            _mod_spec(tok, d, 1),
            pl.BlockSpec((None, d, tn), lambda m, j: (layer, 0, j)),
            pl.BlockSpec((1, HEAD), lambda m, j: (0, 0)),
            pl.BlockSpec((1, HEAD), lambda m, j: (0, 0)),
        ],
        out_specs=pl.BlockSpec((tm, tn), lambda m, j: (m, j)),
        scratch_shapes=[pltpu.VMEM((tm, d), BF16)],
        compiler_params=_params("parallel", "arbitrary"),
        name="na_qkv",
    )(x_all, norm_w.reshape(1, d), mod, mod, w, q_gain.reshape(1, HEAD), k_gain.reshape(1, HEAD))


def _mm_res_kernel(a_ref, w_ref, x_ref, g_ref, o_ref):
    o_ref[...] = x_ref[...] + g_ref[...] * _dot(a_ref[...], w_ref[...])


def _matmul_residual(tok, tile0, a, w, layer, x_all, mod, gate_chunk, name):
    _, k, d = w.shape
    tm = tok.tm
    tn = _pick(d, (1024, 512, 256, 128))
    n_tiles = a.shape[0] // tm
    return pl.pallas_call(
        _mm_res_kernel,
        out_shape=jax.ShapeDtypeStruct(x_all.shape, F32),
        grid=(n_tiles, d // tn),
        in_specs=[
            pl.BlockSpec((tm, k), lambda m, j: (m, 0)),
            pl.BlockSpec((None, k, tn), lambda m, j: (layer, 0, j)),
            pl.BlockSpec((tm, tn), lambda m, j: (tile0 + m, j)),
            pl.BlockSpec((None, 1, tn), lambda m, j: (tok.mod_row(tile0 + m), 0, gate_chunk * (d // tn) + j)),
        ],
        out_specs=pl.BlockSpec((tm, tn), lambda m, j: (tile0 + m, j)),
        input_output_aliases={2: 0},
        compiler_params=_params("parallel", "parallel"),
        name=name,
    )(a, w, x_all, mod)


def _ffn_kernel(x_ref, nw_ref, sh_ref, sc_ref, g_ref, w1_ref, w3_ref, w2_ref, o_ref, h_scr, acc_scr):
    f = pl.program_id(1)

    @pl.when(f == 0)
    def _():
        h_scr[...] = _norm_mod(x_ref[...], nw_ref[...], sh_ref[...], sc_ref[...]).astype(BF16)
        acc_scr[...] = jnp.zeros_like(acc_scr)

    h = h_scr[...]
    u = _silu(_dot(h, w1_ref[...])) * _dot(h, w3_ref[...])
    acc_scr[...] += _dot(u.astype(BF16), w2_ref[...])

    @pl.when(f == pl.num_programs(1) - 1)
    def _():
        o_ref[...] = x_ref[...] + g_ref[...] * acc_scr[...]


def _ffn(tok, n_tiles, x_all, norm_w, mod, w1, w3, w2, layer, in_place):
    _, d, dff = w1.shape
    tm = tok.tm
    tf = _pick(dff, (512, 256, 128))
    return pl.pallas_call(
        _ffn_kernel,
        out_shape=jax.ShapeDtypeStruct(x_all.shape if in_place else (n_tiles * tm, d), F32),
        grid=(n_tiles, dff // tf),
        in_specs=[
            pl.BlockSpec((tm, d), lambda m, f: (m, 0)),
            pl.BlockSpec((1, d), lambda m, f: (0, 0)),
            _mod_spec(tok, d, 3),
            _mod_spec(tok, d, 4),
            _mod_spec(tok, d, 5),
            pl.BlockSpec((None, d, tf), lambda m, f: (layer, 0, f)),
            pl.BlockSpec((None, d, tf), lambda m, f: (layer, 0, f)),
            pl.BlockSpec((None, tf, d), lambda m, f: (layer, f, 0)),
        ],
        out_specs=pl.BlockSpec((tm, d), lambda m, f: (m, 0)),
        scratch_shapes=[pltpu.VMEM((tm, d), BF16), pltpu.VMEM((tm, d), F32)],
        input_output_aliases={0: 0} if in_place else {},
        compiler_params=_params("parallel", "arbitrary"),
        name="ffn",
    )(x_all, norm_w.reshape(1, d), mod, mod, mod, w1, w3, w2)


def _na_bias_tables(rpb, rows):
    kh = min(WIN_H, rows)
    cols = np.arange(GRID_W)
    col_start = np.clip(cols - WIN_W // 2, 0, GRID_W - WIN_W)
    in_win = (cols[None, :] >= col_start[:, None]) & (cols[None, :] < col_start[:, None] + WIN_W)
    dc = np.clip(cols[None, :] - cols[:, None], 1 - WIN_W, WIN_W - 1) + WIN_W - 1
    off = np.zeros((3, ROW_GROUP, KEY_ROWS), np.int32)
    ok = np.zeros((3, ROW_GROUP, KEY_ROWS), bool)
    for t, r0 in enumerate((0, ROW_GROUP, rows - ROW_GROUP)):
        start = min(max(r0 - WIN_H // 2, 0), rows - KEY_ROWS)
        for i in range(ROW_GROUP):
            r = r0 + i
            rs = min(max(r - kh // 2, 0), rows - kh)
            for m in range(KEY_ROWS):
                j = start + m - rs
                ok[t, i, m] = 0 <= j < kh
                off[t, i, m] = min(max(rs - r + j + WIN_H - 1, 0), 2 * WIN_H - 2)
    heads = rpb.shape[0]
    onehot = ((dc[None] == np.arange(2 * WIN_W - 1)[:, None, None]) & in_win[None]).astype(np.float32)
    by_col = jnp.einsum("hod,dqk->hqok", rpb.astype(F32), onehot, precision=lax.Precision.HIGHEST)
    by_col = (by_col + np.where(in_win, 0.0, NEG_INF).astype(np.float32)[:, None, :]) * LOG2E
    by_col = by_col.reshape(heads, GRID_W, (2 * WIN_H - 1) * GRID_W)
    pieces = []
    for t in range(3):
        for i in range(ROW_GROUP):
            valid = np.nonzero(ok[t, i])[0]
            m0, m1 = int(valid[0]), int(valid[-1]) + 1
            o0 = int(off[t, i, m0])
            assert np.array_equal(valid, np.arange(m0, m1))
            assert np.array_equal(off[t, i, m0:m1], o0 + np.arange(m1 - m0))
            pieces.append(jnp.pad(by_col[:, :, o0 * GRID_W:(o0 + m1 - m0) * GRID_W],
                                  ((0, 0), (0, 0), (m0 * GRID_W, (KEY_ROWS - m1) * GRID_W)),
                                  constant_values=NEG_INF * LOG2E))
    return jnp.stack(pieces, axis=1).reshape(heads, 3, ROW_GROUP * GRID_W, KEY_ROWS * GRID_W)


def _softmax_pv(parts):
    m = functools.reduce(jnp.maximum, [jnp.max(s, axis=-1, keepdims=True) for s, _ in parts])
    ps = [jnp.exp2(s - m) for s, _ in parts]
    den = functools.reduce(jnp.add, [jnp.sum(p, axis=-1, keepdims=True) for p in ps])
    num = functools.reduce(jnp.add, [_dot(p.astype(BF16), v) for p, (_, v) in zip(ps, parts)])
    return num / den


def _na_kernel(q_ref, k_ref, v_ref, qc_ref, kc_ref, vc_ref, bias_ref, o_ref, oc_ref, *, rows):
    kc = kc_ref[...]
    vc = vc_ref[...]
    n_groups = rows // ROW_GROUP
    gq = ROW_GROUP * GRID_W
    gk = KEY_ROWS * GRID_W

    def group(gi, carry):
        r0 = gi * ROW_GROUP
        start = jnp.clip(r0 - WIN_H // 2, 0, rows - KEY_ROWS)
        table = jnp.where(gi == 0, 0, jnp.where(gi == n_groups - 1, 2, 1))
        q = q_ref[pl.ds(pl.multiple_of(r0 * GRID_W, gq), gq), :]
        koff = pl.multiple_of(start * GRID_W, GRID_W)
        ks = k_ref[pl.ds(koff, gk), :]
        vs = v_ref[pl.ds(koff, gk), :]
        s_loc = _dot_nt(q, ks) + bias_ref[table]
        s_ctx = _dot_nt(q, kc)
        o = _softmax_pv([(s_loc, vs), (s_ctx, vc)])
        o_ref[pl.ds(pl.multiple_of(r0 * GRID_W, gq), gq), :] = o.astype(o_ref.dtype)
        return carry

    lax.fori_loop(0, n_groups, group, 0, unroll=NA_GROUP_UNROLL)
    s_cc = _dot_nt(qc_ref[...], kc)
    oc_ref[...] = _softmax_pv([(s_cc, vc)]).astype(oc_ref.dtype)


def _na_attention(tok, qkv, bias_tables, d):
    heads = d // HEAD
    seq, ctx, batch = tok.seq, tok.ctx, tok.batch
    rows = seq // GRID_W
    assert rows % ROW_GROUP == 0 and rows >= KEY_ROWS
    ctx_blk0 = tok.n_lat // ctx
    gq, gk = ROW_GROUP * GRID_W, KEY_ROWS * GRID_W

    def lat(kind):
        return pl.BlockSpec((seq, HEAD), lambda h, b: (b, kind * heads + h))

    def cx(kind):
        return pl.BlockSpec((ctx, HEAD), lambda h, b: (ctx_blk0 + b, kind * heads + h))

    o_lat, o_ctx = pl.pallas_call(
        functools.partial(_na_kernel, rows=rows),
        out_shape=(jax.ShapeDtypeStruct((tok.n_lat, d), BF16),
                   jax.ShapeDtypeStruct((batch * ctx, d), BF16)),
        grid=(heads, batch),
        in_specs=[lat(0), lat(1), lat(2), cx(0), cx(1), cx(2),
                  pl.BlockSpec((None, 3, gq, gk), lambda h, b: (h, 0, 0, 0))],
        out_specs=(pl.BlockSpec((seq, HEAD), lambda h, b: (b, h)),
                   pl.BlockSpec((ctx, HEAD), lambda h, b: (b, h))),
        compiler_params=_params("parallel", "parallel"),
        name="na_attention",
    )(qkv, qkv, qkv, qkv, qkv, qkv, bias_tables)
    return o_lat, o_ctx


def _gdn_in_kernel(x_ref, xp_ref, xn_ref, nw_ref, sh_ref, sc_ref, w_ref, cw_ref, wba_ref, qkv_ref, z_ref, ba_ref,
                   h_scr, *acc_refs, tok, n_key_tiles, n_conv_tiles):
    m = pl.program_id(0)
    n = pl.program_id(1)
    tm = x_ref.shape[0]
    tn = w_ref.shape[1]
    heads_per_slab = NORM_SLAB // HEAD
    acc = [acc_refs[slot * heads_per_slab:(slot + 1) * heads_per_slab] for slot in range(2)]

    def put(slot, res):
        for i in range(heads_per_slab):
            acc[slot][i][...] = res[:, i * HEAD:(i + 1) * HEAD]

    @pl.when(n == 0)
    def _():
        nw, sh, sc = nw_ref[...], sh_ref[...], sc_ref[...]
        h_scr[pl.ds(0, CONV_PAD), :] = _norm_mod(xp_ref[...], nw, sh, sc).astype(BF16)
        h_scr[pl.ds(CONV_PAD, tm), :] = _norm_mod(x_ref[...], nw, sh, sc).astype(BF16)
        h_scr[pl.ds(CONV_PAD + tm, CONV_PAD), :] = _norm_mod(xn_ref[...], nw, sh, sc).astype(BF16)
        ba_ref[...] = _dot(h_scr[pl.ds(CONV_PAD, tm), :], wba_ref[...])

    @pl.when(n >= n_conv_tiles)
    def _():
        z_ref[...] = _dot(h_scr[pl.ds(CONV_PAD, tm), :], w_ref[...]).astype(z_ref.dtype)

    def conv_tile(normalise):
        seq_len = jnp.where(m < tok.lat_tiles, tok.seq, tok.ctx)
        pos = (m * tm + lax.broadcasted_iota(jnp.int32, (tm, 1), 0)) & (seq_len - 1)
        ok = [pos >= 1, None, pos <= seq_len - 2, pos <= seq_len - 3]
        period = min(tok.seq, tok.ctx, tm)
        edges = sorted({e for b in range(0, tm + 1, period) for e in (b - CONV_PAD, b) if 0 <= e < tm})
        scale = jnp.where(n < n_key_tiles, GDN_SCALE, 1.0)
        h = h_scr[...]
        n_slabs = tn // NORM_SLAB
        put(0, _dot(h, w_ref[:, :NORM_SLAB]))
        for j in range(n_slabs):
            cols = slice(j * NORM_SLAB, (j + 1) * NORM_SLAB)
            if j + 1 < n_slabs:
                put((j + 1) % 2, _dot(h, w_ref[:, (j + 1) * NORM_SLAB:(j + 2) * NORM_SLAB]))
            cw = cw_ref[:, cols]
            for i in range(heads_per_slab):
                def taps(row0, n_rows, masked):
                    y = None
                    for tap in range(CONV_TAPS):
                        rows = pl.ds(CONV_PAD - (CONV_TAPS - 1) // 2 + tap + row0, n_rows)
                        term = acc[j % 2][i][rows, :] * cw[tap:tap + 1, i * HEAD:(i + 1) * HEAD]
                        if masked and ok[tap] is not None:
                            term = jnp.where(ok[tap][row0:row0 + n_rows], term, 0.0)
                        y = term if y is None else y + term
                    return y

                parts, r = [], 0
                for e in edges:
                    if e > r:
                        parts.append(taps(r, e - r, False))
                    parts.append(taps(e, CONV_PAD, True))
                    r = e + CONV_PAD
                if r < tm:
                    parts.append(taps(r, tm - r, False))
                yh = _silu(jnp.concatenate(parts, axis=0))
                if normalise:
                    yh = yh * (lax.rsqrt(jnp.sum(yh * yh, axis=-1, keepdims=True) + EPS) * scale)
                lo = j * NORM_SLAB + i * HEAD
                qkv_ref[:, lo:lo + HEAD] = yh

    pl.when(n < 2 * n_key_tiles)(functools.partial(conv_tile, True))
    pl.when((n >= 2 * n_key_tiles) & (n < n_conv_tiles))(functools.partial(conv_tile, False))


def _gdn_in(tok, x_all, norm_w, mod, w, layer, conv_w, w_ba, key_dim):
    _, d, n = w.shape
    conv_dim = conv_w.shape[1]
    tm = tok.tm
    tn = _pick(np.gcd(key_dim, n - conv_dim), (1024, 512, 256))
    n_conv_tiles = conv_dim // tn
    halo_blocks = tm // CONV_PAD
    last_halo = tok.n_all // CONV_PAD - 1
    assert tok.seq & (tok.seq - 1) == 0 and tok.ctx & (tok.ctx - 1) == 0
    kern = functools.partial(_gdn_in_kernel, tok=tok, n_key_tiles=key_dim // tn, n_conv_tiles=n_conv_tiles)
    return pl.pallas_call(
        kern,
        out_shape=(jax.ShapeDtypeStruct((tok.n_all, conv_dim), F32),
                   jax.ShapeDtypeStruct((tok.n_all, n - conv_dim), BF16),
                   jax.ShapeDtypeStruct((tok.n_all, w_ba.shape[1]), F32)),
        grid=(tok.all_tiles, n // tn),
        in_specs=[
            pl.BlockSpec((tm, d), lambda m, j: (m, 0)),
            pl.BlockSpec((CONV_PAD, d), lambda m, j: (jnp.maximum(m * halo_blocks - 1, 0), 0)),
            pl.BlockSpec((CONV_PAD, d), lambda m, j: (jnp.minimum((m + 1) * halo_blocks, last_halo), 0)),
            pl.BlockSpec((1, d), lambda m, j: (0, 0)),
            _mod_spec(tok, d, 0),
            _mod_spec(tok, d, 1),
            pl.BlockSpec((None, d, tn), lambda m, j: (layer, 0, j)),
            pl.BlockSpec((CONV_TAPS, tn), lambda m, j: (0, jnp.minimum(j, n_conv_tiles - 1))),
            pl.BlockSpec(w_ba.shape, lambda m, j: (0, 0)),
        ],
        out_specs=(pl.BlockSpec((tm, tn), lambda m, j: (m, jnp.minimum(j, n_conv_tiles - 1))),
                   pl.BlockSpec((tm, tn), lambda m, j: (m, jnp.maximum(j - n_conv_tiles, 0))),
                   pl.BlockSpec((tm, w_ba.shape[1]), lambda m, j: (m, 0))),
        scratch_shapes=[pltpu.VMEM((tm + 2 * CONV_PAD, d), BF16)]
        + [pltpu.VMEM((tm + 2 * CONV_PAD, HEAD), F32)] * (2 * NORM_SLAB // HEAD),
        compiler_params=_params("parallel", "arbitrary"),
        name="gdn_in",
    )(x_all, x_all, x_all, norm_w.reshape(1, d), mod, mod, w, conv_w, w_ba)


def _gates_kernel(ba_ref, alog_ref, dt_ref, o_ref, *, n_chunks, n_vheads):
    row = lax.broadcasted_iota(jnp.int32, (CHUNK, CHUNK), 0)
    col = lax.broadcasted_iota(jnp.int32, (CHUNK, CHUNK), 1)
    prefix = (row <= col).astype(F32)
    suffix = (row >= col).astype(F32)
    is_decay = (row % (2 * n_vheads)) >= n_vheads
    neg_a = -jnp.exp(alog_ref[...])
    for c in range(n_chunks):
        bt = ba_ref[pl.ds(c * CHUNK, CHUNK), :].T
        beta = 1.0 / (1.0 + jnp.exp(-bt))
        x = bt + dt_ref[...]
        g = neg_a * (jnp.maximum(x, 0.0) + jnp.log1p(jnp.exp(-jnp.abs(x))))
        cum = jnp.where(row < 2 * n_vheads, _dot_hi(g, prefix), _dot_hi(g, suffix))
        o_ref[c] = jnp.where(is_decay, cum, beta)


def _gdn_gates(ba, a_log, dt_bias):
    n_tok, n_col = ba.shape
    n_vheads = a_log.shape[1]
    assert n_col == CHUNK and 4 * n_vheads <= n_col
    zeros = jnp.zeros_like(a_log)
    pad = ((0, n_col - 4 * n_vheads), (0, 0))
    alog_col = jnp.pad(jnp.concatenate([zeros, a_log], axis=1).reshape(4 * n_vheads, 1), pad)
    dt_col = jnp.pad(jnp.concatenate([zeros, dt_bias], axis=1).reshape(4 * n_vheads, 1), pad)
    per_step = 4
    assert (n_tok // CHUNK) % per_step == 0
    return pl.pallas_call(
        functools.partial(_gates_kernel, n_chunks=per_step, n_vheads=n_vheads),
        out_shape=jax.ShapeDtypeStruct((n_tok // CHUNK, n_col, CHUNK), F32),
        grid=(n_tok // (CHUNK * per_step),),
        in_specs=[pl.BlockSpec((CHUNK * per_step, n_col), lambda i: (i, 0)),
                  pl.BlockSpec((n_col, 1), lambda i: (0, 0)),
                  pl.BlockSpec((n_col, 1), lambda i: (0, 0))],
        out_specs=pl.BlockSpec((per_step, n_col, CHUNK), lambda i: (i, 0, 0)),
        compiler_params=_params("parallel"),
        name="gdn_gates",
    )(ba, alog_col, dt_col)


def _bdot(a, b):
    return jnp.einsum("pmk,pkn->pmn", a, b, preferred_element_type=F32)


def _bdot_nt(a, b):
    return jnp.einsum("pmk,pnk->pmn", a, b, preferred_element_type=F32)


def _unit_tri_inverse(l, row, col, tick):
    n = l.shape[-1]
    b = 2
    x = (row == col).astype(F32) - jnp.where(row // b == col // b, l, 0.0)
    while b < n:
        pair = (row // (2 * b) == col // (2 * b)) & (row // b != col // b)
        c16 = jnp.where(pair, l, 0.0).astype(BF16)
        x16 = x.astype(BF16)
        x = x - _bdot(x16, _bdot(c16, x16).astype(BF16))
        tick()
        b *= 2
    return x


def _chunk_factors(q, k, v, beta_r, g_r, row, col, reverse, tick):
    p = q.shape[0]

    def per_row(r):
        return jnp.stack([jnp.broadcast_to(r[c], (CHUNK, CHUNK)).T for c in range(p)], axis=0)

    incl = (row <= col) if reverse else (row >= col)
    strict = (row < col) if reverse else (row > col)
    beta_c = per_row(beta_r)
    g_c = per_row(g_r)
    decay = jnp.where(incl, jnp.exp(jnp.where(incl, g_c - g_r, 0.0)), 0.0)
    k16 = k.astype(BF16)
    k_beta = k * beta_c
    v_beta = v * beta_c
    l_mat = jnp.where(strict, _bdot_nt(k_beta.astype(BF16), k16) * decay, 0.0)
    tick()
    t_inv = _unit_tri_inverse(l_mat, row, col, tick)
    exp_g = jnp.exp(g_c)
    rhs = jnp.concatenate([k_beta * exp_g, v_beta], axis=2).astype(BF16)
    wu16 = _bdot(t_inv.astype(BF16), rhs).astype(BF16)
    tick()
    last = 0 if reverse else CHUNK - 1
    g_last = g_c[:, last:last + 1, :]
    kd = k * jnp.exp(g_last - g_c)
    kd_t = jnp.stack([kd[c].T for c in range(p)], axis=0).astype(BF16)
    a16 = jnp.where(incl, _bdot_nt(q.astype(BF16), k16) * decay, 0.0).astype(BF16)
    r = _bdot(jnp.concatenate([kd_t, a16], axis=1), wu16)
    tick()
    qe = q * exp_g - r[:, CHUNK:, :HEAD]
    mq = jnp.concatenate([-r[:, :CHUNK, :HEAD], qe], axis=1).astype(BF16)
    return mq, r[:, :, HEAD:], jnp.exp(g_last)


def _gdn_core_kernel(qf_ref, kf_ref, vf_ref, qb_ref, kb_ref, vb_ref, gf_ref, gb_ref, s0_ref, z_ref, nw_ref,
                     y_ref, sf_ref, s_scr, o_scr, *, n_vheads, rep, batch_heads):
    grp = pl.program_id(1)
    t = pl.program_id(2)
    n_t = pl.num_programs(2)
    tb = qf_ref.shape[0]
    cb = tb // CHUNK
    heads = s_scr.shape[0]
    row = lax.broadcasted_iota(jnp.int32, (CHUNK, CHUNK), 0)
    col = lax.broadcasted_iota(jnp.int32, (CHUNK, CHUNK), 1)

    @pl.when(t == 0)
    def _():
        s_scr[...] = s0_ref[...]

    pending = []

    def tick():
        if pending:
            pending.pop(0)()

    def recurrence(z, hs, mq, na, egl):
        state = {h: s_scr[h, z] for h in hs}
        blk = t if z == 0 else n_t - 1 - t

        def step(c):
            for j, h in enumerate(hs):
                s = state[h]
                r = _dot(mq[j * cb + c], s.astype(BF16)) + na[j * cb + c]
                o_scr[h, z, pl.ds(pl.multiple_of(blk * tb + c * CHUNK, CHUNK), CHUNK), :] = r[CHUNK:]
                state[h] = s * egl[j * cb + c] + r[:CHUNK]

        def finish():
            for h in hs:
                s_scr[h, z] = state[h]

        order = list(range(cb)) if z == 0 else list(reversed(range(cb)))
        return [functools.partial(step, c) for c in order] + [finish]

    def chunks(ref, lanes):
        return ref[:, lanes].reshape(cb, CHUNK, HEAD)

    for h0 in range(0, heads, batch_heads):
        hs = list(range(h0, h0 + batch_heads))
        for z in range(2):
            q_ref, k_ref, v_ref, gates_ref = (qf_ref, kf_ref, vf_ref, gf_ref) if z == 0 else (qb_ref, kb_ref, vb_ref, gb_ref)
            ksl = [slice((h // rep) * HEAD, (h // rep + 1) * HEAD) for h in hs]
            gate0 = z * 2 * n_vheads + grp * heads
            mq, na, egl = _chunk_factors(
                jnp.concatenate([chunks(q_ref, sl) for sl in ksl], axis=0),
                jnp.concatenate([chunks(k_ref, sl) for sl in ksl], axis=0),
                jnp.concatenate([chunks(v_ref, slice(h * HEAD, (h + 1) * HEAD)) for h in hs], axis=0),
                jnp.concatenate([gates_ref[:, pl.ds(gate0 + h, 1), :] for h in hs], axis=0),
                jnp.concatenate([gates_ref[:, pl.ds(gate0 + n_vheads + h, 1), :] for h in hs], axis=0),
                row, col, reverse=(z == 1), tick=tick)
            while pending:
                tick()
            pending.extend(recurrence(z, hs, mq, na, egl))
    while pending:
        tick()

    @pl.when(t == n_t - 1)
    def _():
        sf_ref[...] = s_scr[...]
        for hh in range(heads):
            o = o_scr[hh, 0] + o_scr[hh, 1]
            ms = jnp.mean(o * o, axis=-1, keepdims=True)
            zz = z_ref[:, hh * HEAD:(hh + 1) * HEAD].astype(F32)
            y = (o * lax.rsqrt(ms + EPS) * nw_ref[...]) * _silu(zz)
            y_ref[:, hh * HEAD:(hh + 1) * HEAD] = y.astype(y_ref.dtype)


def _gdn_core(qkv_conv, gates, s0, z_gate, norm_w, n_seq, seq_len, blk0, chunk0, key_dim, n_vheads, heads,
              batch_heads):
    tb = min(seq_len, PREP_ROWS)
    n_tb = seq_len // tb
    cb = tb // CHUNK
    nk = key_dim // HEAD
    rep = n_vheads // nk
    kw = max(1, heads // rep)
    assert (heads == 1 or heads % rep == 0) and heads % batch_heads == 0
    assert n_vheads % heads == 0 and nk % kw == 0 and (2 * nk) % heads == 0 and chunk0 == blk0 * n_tb * cb

    def fwd(s, g, t):
        return (blk0 + s) * n_tb + t

    def bwd(s, g, t):
        return (blk0 + s) * n_tb + n_tb - 1 - t

    def specs(tblk):
        return [
            pl.BlockSpec((tb, kw * HEAD), lambda s, g, t: (tblk(s, g, t), (g * heads) // (rep * kw))),
            pl.BlockSpec((tb, kw * HEAD), lambda s, g, t: (tblk(s, g, t), nk // kw + (g * heads) // (rep * kw))),
            pl.BlockSpec((tb, heads * HEAD), lambda s, g, t: (tblk(s, g, t), 2 * nk // heads + g)),
        ]

    def gate_spec(tblk):
        return pl.BlockSpec((cb, CHUNK, CHUNK), lambda s, g, t: (tblk(s, g, t), 0, 0))

    state_spec = pl.BlockSpec((None, heads, 2, HEAD, HEAD), lambda s, g, t: (s, g, 0, 0, 0))
    return pl.pallas_call(
        functools.partial(_gdn_core_kernel, n_vheads=n_vheads, rep=rep, batch_heads=batch_heads),
        out_shape=(jax.ShapeDtypeStruct((n_seq * seq_len, n_vheads * HEAD), BF16),
                   jax.ShapeDtypeStruct((n_seq, n_vheads, 2, HEAD, HEAD), F32)),
        grid=(n_seq, n_vheads // heads, n_tb),
        in_specs=specs(fwd) + specs(bwd) + [gate_spec(fwd), gate_spec(bwd), state_spec,
                  pl.BlockSpec((seq_len, heads * HEAD), lambda s, g, t: (blk0 + s, g)),
                  pl.BlockSpec((1, HEAD), lambda s, g, t: (0, 0))],
        out_specs=(pl.BlockSpec((seq_len, heads * HEAD), lambda s, g, t: (s, g)), state_spec),
        scratch_shapes=[pltpu.VMEM((heads, 2, HEAD, HEAD), F32), pltpu.VMEM((heads, 2, seq_len, HEAD), F32)],
        compiler_params=_params("parallel", "parallel", "arbitrary"),
        name=f"gdn_core_{seq_len}",
    )(qkv_conv, qkv_conv, qkv_conv, qkv_conv, qkv_conv, qkv_conv, gates, gates, s0, z_gate, norm_w.reshape(1, HEAD))


def _gdn_mixer(tok, qkv_conv, z_gate, ba, key_dim, a_log, dt_bias, norm_w):
    n_vheads = z_gate.shape[1] // HEAD
    batch, seq, ctx = tok.batch, tok.seq, tok.ctx
    gates = _gdn_gates(ba, a_log, dt_bias)

    def run(n_seq, seq_len, blk0, s0, heads, batch_heads):
        return _gdn_core(qkv_conv, gates, s0, z_gate, norm_w, n_seq, seq_len, blk0, blk0 * seq_len // CHUNK,
                         key_dim, n_vheads, heads, batch_heads)

    zero_state = jnp.zeros((batch, n_vheads, 2, HEAD, HEAD), F32)
    ctx_heads = min(CTX_HEADS_PER_STEP, n_vheads)
    y_ctx, s_ctx = run(batch, ctx, tok.n_lat // ctx, zero_state, ctx_heads, ctx_heads)
    y_lat, _ = run(batch, seq, 0, s_ctx, LAT_HEADS_PER_STEP, LAT_HEADS_PER_STEP)
    return y_lat, y_ctx


def kernel(x, c, ctx, c_ctx, w_ada, b_ada, norm1_w, norm2_w, na_w_qkv, na_w_o, na_q_gain, na_k_gain, na_rpb, gdn_w_in, gdn_conv_w, gdn_w_ba, gdn_a_log, gdn_dt_bias, gdn_norm_w, gdn_w_o, ffn_w1, ffn_w3, ffn_w2):
    batch, seq, d = x.shape
    n_ctx = ctx.shape[1]
    depth = w_ada.shape[0]
    tok = _Tokens(batch, seq, n_ctx)
    assert batch + 1 <= 8

    x_all = jnp.concatenate([x.reshape(batch * seq, d), ctx.reshape(batch * n_ctx, d)], axis=0)
    cond = jnp.concatenate([c, c_ctx[None, :], jnp.zeros((8 - batch - 1, d), F32)], axis=0)
    mods = _ada(cond, w_ada, b_ada)

    na_qkv16, na_o16 = na_w_qkv.astype(BF16), na_w_o.astype(BF16)
    gdn_in16, gdn_o16 = gdn_w_in.astype(BF16), gdn_w_o.astype(BF16)
    w1_16, w3_16, w2_16 = ffn_w1.astype(BF16), ffn_w3.astype(BF16), ffn_w2.astype(BF16)

    for i in range(depth):
        update_ctx = i < depth - 1
        n_tiles = tok.all_tiles if update_ctx else tok.lat_tiles
        mod = mods[i].reshape(8, 1, 6 * d)
        j = i // 2
        if i % 2 == 0:
            qkv = _na_qkv(tok, x_all, norm1_w[i], mod, na_qkv16, j, na_q_gain[j], na_k_gain[j])
            tables = _na_bias_tables(na_rpb[j], seq // GRID_W)
            mix_lat, mix_ctx = _na_attention(tok, qkv, tables, d)
            w_o = na_o16
        else:
            conv_dim = gdn_conv_w.shape[2]
            key_dim = (2 * conv_dim - gdn_w_in.shape[2]) // 2
            w_ba = jnp.concatenate([gdn_w_ba[j, 0], gdn_w_ba[j, 1]], axis=1).astype(BF16)
            w_ba = jnp.pad(w_ba, ((0, 0), (0, CHUNK - w_ba.shape[1])))
            qkv_conv, z_gate, ba = _gdn_in(tok, x_all, norm1_w[i], mod, gdn_in16, j, gdn_conv_w[j], w_ba, key_dim)
            mix_lat, mix_ctx = _gdn_mixer(tok, qkv_conv, z_gate, ba, key_dim, gdn_a_log[j], gdn_dt_bias[j],
                                          gdn_norm_w[j])
            w_o = gdn_o16
        x_all = _matmul_residual(tok, 0, mix_lat, w_o, j, x_all, mod, 2, "mixer_out")
        if update_ctx:
            x_all = _matmul_residual(tok, tok.lat_tiles, mix_ctx, w_o, j, x_all, mod, 2, "mixer_out_ctx")
        x_all = _ffn(tok, n_tiles, x_all, norm2_w[i], mod, w1_16, w3_16, w2_16, i, in_place=update_ctx)
    return x_all.reshape(batch, seq, d)
```

```python
import functools

import jax
import jax.numpy as jnp
import numpy as np
from jax import lax
from jax.experimental import pallas as pl
from jax.experimental.pallas import tpu as pltpu

HEAD = 128
GRID_W = 64
WIN_H = 8
WIN_W = 16
ROW_GROUP = 4
NA_GROUP_UNROLL = 8
KEY_ROWS = ROW_GROUP + WIN_H
CHUNK = 128
CONV_TAPS = 4
CONV_PAD = 8
EPS = 1e-6
NEG_INF = -1e30
NA_SCALE = HEAD ** -0.5
LOG2E = 1.4426950408889634
GDN_SCALE = HEAD ** -0.5
TOKEN_TILE = 512
NORM_SLAB = 256
PREP_ROWS = 1024
CTX_HEADS_PER_STEP = 8
LAT_HEADS_PER_STEP = 2
VMEM_LIMIT = 48 * 1024 * 1024

F32 = jnp.float32
BF16 = jnp.bfloat16


def _params(*sem):
    return pltpu.CompilerParams(dimension_semantics=sem, vmem_limit_bytes=VMEM_LIMIT)


def _pick(n, cands):
    for c in cands:
        if n % c == 0:
            return c
    raise ValueError(f"no tile in {cands} divides {n}")


def _silu(x):
    return x / (1.0 + jnp.exp(-x))


def _dot(a, b):
    return jnp.dot(a, b, preferred_element_type=F32)


def _dot_nt(a, b):
    return lax.dot_general(a, b, (((1,), (1,)), ((), ())), preferred_element_type=F32)


def _dot_hi(a, b):
    return jnp.dot(a, b, preferred_element_type=F32, precision=lax.Precision.HIGHEST)


def _ada_kernel(c_ref, w_ref, b_ref, o_ref):
    s = _silu(c_ref[...]).astype(BF16)
    o_ref[...] = _dot(s, w_ref[...].astype(BF16)) + b_ref[...]


def _ada(cond, w_ada, b_ada):
    depth, d, n = w_ada.shape
    tn = _pick(n, (1024, 512, 256, 128))
    return pl.pallas_call(
        _ada_kernel,
        out_shape=jax.ShapeDtypeStruct((depth, 8, n), F32),
        grid=(depth, n // tn),
        in_specs=[
            pl.BlockSpec((8, d), lambda l, j: (0, 0)),
            pl.BlockSpec((None, d, tn), lambda l, j: (l, 0, j)),
            pl.BlockSpec((None, 1, tn), lambda l, j: (l, 0, j)),
        ],
        out_specs=pl.BlockSpec((None, 8, tn), lambda l, j: (l, 0, j)),
        compiler_params=_params("parallel", "parallel"),
        name="ada_mod",
    )(cond, w_ada, b_ada.reshape(depth, 1, n))


class _Tokens:
    def __init__(self, batch, seq, ctx):
        self.batch, self.seq, self.ctx = batch, seq, ctx
        self.n_lat = batch * seq
        self.n_all = self.n_lat + batch * ctx
        self.tm = TOKEN_TILE
        assert seq % self.tm == 0 and (batch * ctx) % self.tm == 0
        self.lat_tiles = self.n_lat // self.tm
        self.all_tiles = self.n_all // self.tm
        self.tiles_per_batch = seq // self.tm

    def mod_row(self, t):
        return jnp.where(t < self.lat_tiles, t // self.tiles_per_batch, self.batch)


def _mod_spec(tok, d, chunk):
    return pl.BlockSpec((None, 1, d), lambda m, j: (tok.mod_row(m), 0, chunk))


def _norm_mod(x, nw, sh, sc):
    ms = jnp.mean(x * x, axis=-1, keepdims=True)
    return (x * lax.rsqrt(ms + EPS) * nw) * (1.0 + sc) + sh


def _na_qkv_kernel(x_ref, nw_ref, sh_ref, sc_ref, w_ref, qg_ref, kg_ref, o_ref, h_scr, *, tiles_per_kind):
    n = pl.program_id(1)

    @pl.when(n == 0)
    def _():
        h_scr[...] = _norm_mod(x_ref[...], nw_ref[...], sh_ref[...], sc_ref[...]).astype(BF16)

    kind = n // tiles_per_kind

    @pl.when(kind == 2)
    def _():
        o_ref[...] = _dot(h_scr[...], w_ref[...]).astype(o_ref.dtype)

    @pl.when(kind < 2)
    def _():
        gain = jnp.where(kind == 0, qg_ref[...] * (NA_SCALE * LOG2E), kg_ref[...])
        h = h_scr[...]
        n_slabs = o_ref.shape[1] // NORM_SLAB
        nxt = _dot(h, w_ref[:, :NORM_SLAB])
        for j in range(n_slabs):
            acc = nxt
            if j + 1 < n_slabs:
                nxt = _dot(h, w_ref[:, (j + 1) * NORM_SLAB:(j + 2) * NORM_SLAB])
            for i in range(NORM_SLAB // HEAD):
                a = acc[:, i * HEAD:(i + 1) * HEAD]
                ms = jnp.mean(a * a, axis=-1, keepdims=True)
                lo = j * NORM_SLAB + i * HEAD
                o_ref[:, lo:lo + HEAD] = (a * lax.rsqrt(ms + EPS) * gain).astype(o_ref.dtype)


def _na_qkv(tok, x_all, norm_w, mod, w, layer, q_gain, k_gain):
    _, d, n = w.shape
    tm = tok.tm
    tn = _pick(d, (1024, 512, 256, 128))
    return pl.pallas_call(
        functools.partial(_na_qkv_kernel, tiles_per_kind=d // tn),
        out_shape=jax.ShapeDtypeStruct((tok.n_all, n), BF16),
        grid=(tok.all_tiles, n // tn),
        in_specs=[
            pl.BlockSpec((tm, d), lambda m, j: (m, 0)),
            pl.BlockSpec((1, d), lambda m, j: (0, 0)),
            _mod_spec(tok, d, 0),
            _mod_spec(tok, d, 1),
            pl.BlockSpec((None, d, tn), lambda m, j: (layer, 0, j)),
            pl.BlockSpec((1, HEAD), lambda m, j: (0, 0)),
            pl.BlockSpec((1, HEAD), lambda m, j: (0, 0)),
        ],
        out_specs=pl.BlockSpec((tm, tn), lambda m, j: (m, j)),
        scratch_shapes=[pltpu.VMEM((tm, d), BF16)],
        compiler_params=_params("parallel", "arbitrary"),
        name="na_qkv",
    )(x_all, norm_w.reshape(1, d), mod, mod, w, q_gain.reshape(1, HEAD), k_gain.reshape(1, HEAD))


def _mm_res_kernel(a_ref, w_ref, x_ref, g_ref, o_ref):
    o_ref[...] = x_ref[...] + g_ref[...] * _dot(a_ref[...], w_ref[...])


def _matmul_residual(tok, tile0, a, w, layer, x_all, mod, gate_chunk, name):
    _, k, d = w.shape
    tm = tok.tm
    tn = _pick(d, (1024, 512, 256, 128))
    n_tiles = a.shape[0] // tm
    return pl.pallas_call(
        _mm_res_kernel,
        out_shape=jax.ShapeDtypeStruct(x_all.shape, F32),
        grid=(n_tiles, d // tn),
        in_specs=[
            pl.BlockSpec((tm, k), lambda m, j: (m, 0)),
            pl.BlockSpec((None, k, tn), lambda m, j: (layer, 0, j)),
            pl.BlockSpec((tm, tn), lambda m, j: (tile0 + m, j)),
            pl.BlockSpec((None, 1, tn), lambda m, j: (tok.mod_row(tile0 + m), 0, gate_chunk * (d // tn) + j)),
        ],
        out_specs=pl.BlockSpec((tm, tn), lambda m, j: (tile0 + m, j)),
        input_output_aliases={2: 0},
        compiler_params=_params("parallel", "parallel"),
        name=name,
    )(a, w, x_all, mod)


def _ffn_kernel(x_ref, nw_ref, sh_ref, sc_ref, g_ref, w1_ref, w3_ref, w2_ref, o_ref, h_scr, acc_scr):
    f = pl.program_id(1)

    @pl.when(f == 0)
    def _():
        h_scr[...] = _norm_mod(x_ref[...], nw_ref[...], sh_ref[...], sc_ref[...]).astype(BF16)
        acc_scr[...] = jnp.zeros_like(acc_scr)

    h = h_scr[...]
    u = _silu(_dot(h, w1_ref[...])) * _dot(h, w3_ref[...])
    acc_scr[...] += _dot(u.astype(BF16), w2_ref[...])

    @pl.when(f == pl.num_programs(1) - 1)
    def _():
        o_ref[...] = x_ref[...] + g_ref[...] * acc_scr[...]


def _ffn(tok, n_tiles, x_all, norm_w, mod, w1, w3, w2, layer, in_place):
    _, d, dff = w1.shape
    tm = tok.tm
    tf = _pick(dff, (512, 256, 128))
    return pl.pallas_call(
        _ffn_kernel,
        out_shape=jax.ShapeDtypeStruct(x_all.shape if in_place else (n_tiles * tm, d), F32),
        grid=(n_tiles, dff // tf),
        in_specs=[
            pl.BlockSpec((tm, d), lambda m, f: (m, 0)),
            pl.BlockSpec((1, d), lambda m, f: (0, 0)),
            _mod_spec(tok, d, 3),
            _mod_spec(tok, d, 4),
            _mod_spec(tok, d, 5),
            pl.BlockSpec((None, d, tf), lambda m, f: (layer, 0, f)),
            pl.BlockSpec((None, d, tf), lambda m, f: (layer, 0, f)),
            pl.BlockSpec((None, tf, d), lambda m, f: (layer, f, 0)),
        ],
        out_specs=pl.BlockSpec((tm, d), lambda m, f: (m, 0)),
        scratch_shapes=[pltpu.VMEM((tm, d), BF16), pltpu.VMEM((tm, d), F32)],
        input_output_aliases={0: 0} if in_place else {},
        compiler_params=_params("parallel", "arbitrary"),
        name="ffn",
    )(x_all, norm_w.reshape(1, d), mod, mod, mod, w1, w3, w2)


def _na_bias_tables(rpb, rows):
    kh = min(WIN_H, rows)
    cols = np.arange(GRID_W)
    col_start = np.clip(cols - WIN_W // 2, 0, GRID_W - WIN_W)
    in_win = (cols[None, :] >= col_start[:, None]) & (cols[None, :] < col_start[:, None] + WIN_W)
    dc = np.clip(cols[None, :] - cols[:, None], 1 - WIN_W, WIN_W - 1) + WIN_W - 1
    off = np.zeros((3, ROW_GROUP, KEY_ROWS), np.int32)
    ok = np.zeros((3, ROW_GROUP, KEY_ROWS), bool)
    for t, r0 in enumerate((0, ROW_GROUP, rows - ROW_GROUP)):
        start = min(max(r0 - WIN_H // 2, 0), rows - KEY_ROWS)
        for i in range(ROW_GROUP):
            r = r0 + i
            rs = min(max(r - kh // 2, 0), rows - kh)
            for m in range(KEY_ROWS):
                j = start + m - rs
                ok[t, i, m] = 0 <= j < kh
                off[t, i, m] = min(max(rs - r + j + WIN_H - 1, 0), 2 * WIN_H - 2)
    heads = rpb.shape[0]
    onehot = ((dc[None] == np.arange(2 * WIN_W - 1)[:, None, None]) & in_win[None]).astype(np.float32)
    by_col = jnp.einsum("hod,dqk->hqok", rpb.astype(F32), onehot, precision=lax.Precision.HIGHEST)
    by_col = (by_col + np.where(in_win, 0.0, NEG_INF).astype(np.float32)[:, None, :]) * LOG2E
    by_col = by_col.reshape(heads, GRID_W, (2 * WIN_H - 1) * GRID_W)
    pieces = []
    for t in range(3):
        for i in range(ROW_GROUP):
            valid = np.nonzero(ok[t, i])[0]
            m0, m1 = int(valid[0]), int(valid[-1]) + 1
            o0 = int(off[t, i, m0])
            assert np.array_equal(valid, np.arange(m0, m1))
            assert np.array_equal(off[t, i, m0:m1], o0 + np.arange(m1 - m0))
            pieces.append(jnp.pad(by_col[:, :, o0 * GRID_W:(o0 + m1 - m0) * GRID_W],
                                  ((0, 0), (0, 0), (m0 * GRID_W, (KEY_ROWS - m1) * GRID_W)),
                                  constant_values=NEG_INF * LOG2E))
    return jnp.stack(pieces, axis=1).reshape(heads, 3, ROW_GROUP * GRID_W, KEY_ROWS * GRID_W)


def _softmax_pv(parts):
    m = functools.reduce(jnp.maximum, [jnp.max(s, axis=-1, keepdims=True) for s, _ in parts])
    ps = [jnp.exp2(s - m) for s, _ in parts]
    den = functools.reduce(jnp.add, [jnp.sum(p, axis=-1, keepdims=True) for p in ps])
    num = functools.reduce(jnp.add, [_dot(p.astype(BF16), v) for p, (_, v) in zip(ps, parts)])
    return num / den


def _na_kernel(q_ref, k_ref, v_ref, qc_ref, kc_ref, vc_ref, bias_ref, o_ref, oc_ref, *, rows):
    kc = kc_ref[...]
    vc = vc_ref[...]
    n_groups = rows // ROW_GROUP
    gq = ROW_GROUP * GRID_W
    gk = KEY_ROWS * GRID_W

    def group(gi, carry):
        r0 = gi * ROW_GROUP
        start = jnp.clip(r0 - WIN_H // 2, 0, rows - KEY_ROWS)
        table = jnp.where(gi == 0, 0, jnp.where(gi == n_groups - 1, 2, 1))
        q = q_ref[pl.ds(pl.multiple_of(r0 * GRID_W, gq), gq), :]
        koff = pl.multiple_of(start * GRID_W, GRID_W)
        ks = k_ref[pl.ds(koff, gk), :]
        vs = v_ref[pl.ds(koff, gk), :]
        s_loc = _dot_nt(q, ks) + bias_ref[table]
        s_ctx = _dot_nt(q, kc)
        o = _softmax_pv([(s_loc, vs), (s_ctx, vc)])
        o_ref[pl.ds(pl.multiple_of(r0 * GRID_W, gq), gq), :] = o.astype(o_ref.dtype)
        return carry

    lax.fori_loop(0, n_groups, group, 0, unroll=NA_GROUP_UNROLL)
    s_cc = _dot_nt(qc_ref[...], kc)
    oc_ref[...] = _softmax_pv([(s_cc, vc)]).astype(oc_ref.dtype)


def _na_attention(tok, qkv, bias_tables, d):
    heads = d // HEAD
    seq, ctx, batch = tok.seq, tok.ctx, tok.batch
    rows = seq // GRID_W
    assert rows % ROW_GROUP == 0 and rows >= KEY_ROWS
    ctx_blk0 = tok.n_lat // ctx
    gq, gk = ROW_GROUP * GRID_W, KEY_ROWS * GRID_W

    def lat(kind):
        return pl.BlockSpec((seq, HEAD), lambda h, b: (b, kind * heads + h))

    def cx(kind):
        return pl.BlockSpec((ctx, HEAD), lambda h, b: (ctx_blk0 + b, kind * heads + h))

    o_lat, o_ctx = pl.pallas_call(
        functools.partial(_na_kernel, rows=rows),
        out_shape=(jax.ShapeDtypeStruct((tok.n_lat, d), BF16),
                   jax.ShapeDtypeStruct((batch * ctx, d), BF16)),
        grid=(heads, batch),
        in_specs=[lat(0), lat(1), lat(2), cx(0), cx(1), cx(2),
                  pl.BlockSpec((None, 3, gq, gk), lambda h, b: (h, 0, 0, 0))],
        out_specs=(pl.BlockSpec((seq, HEAD), lambda h, b: (b, h)),
                   pl.BlockSpec((ctx, HEAD), lambda h, b: (b, h))),
        compiler_params=_params("parallel", "parallel"),
        name="na_attention",
    )(qkv, qkv, qkv, qkv, qkv, qkv, bias_tables)
    return o_lat, o_ctx


def _gdn_in_kernel(x_ref, xp_ref, xn_ref, nw_ref, sh_ref, sc_ref, w_ref, cw_ref, wba_ref, qkv_ref, z_ref, ba_ref,
                   h_scr, *acc_refs, tok, n_key_tiles, n_conv_tiles):
    m = pl.program_id(0)
    n = pl.program_id(1)
    tm = x_ref.shape[0]
    tn = w_ref.shape[1]
    heads_per_slab = NORM_SLAB // HEAD
    acc = [acc_refs[slot * heads_per_slab:(slot + 1) * heads_per_slab] for slot in range(2)]

    def put(slot, res):
        for i in range(heads_per_slab):
            acc[slot][i][...] = res[:, i * HEAD:(i + 1) * HEAD]

    @pl.when(n == 0)
    def _():
        nw, sh, sc = nw_ref[...], sh_ref[...], sc_ref[...]
        h_scr[pl.ds(0, CONV_PAD), :] = _norm_mod(xp_ref[...], nw, sh, sc).astype(BF16)
        h_scr[pl.ds(CONV_PAD, tm), :] = _norm_mod(x_ref[...], nw, sh, sc).astype(BF16)
        h_scr[pl.ds(CONV_PAD + tm, CONV_PAD), :] = _norm_mod(xn_ref[...], nw, sh, sc).astype(BF16)
        ba_ref[...] = _dot(h_scr[pl.ds(CONV_PAD, tm), :], wba_ref[...])

    @pl.when(n >= n_conv_tiles)
    def _():
        z_ref[...] = _dot(h_scr[pl.ds(CONV_PAD, tm), :], w_ref[...]).astype(z_ref.dtype)

    def conv_tile(normalise):
        seq_len = jnp.where(m < tok.lat_tiles, tok.seq, tok.ctx)
        pos = (m * tm + lax.broadcasted_iota(jnp.int32, (tm, 1), 0)) & (seq_len - 1)
        ok = [pos >= 1, None, pos <= seq_len - 2, pos <= seq_len - 3]
        period = min(tok.seq, tok.ctx, tm)
        edges = sorted({e for b in range(0, tm + 1, period) for e in (b - CONV_PAD, b) if 0 <= e < tm})
        scale = jnp.where(n < n_key_tiles, GDN_SCALE, 1.0)
        h = h_scr[...]
        n_slabs = tn // NORM_SLAB
        put(0, _dot(h, w_ref[:, :NORM_SLAB]))
        for j in range(n_slabs):
            cols = slice(j * NORM_SLAB, (j + 1) * NORM_SLAB)
            if j + 1 < n_slabs:
                put((j + 1) % 2, _dot(h, w_ref[:, (j + 1) * NORM_SLAB:(j + 2) * NORM_SLAB]))
            cw = cw_ref[:, cols]
            for i in range(heads_per_slab):
                def taps(row0, n_rows, masked):
                    y = None
                    for tap in range(CONV_TAPS):
                        rows = pl.ds(CONV_PAD - (CONV_TAPS - 1) // 2 + tap + row0, n_rows)
                        term = acc[j % 2][i][rows, :] * cw[tap:tap + 1, i * HEAD:(i + 1) * HEAD]
                        if masked and ok[tap] is not None:
                            term = jnp.where(ok[tap][row0:row0 + n_rows], term, 0.0)
                        y = term if y is None else y + term
                    return y

                parts, r = [], 0
                for e in edges:
                    if e > r:
                        parts.append(taps(r, e - r, False))
                    parts.append(taps(e, CONV_PAD, True))
                    r = e + CONV_PAD
                if r < tm:
                    parts.append(taps(r, tm - r, False))
                yh = _silu(jnp.concatenate(parts, axis=0))
                if normalise:
                    yh = yh * (lax.rsqrt(jnp.sum(yh * yh, axis=-1, keepdims=True) + EPS) * scale)
                lo = j * NORM_SLAB + i * HEAD
                qkv_ref[:, lo:lo + HEAD] = yh

    pl.when(n < 2 * n_key_tiles)(functools.partial(conv_tile, True))
    pl.when((n >= 2 * n_key_tiles) & (n < n_conv_tiles))(functools.partial(conv_tile, False))


def _gdn_in(tok, x_all, norm_w, mod, w, layer, conv_w, w_ba, key_dim):
    _, d, n = w.shape
    conv_dim = conv_w.shape[1]
    tm = tok.tm
    tn = _pick(np.gcd(key_dim, n - conv_dim), (1024, 512, 256))
    n_conv_tiles = conv_dim // tn
    halo_blocks = tm // CONV_PAD
    last_halo = tok.n_all // CONV_PAD - 1
    assert tok.seq & (tok.seq - 1) == 0 and tok.ctx & (tok.ctx - 1) == 0
    kern = functools.partial(_gdn_in_kernel, tok=tok, n_key_tiles=key_dim // tn, n_conv_tiles=n_conv_tiles)
    return pl.pallas_call(
        kern,
        out_shape=(jax.ShapeDtypeStruct((tok.n_all, conv_dim), F32),
                   jax.ShapeDtypeStruct((tok.n_all, n - conv_dim), BF16),
                   jax.ShapeDtypeStruct((tok.n_all, w_ba.shape[1]), F32)),
        grid=(tok.all_tiles, n // tn),
        in_specs=[
            pl.BlockSpec((tm, d), lambda m, j: (m, 0)),
            pl.BlockSpec((CONV_PAD, d), lambda m, j: (jnp.maximum(m * halo_blocks - 1, 0), 0)),
            pl.BlockSpec((CONV_PAD, d), lambda m, j: (jnp.minimum((m + 1) * halo_blocks, last_halo), 0)),
            pl.BlockSpec((1, d), lambda m, j: (0, 0)),
            _mod_spec(tok, d, 0),
            _mod_spec(tok, d, 1),
            pl.BlockSpec((None, d, tn), lambda m, j: (layer, 0, j)),
            pl.BlockSpec((CONV_TAPS, tn), lambda m, j: (0, jnp.minimum(j, n_conv_tiles - 1))),
            pl.BlockSpec(w_ba.shape, lambda m, j: (0, 0)),
        ],
        out_specs=(pl.BlockSpec((tm, tn), lambda m, j: (m, jnp.minimum(j, n_conv_tiles - 1))),
                   pl.BlockSpec((tm, tn), lambda m, j: (m, jnp.maximum(j - n_conv_tiles, 0))),
                   pl.BlockSpec((tm, w_ba.shape[1]), lambda m, j: (m, 0))),
        scratch_shapes=[pltpu.VMEM((tm + 2 * CONV_PAD, d), BF16)]
        + [pltpu.VMEM((tm + 2 * CONV_PAD, HEAD), F32)] * (2 * NORM_SLAB // HEAD),
        compiler_params=_params("parallel", "arbitrary"),
        name="gdn_in",
    )(x_all, x_all, x_all, norm_w.reshape(1, d), mod, mod, w, conv_w, w_ba)


def _gates_kernel(ba_ref, alog_ref, dt_ref, o_ref, *, n_chunks, n_vheads):
    row = lax.broadcasted_iota(jnp.int32, (CHUNK, CHUNK), 0)
    col = lax.broadcasted_iota(jnp.int32, (CHUNK, CHUNK), 1)
    prefix = (row <= col).astype(F32)
    suffix = (row >= col).astype(F32)
    is_decay = (row % (2 * n_vheads)) >= n_vheads
    neg_a = -jnp.exp(alog_ref[...])
    for c in range(n_chunks):
        bt = ba_ref[pl.ds(c * CHUNK, CHUNK), :].T
        beta = 1.0 / (1.0 + jnp.exp(-bt))
        x = bt + dt_ref[...]
        g = neg_a * (jnp.maximum(x, 0.0) + jnp.log1p(jnp.exp(-jnp.abs(x))))
        cum = jnp.where(row < 2 * n_vheads, _dot_hi(g, prefix), _dot_hi(g, suffix))
        o_ref[c] = jnp.where(is_decay, cum, beta)


def _gdn_gates(ba, a_log, dt_bias):
    n_tok, n_col = ba.shape
    n_vheads = a_log.shape[1]
    assert n_col == CHUNK and 4 * n_vheads <= n_col
    zeros = jnp.zeros_like(a_log)
    pad = ((0, n_col - 4 * n_vheads), (0, 0))
    alog_col = jnp.pad(jnp.concatenate([zeros, a_log], axis=1).reshape(4 * n_vheads, 1), pad)
    dt_col = jnp.pad(jnp.concatenate([zeros, dt_bias], axis=1).reshape(4 * n_vheads, 1), pad)
    per_step = 4
    assert (n_tok // CHUNK) % per_step == 0
    return pl.pallas_call(
        functools.partial(_gates_kernel, n_chunks=per_step, n_vheads=n_vheads),
        out_shape=jax.ShapeDtypeStruct((n_tok // CHUNK, n_col, CHUNK), F32),
        grid=(n_tok // (CHUNK * per_step),),
        in_specs=[pl.BlockSpec((CHUNK * per_step, n_col), lambda i: (i, 0)),
                  pl.BlockSpec((n_col, 1), lambda i: (0, 0)),
                  pl.BlockSpec((n_col, 1), lambda i: (0, 0))],
        out_specs=pl.BlockSpec((per_step, n_col, CHUNK), lambda i: (i, 0, 0)),
        compiler_params=_params("parallel"),
        name="gdn_gates",
    )(ba, alog_col, dt_col)


def _bdot(a, b):
    return jnp.einsum("pmk,pkn->pmn", a, b, preferred_element_type=F32)


def _bdot_nt(a, b):
    return jnp.einsum("pmk,pnk->pmn", a, b, preferred_element_type=F32)


def _unit_tri_inverse(l, row, col, tick):
    n = l.shape[-1]
    b = 2
    x = (row == col).astype(F32) - jnp.where(row // b == col // b, l, 0.0)
    while b < n:
        pair = (row // (2 * b) == col // (2 * b)) & (row // b != col // b)
        c16 = jnp.where(pair, l, 0.0).astype(BF16)
        x16 = x.astype(BF16)
        x = x - _bdot(x16, _bdot(c16, x16).astype(BF16))
        tick()
        b *= 2
    return x


def _chunk_factors(q, k, v, beta_r, g_r, row, col, reverse, tick):
    p = q.shape[0]

    def per_row(r):
        return jnp.stack([jnp.broadcast_to(r[c], (CHUNK, CHUNK)).T for c in range(p)], axis=0)

    incl = (row <= col) if reverse else (row >= col)
    strict = (row < col) if reverse else (row > col)
    beta_c = per_row(beta_r)
    g_c = per_row(g_r)
    decay = jnp.where(incl, jnp.exp(jnp.where(incl, g_c - g_r, 0.0)), 0.0)
    k16 = k.astype(BF16)
    k_beta = k * beta_c
    v_beta = v * beta_c
    l_mat = jnp.where(strict, _bdot_nt(k_beta.astype(BF16), k16) * decay, 0.0)
    tick()
    t_inv = _unit_tri_inverse(l_mat, row, col, tick)
    exp_g = jnp.exp(g_c)
    rhs = jnp.concatenate([k_beta * exp_g, v_beta], axis=2).astype(BF16)
    wu16 = _bdot(t_inv.astype(BF16), rhs).astype(BF16)
    tick()
    last = 0 if reverse else CHUNK - 1
    g_last = g_c[:, last:last + 1, :]
    kd = k * jnp.exp(g_last - g_c)
    kd_t = jnp.stack([kd[c].T for c in range(p)], axis=0).astype(BF16)
    a16 = jnp.where(incl, _bdot_nt(q.astype(BF16), k16) * decay, 0.0).astype(BF16)
    r = _bdot(jnp.concatenate([kd_t, a16], axis=1), wu16)
    tick()
    qe = q * exp_g - r[:, CHUNK:, :HEAD]
    mq = jnp.concatenate([-r[:, :CHUNK, :HEAD], qe], axis=1).astype(BF16)
    return mq, r[:, :, HEAD:], jnp.exp(g_last)


def _gdn_core_kernel(qf_ref, kf_ref, vf_ref, qb_ref, kb_ref, vb_ref, gf_ref, gb_ref, s0_ref, z_ref, nw_ref,
                     y_ref, sf_ref, s_scr, o_scr, *, n_vheads, rep, batch_heads):
    grp = pl.program_id(1)
    t = pl.program_id(2)
    n_t = pl.num_programs(2)
    tb = qf_ref.shape[0]
    cb = tb // CHUNK
    heads = s_scr.shape[0]
    row = lax.broadcasted_iota(jnp.int32, (CHUNK, CHUNK), 0)
    col = lax.broadcasted_iota(jnp.int32, (CHUNK, CHUNK), 1)

    @pl.when(t == 0)
    def _():
        s_scr[...] = s0_ref[...]

    pending = []

    def tick():
        if pending:
            pending.pop(0)()

    def recurrence(z, hs, mq, na, egl):
        state = {h: s_scr[h, z] for h in hs}
        blk = t if z == 0 else n_t - 1 - t

        def step(c):
            for j, h in enumerate(hs):
                s = state[h]
                r = _dot(mq[j * cb + c], s.astype(BF16)) + na[j * cb + c]
                o_scr[h, z, pl.ds(pl.multiple_of(blk * tb + c * CHUNK, CHUNK), CHUNK), :] = r[CHUNK:]
                state[h] = s * egl[j * cb + c] + r[:CHUNK]

        def finish():
            for h in hs:
                s_scr[h, z] = state[h]

        order = list(range(cb)) if z == 0 else list(reversed(range(cb)))
        return [functools.partial(step, c) for c in order] + [finish]

    def chunks(ref, lanes):
        return ref[:, lanes].reshape(cb, CHUNK, HEAD)

    for h0 in range(0, heads, batch_heads):
        hs = list(range(h0, h0 + batch_heads))
        for z in range(2):
            q_ref, k_ref, v_ref, gates_ref = (qf_ref, kf_ref, vf_ref, gf_ref) if z == 0 else (qb_ref, kb_ref, vb_ref, gb_ref)
            ksl = [slice((h // rep) * HEAD, (h // rep + 1) * HEAD) for h in hs]
            gate0 = z * 2 * n_vheads + grp * heads
            mq, na, egl = _chunk_factors(
                jnp.concatenate([chunks(q_ref, sl) for sl in ksl], axis=0),
                jnp.concatenate([chunks(k_ref, sl) for sl in ksl], axis=0),
                jnp.concatenate([chunks(v_ref, slice(h * HEAD, (h + 1) * HEAD)) for h in hs], axis=0),
                jnp.concatenate([gates_ref[:, pl.ds(gate0 + h, 1), :] for h in hs], axis=0),
                jnp.concatenate([gates_ref[:, pl.ds(gate0 + n_vheads + h, 1), :] for h in hs], axis=0),
                row, col, reverse=(z == 1), tick=tick)
            while pending:
                tick()
            pending.extend(recurrence(z, hs, mq, na, egl))
    while pending:
        tick()

    @pl.when(t == n_t - 1)
    def _():
        sf_ref[...] = s_scr[...]
        for hh in range(heads):
            o = o_scr[hh, 0] + o_scr[hh, 1]
            ms = jnp.mean(o * o, axis=-1, keepdims=True)
            zz = z_ref[:, hh * HEAD:(hh + 1) * HEAD].astype(F32)
            y = (o * lax.rsqrt(ms + EPS) * nw_ref[...]) * _silu(zz)
            y_ref[:, hh * HEAD:(hh + 1) * HEAD] = y.astype(y_ref.dtype)


def _gdn_core(qkv_conv, gates, s0, z_gate, norm_w, n_seq, seq_len, blk0, chunk0, key_dim, n_vheads, heads,
              batch_heads):
    tb = min(seq_len, PREP_ROWS)
    n_tb = seq_len // tb
    cb = tb // CHUNK
    nk = key_dim // HEAD
    rep = n_vheads // nk
    kw = max(1, heads // rep)
    assert (heads == 1 or heads % rep == 0) and heads % batch_heads == 0
    assert n_vheads % heads == 0 and nk % kw == 0 and (2 * nk) % heads == 0 and chunk0 == blk0 * n_tb * cb

    def fwd(s, g, t):
        return (blk0 + s) * n_tb + t

    def bwd(s, g, t):
        return (blk0 + s) * n_tb + n_tb - 1 - t

    def specs(tblk):
        return [
            pl.BlockSpec((tb, kw * HEAD), lambda s, g, t: (tblk(s, g, t), (g * heads) // (rep * kw))),
            pl.BlockSpec((tb, kw * HEAD), lambda s, g, t: (tblk(s, g, t), nk // kw + (g * heads) // (rep * kw))),
            pl.BlockSpec((tb, heads * HEAD), lambda s, g, t: (tblk(s, g, t), 2 * nk // heads + g)),
        ]

    def gate_spec(tblk):
        return pl.BlockSpec((cb, CHUNK, CHUNK), lambda s, g, t: (tblk(s, g, t), 0, 0))

    state_spec = pl.BlockSpec((None, heads, 2, HEAD, HEAD), lambda s, g, t: (s, g, 0, 0, 0))
    return pl.pallas_call(
        functools.partial(_gdn_core_kernel, n_vheads=n_vheads, rep=rep, batch_heads=batch_heads),
        out_shape=(jax.ShapeDtypeStruct((n_seq * seq_len, n_vheads * HEAD), BF16),
                   jax.ShapeDtypeStruct((n_seq, n_vheads, 2, HEAD, HEAD), F32)),
        grid=(n_seq, n_vheads // heads, n_tb),
        in_specs=specs(fwd) + specs(bwd) + [gate_spec(fwd), gate_spec(bwd), state_spec,
                  pl.BlockSpec((seq_len, heads * HEAD), lambda s, g, t: (blk0 + s, g)),
                  pl.BlockSpec((1, HEAD), lambda s, g, t: (0, 0))],
        out_specs=(pl.BlockSpec((seq_len, heads * HEAD), lambda s, g, t: (s, g)), state_spec),
        scratch_shapes=[pltpu.VMEM((heads, 2, HEAD, HEAD), F32), pltpu.VMEM((heads, 2, seq_len, HEAD), F32)],
        compiler_params=_params("parallel", "parallel", "arbitrary"),
        name=f"gdn_core_{seq_len}",
    )(qkv_conv, qkv_conv, qkv_conv, qkv_conv, qkv_conv, qkv_conv, gates, gates, s0, z_gate, norm_w.reshape(1, HEAD))


def _gdn_mixer(tok, qkv_conv, z_gate, ba, key_dim, a_log, dt_bias, norm_w):
    n_vheads = z_gate.shape[1] // HEAD
    batch, seq, ctx = tok.batch, tok.seq, tok.ctx
    gates = _gdn_gates(ba, a_log, dt_bias)

    def run(n_seq, seq_len, blk0, s0, heads, batch_heads):
        return _gdn_core(qkv_conv, gates, s0, z_gate, norm_w, n_seq, seq_len, blk0, blk0 * seq_len // CHUNK,
                         key_dim, n_vheads, heads, batch_heads)

    zero_state = jnp.zeros((batch, n_vheads, 2, HEAD, HEAD), F32)
    ctx_heads = min(CTX_HEADS_PER_STEP, n_vheads)
    y_ctx, s_ctx = run(batch, ctx, tok.n_lat // ctx, zero_state, ctx_heads, ctx_heads)
    y_lat, _ = run(batch, seq, 0, s_ctx, LAT_HEADS_PER_STEP, LAT_HEADS_PER_STEP)
    return y_lat, y_ctx


def kernel(x, c, ctx, c_ctx, w_ada, b_ada, norm1_w, norm2_w, na_w_qkv, na_w_o, na_q_gain, na_k_gain, na_rpb, gdn_w_in, gdn_conv_w, gdn_w_ba, gdn_a_log, gdn_dt_bias, gdn_norm_w, gdn_w_o, ffn_w1, ffn_w3, ffn_w2):
    batch, seq, d = x.shape
    n_ctx = ctx.shape[1]
    depth = w_ada.shape[0]
    tok = _Tokens(batch, seq, n_ctx)
    assert batch + 1 <= 8

    x_all = jnp.concatenate([x.reshape(batch * seq, d), ctx.reshape(batch * n_ctx, d)], axis=0)
    cond = jnp.concatenate([c, c_ctx[None, :], jnp.zeros((8 - batch - 1, d), F32)], axis=0)
    mods = _ada(cond, w_ada, b_ada)

    na_qkv16, na_o16 = na_w_qkv.astype(BF16), na_w_o.astype(BF16)
    gdn_in16, gdn_o16 = gdn_w_in.astype(BF16), gdn_w_o.astype(BF16)
    w1_16, w3_16, w2_16 = ffn_w1.astype(BF16), ffn_w3.astype(BF16), ffn_w2.astype(BF16)

    for i in range(depth):
        update_ctx = i < depth - 1
        n_tiles = tok.all_tiles if update_ctx else tok.lat_tiles
        mod = mods[i].reshape(8, 1, 6 * d)
        j = i // 2
        if i % 2 == 0:
            qkv = _na_qkv(tok, x_all, norm1_w[i], mod, na_qkv16, j, na_q_gain[j], na_k_gain[j])
            tables = _na_bias_tables(na_rpb[j], seq // GRID_W)
            mix_lat, mix_ctx = _na_attention(tok, qkv, tables, d)
            w_o = na_o16
        else:
            conv_dim = gdn_conv_w.shape[2]
            key_dim = (2 * conv_dim - gdn_w_in.shape[2]) // 2
            w_ba = jnp.concatenate([gdn_w_ba[j, 0], gdn_w_ba[j, 1]], axis=1).astype(BF16)
            w_ba = jnp.pad(w_ba, ((0, 0), (0, CHUNK - w_ba.shape[1])))
            qkv_conv, z_gate, ba = _gdn_in(tok, x_all, norm1_w[i], mod, gdn_in16, j, gdn_conv_w[j], w_ba, key_dim)
            mix_lat, mix_ctx = _gdn_mixer(tok, qkv_conv, z_gate, ba, key_dim, gdn_a_log[j], gdn_dt_bias[j],
                                          gdn_norm_w[j])
            w_o = gdn_o16
        x_all = _matmul_residual(tok, 0, mix_lat, w_o, j, x_all, mod, 2, "mixer_out")
        if update_ctx:
            x_all = _matmul_residual(tok, tok.lat_tiles, mix_ctx, w_o, j, x_all, mod, 2, "mixer_out_ctx")
        x_all = _ffn(tok, n_tiles, x_all, norm2_w[i], mod, w1_16, w3_16, w2_16, i, in_place=update_ctx)
    return x_all.reshape(batch, seq, d)
```
